```python
import math
import jax, jax.numpy as jnp
from jax import lax
import numpy as np

D_MODEL = 1024
BATCH = 2
SEQ = 8192
DEPTH = 1

CHUNK = 64
D_MIX = 2 * D_MODEL
SSD_WIDTH = D_MIX // 2
SSD_HEAD_DIM = 64
SSD_HEADS = SSD_WIDTH // SSD_HEAD_DIM
SSD_GROUPS = 2
SSD_HPG = SSD_HEADS // SSD_GROUPS
SSD_STATE = 128
SSD_CONV = 4
SSD_CONV_CH = SSD_WIDTH + 2 * SSD_GROUPS * SSD_STATE
SC_WIDTH = D_MIX - SSD_WIDTH
SC_GROUP_DIM = 64
SC_GROUPS = SC_WIDTH // SC_GROUP_DIM
SC_CONV = 3
IN_COLS = SSD_WIDTH + SSD_CONV_CH + SSD_HEADS + 3 * SC_WIDTH
N_EXPERT_GROUPS = 4
EXPERTS_PER_GROUP = 8
N_EXPERTS = N_EXPERT_GROUPS * EXPERTS_PER_GROUP
TOP_K_IN_GROUP = 2
D_FF_EXPERT = D_MODEL // 2
EXPERT_BLOCK = 128
EPS = 1e-6

kernel_name = "hymba_ssd_shortconv_hiermoe_layer"


def rmsnorm(x, g):
    xf = x.astype(jnp.float32)
    y = xf * lax.rsqrt(jnp.mean(xf * xf, axis=-1, keepdims=True) + EPS)
    return (y * g.astype(jnp.float32)).astype(x.dtype)


def grouped_rmsnorm(x, g, n_groups):
    shp = x.shape
    xf = x.astype(jnp.float32).reshape(*shp[:-1], n_groups, shp[-1] // n_groups)
    y = xf * lax.rsqrt(jnp.mean(xf * xf, axis=-1, keepdims=True) + EPS)
    return (y.reshape(shp) * g.astype(jnp.float32)).astype(x.dtype)


def causal_depthwise_conv(x, w):
    k = w.shape[0]
    return lax.conv_general_dilated(
        x, w[:, None, :].astype(x.dtype), window_strides=(1,), padding=[(k - 1, 0)],
        dimension_numbers=("NWC", "WIO", "NWC"), feature_group_count=x.shape[-1])


def ssd_chunked_scan(x, dt, a, bmat, cmat):
    bsz, seqlen = x.shape[0], x.shape[1]
    nc = seqlen // CHUNK
    f32 = jnp.float32
    xdt = (x.astype(f32) * dt[..., None]).reshape(bsz, nc, CHUNK, SSD_GROUPS, SSD_HPG, SSD_HEAD_DIM)
    bc = bmat.astype(f32).reshape(bsz, nc, CHUNK, SSD_GROUPS, SSD_STATE)
    cc = cmat.astype(f32).reshape(bsz, nc, CHUNK, SSD_GROUPS, SSD_STATE)
    a_dt = (dt * a.astype(f32)).reshape(bsz, nc, CHUNK, SSD_GROUPS, SSD_HPG)
    ac = jnp.cumsum(a_dt, axis=2).transpose(0, 1, 3, 4, 2)
    seg = ac[..., :, None] - ac[..., None, :]
    causal = jnp.tril(jnp.ones((CHUNK, CHUNK), dtype=bool))
    decay_in = jnp.exp(jnp.where(causal, seg, -jnp.inf))
    cb = jnp.einsum("bclgn,bcsgn->bcgls", cc, bc)
    y_diag = jnp.einsum("bcgls,bcgrls,bcsgrp->bclgrp", cb, decay_in, xdt)
    decay_to_end = jnp.exp(ac[..., -1:] - ac)
    states = jnp.einsum("bclgn,bcgrl,bclgrp->bcgrpn", bc, decay_to_end, xdt)
    chunk_decay = jnp.exp(ac[..., -1])

    def step(carry, inp):
        st, dec = inp
        return carry * dec[..., None, None] + st, carry

    init = jnp.zeros((bsz, SSD_GROUPS, SSD_HPG, SSD_HEAD_DIM, SSD_STATE), f32)
    _, prev = lax.scan(step, init, (jnp.moveaxis(states, 1, 0), jnp.moveaxis(chunk_decay, 1, 0)))
    prev = jnp.moveaxis(prev, 0, 1)
    y_off = jnp.einsum("bclgn,bcgrpn,bcgrl->bclgrp", cc, prev, jnp.exp(ac))
    return (y_diag + y_off).reshape(bsz, seqlen, SSD_HEADS, SSD_HEAD_DIM)


def hybrid_mixer(hn, w_in, ssd_conv_w, ssd_conv_b, dt_bias, a_log, d_skip, ssd_norm,
                 sc_conv_w, sc_norm, w_out):
    bsz, seqlen, _ = hn.shape
    proj = hn @ w_in
    o1 = SSD_WIDTH
    o2 = o1 + SSD_CONV_CH
    o3 = o2 + SSD_HEADS
    o4 = o3 + SC_WIDTH
    o5 = o4 + SC_WIDTH
    z, xbc, dt_raw, sc_b, sc_c, sc_v = jnp.split(proj, [o1, o2, o3, o4, o5], axis=-1)
    xbc = jax.nn.silu(causal_depthwise_conv(xbc, ssd_conv_w) + ssd_conv_b)
    xs, bm, cm = jnp.split(xbc, [SSD_WIDTH, SSD_WIDTH + SSD_GROUPS * SSD_STATE], axis=-1)
    dt = jax.nn.softplus(dt_raw.astype(jnp.float32) + dt_bias.astype(jnp.float32))
    a = -jnp.exp(a_log.astype(jnp.float32))
    xh = xs.reshape(bsz, seqlen, SSD_HEADS, SSD_HEAD_DIM)
    y = ssd_chunked_scan(xh, dt, a,
                         bm.reshape(bsz, seqlen, SSD_GROUPS, SSD_STATE),
                         cm.reshape(bsz, seqlen, SSD_GROUPS, SSD_STATE))
    y = y + d_skip.astype(jnp.float32)[:, None] * xh.astype(jnp.float32)
    y = y.reshape(bsz, seqlen, SSD_WIDTH).astype(hn.dtype)
    ssd_out = grouped_rmsnorm(y * jax.nn.silu(z), ssd_norm, SSD_GROUPS)
    sc = sc_b * causal_depthwise_conv(sc_c * sc_v, sc_conv_w)
    sc_out = grouped_rmsnorm(sc, sc_norm, SC_GROUPS)
    return jnp.concatenate([ssd_out, sc_out], axis=-1) @ w_out


def hierarchical_moe(hn, w_router_group, w_router_expert, w_gate, w_up, w_down):
    bsz, seqlen, d = hn.shape
    tokens = hn.reshape(-1, d)
    n = tokens.shape[0]
    g_logits = (tokens @ w_router_group).astype(jnp.float32)
    g_probs = jax.nn.softmax(g_logits, axis=-1)
    g_idx = jnp.argmax(g_logits, axis=-1)
    g_w = jnp.take_along_axis(g_probs, g_idx[:, None], axis=1)
    e_logits = jnp.einsum("nd,gde->nge", tokens, w_router_expert).astype(jnp.float32)
    e_sel = jnp.take_along_axis(e_logits, g_idx[:, None, None], axis=1)[:, 0]
    top_v, top_i = lax.top_k(e_sel, TOP_K_IN_GROUP)
    gates = g_w * jax.nn.softmax(top_v, axis=-1)
    expert_id = (g_idx[:, None] * EXPERTS_PER_GROUP + top_i).reshape(-1).astype(jnp.int32)
    token_id = jnp.repeat(jnp.arange(n, dtype=jnp.int32), TOP_K_IN_GROUP)
    gate_flat = gates.reshape(-1)
    m = n * TOP_K_IN_GROUP
    order = jnp.argsort(expert_id)
    sorted_e = expert_id[order]
    counts = jnp.bincount(expert_id, length=N_EXPERTS)
    padded_counts = ((counts + EXPERT_BLOCK - 1) // EXPERT_BLOCK) * EXPERT_BLOCK
    padded_end = jnp.cumsum(padded_counts)
    padded_start = padded_end - padded_counts
    start = jnp.cumsum(counts) - counts
    rank = jnp.arange(m, dtype=jnp.int32) - start[sorted_e]
    dest = padded_start[sorted_e] + rank
    p_rows = ((m + N_EXPERTS * EXPERT_BLOCK + EXPERT_BLOCK - 1) // EXPERT_BLOCK) * EXPERT_BLOCK
    n_blocks = p_rows // EXPERT_BLOCK
    row_token = jnp.zeros((p_rows,), jnp.int32).at[dest].set(token_id[order])
    row_gate = jnp.zeros((p_rows,), jnp.float32).at[dest].set(gate_flat[order])
    block_expert = jnp.clip(
        jnp.searchsorted(padded_end, jnp.arange(n_blocks) * EXPERT_BLOCK, side="right"),
        0, N_EXPERTS - 1).astype(jnp.int32)
    xs = tokens[row_token].reshape(n_blocks, EXPERT_BLOCK, d)

    def run_block(args):
        xb, e = args
        return (jax.nn.silu(xb @ w_gate[e]) * (xb @ w_up[e])) @ w_down[e]

    ys = lax.map(run_block, (xs, block_expert)).reshape(p_rows, d)
    out = jnp.zeros((n, d), jnp.float32).at[row_token].add(ys.astype(jnp.float32) * row_gate[:, None])
    return out.reshape(bsz, seqlen, d).astype(hn.dtype)


def setup_inputs(seed: int = 0) -> dict:
    key = jax.random.key(seed)
    ks = jax.random.split(key, 24)
    f32 = jnp.float32

    def nrm(k, shape, scale):
        return jax.random.normal(k, shape, f32) * scale

    x = nrm(ks[0], (BATCH, SEQ, D_MODEL), 1.0)
    norm_mix = 1.0 + nrm(ks[1], (DEPTH, D_MODEL), 0.02)
    w_in = nrm(ks[2], (DEPTH, D_MODEL, IN_COLS), D_MODEL ** -0.5)
    ssd_conv_w = nrm(ks[3], (DEPTH, SSD_CONV, SSD_CONV_CH), SSD_CONV ** -0.5)
    ssd_conv_b = nrm(ks[4], (DEPTH, SSD_CONV_CH), 0.01)
    dt0 = jnp.exp(jax.random.uniform(ks[5], (DEPTH, SSD_HEADS), f32,
                                     minval=math.log(1e-3), maxval=math.log(1e-1)))
    dt_bias = dt0 + jnp.log(-jnp.expm1(-dt0))
    a_log = jnp.log(jax.random.uniform(ks[6], (DEPTH, SSD_HEADS), f32, minval=1.0, maxval=16.0))
    d_skip = 1.0 + nrm(ks[7], (DEPTH, SSD_HEADS), 0.1)
    ssd_norm = 1.0 + nrm(ks[8], (DEPTH, SSD_WIDTH), 0.02)
    sc_conv_w = nrm(ks[9], (DEPTH, SC_CONV, SC_WIDTH), SC_CONV ** -0.5)
    sc_norm = 1.0 + nrm(ks[10], (DEPTH, SC_WIDTH), 0.02)
    w_out = nrm(ks[11], (DEPTH, D_MIX, D_MODEL), D_MIX ** -0.5)
    norm_ffn = 1.0 + nrm(ks[12], (DEPTH, D_MODEL), 0.02)
    w_router_group = nrm(ks[13], (DEPTH, D_MODEL, N_EXPERT_GROUPS), D_MODEL ** -0.5)
    w_router_expert = nrm(ks[14], (DEPTH, N_EXPERT_GROUPS, D_MODEL, EXPERTS_PER_GROUP), D_MODEL ** -0.5)
    w_gate = nrm(ks[15], (DEPTH, N_EXPERTS, D_MODEL, D_FF_EXPERT), D_MODEL ** -0.5)
    w_up = nrm(ks[16], (DEPTH, N_EXPERTS, D_MODEL, D_FF_EXPERT), D_MODEL ** -0.5)
    w_down = nrm(ks[17], (DEPTH, N_EXPERTS, D_FF_EXPERT, D_MODEL), D_FF_EXPERT ** -0.5)
    final_norm = 1.0 + nrm(ks[18], (D_MODEL,), 0.02)
    return {"x": x, "norm_mix": norm_mix, "w_in": w_in, "ssd_conv_w": ssd_conv_w,
            "ssd_conv_b": ssd_conv_b, "dt_bias": dt_bias, "a_log": a_log, "d_skip": d_skip,
            "ssd_norm": ssd_norm, "sc_conv_w": sc_conv_w, "sc_norm": sc_norm, "w_out": w_out,
            "norm_ffn": norm_ffn, "w_router_group": w_router_group,
            "w_router_expert": w_router_expert, "w_gate": w_gate, "w_up": w_up,
            "w_down": w_down, "final_norm": final_norm}


def reference(x, norm_mix, w_in, ssd_conv_w, ssd_conv_b, dt_bias, a_log, d_skip, ssd_norm,
              sc_conv_w, sc_norm, w_out, norm_ffn, w_router_group, w_router_expert,
              w_gate, w_up, w_down, final_norm):
    h = x
    for i in range(DEPTH):
        hn = rmsnorm(h, norm_mix[i])
        h = h + hybrid_mixer(hn, w_in[i], ssd_conv_w[i], ssd_conv_b[i], dt_bias[i], a_log[i],
                             d_skip[i], ssd_norm[i], sc_conv_w[i], sc_norm[i], w_out[i])
        hn = rmsnorm(h, norm_ffn[i])
        h = h + hierarchical_moe(hn, w_router_group[i], w_router_expert[i],
                                 w_gate[i], w_up[i], w_down[i])
    return rmsnorm(h, final_norm)
```

```python
import functools

import jax
import jax.numpy as jnp
from jax import lax
from jax.experimental import pallas as pl
from jax.experimental.pallas import tpu as pltpu

F32 = jnp.float32
BF16 = jnp.bfloat16
I32 = jnp.int32

EPS = 1e-6
D_MODEL = 1024
N_HEADS = 16
HEAD_DIM = 64
N_BC_GROUPS = 2
STATE = 128
SSD_WIDTH = 1024
XBC = SSD_WIDTH + 2 * N_BC_GROUPS * STATE
SC_WIDTH = 1024
SC_GROUPS = 16
N_GROUPS = 4
EPG = 8
N_EXPERTS = 32
D_FF = 512

LANES = 128
SUBLANES = 8
ROW_TILES = D_MODEL // LANES

T_MIX = 256
R_BLK = 128
T_CMB = 256

VMEM_LIMIT = 56 * 1024 * 1024


def _dot(a, b):
    return jnp.dot(a, b, preferred_element_type=F32)


def _split3(v):
    p1 = v.astype(BF16).astype(F32)
    r1 = v - p1
    p2 = r1.astype(BF16).astype(F32)
    p3 = (r1 - p2).astype(BF16).astype(F32)
    return p1, p2, p3


def _pack3(v):
    p1, p2, p3 = _split3(v)
    return (p1 + pltpu.roll(p2, 16, 1) + pltpu.roll(p3, 32, 1)).astype(BF16)


def _silu(v):
    return v * jax.nn.sigmoid(v)


def _mixer_kernel(x_ref, gmix_ref, wbig_ref, wdt_ref, convw_ref, convb_ref, dtb_ref, alog_ref,
                  dskip_ref, ssdn_ref, scw_ref, scn_ref, wout_ref, gffn_ref, wrh_ref, wrl_ref,
                  tri_ref, e3_ref, gsum_ref,
                  h_ref, hn2_ref, e_ref, g_ref,
                  cbuf, sbuf, st_ref):
    T = T_MIX
    t = pl.program_id(1)

    @pl.when(t == 0)
    def _():
        cbuf[0:8, :] = jnp.zeros((8, XBC), F32)
        sbuf[0:8, :] = jnp.zeros((8, SC_WIDTH), F32)
        st_ref[...] = jnp.zeros(st_ref.shape, F32)

    @pl.when(t > 0)
    def _():
        cbuf[0:8, :] = cbuf[T:T + 8, :]
        sbuf[0:8, :] = sbuf[T:T + 8, :]

    x = x_ref[0]
    ms = jnp.mean(x * x, axis=-1, keepdims=True)
    hn = (x * lax.rsqrt(ms + EPS) * gmix_ref[...]).astype(BF16)

    z = _dot(hn, wbig_ref[:, 0:1024])
    xbc = _dot(hn, wbig_ref[:, 1024:2560])
    scb = _dot(hn, wbig_ref[:, 2560:3584])
    scc = _dot(hn, wbig_ref[:, 3584:4608])
    scv = _dot(hn, wbig_ref[:, 4608:5632])
    dt_raw = _dot(hn, wdt_ref[...])

    cbuf[8:8 + T, :] = xbc
    cw = convw_ref[...]
    acc = convb_ref[...] + cw[3:4, :] * xbc
    for k in range(3):
        acc = acc + cw[k:k + 1, :] * cbuf[5 + k:5 + k + T, :]
    xact = _silu(acc)
    xs = xact[:, 0:SSD_WIDTH]

    lane = lax.broadcasted_iota(I32, (1, LANES), 1)
    hmask = lane < N_HEADS
    a = jnp.where(hmask, -jnp.exp(alog_ref[...]), 0.0)
    dtv = dt_raw + dtb_ref[...]
    dt = jnp.where(hmask, jnp.maximum(dtv, 0.0) + jnp.log1p(jnp.exp(-jnp.abs(dtv))), 0.0)
    adt = dt * a
    c3 = _dot(tri_ref[...], _pack3(adt))
    ac = jnp.where(hmask, c3 + pltpu.roll(c3, LANES - 16, 1) + pltpu.roll(c3, LANES - 32, 1), 0.0)
    ac_last = ac[T - 1:T, :]
    eac = jnp.where(hmask, jnp.exp(ac), 0.0)
    wdt = dt * jnp.exp(ac_last - ac)

    stacked = jnp.concatenate([_pack3(dt), _pack3(wdt), _pack3(eac)], axis=0)
    ex = _dot(stacked, e3_ref[...])
    dt_e = ex[0:T]
    wdt_e = ex[T:2 * T]
    eac_e = ex[2 * T:3 * T]
    xdt = (xs * dt_e).astype(BF16)
    xdtw = (xs * wdt_e).astype(BF16)

    ac_rows = ac.T
    rr = lax.broadcasted_iota(I32, (T, T), 0)
    cc = lax.broadcasted_iota(I32, (T, T), 1)
    causal = rr >= cc
    cblk = lax.shift_right_logical(lax.broadcasted_iota(I32, (T, 4 * HEAD_DIM), 1), 6)

    y_parts = []
    for g in range(N_BC_GROUPS):
        bg = xact[:, SSD_WIDTH + STATE * g:SSD_WIDTH + STATE * (g + 1)]
        cg = xact[:, SSD_WIDTH + 2 * STATE + STATE * g:SSD_WIDTH + 2 * STATE + STATE * (g + 1)]
        bb = bg.astype(BF16)
        cbf = cg.astype(BF16)
        cb = lax.dot_general(cbf, bb, (((1,), (1,)), ((), ())), preferred_element_type=F32)
        st = st_ref[g]
        yoff = _dot(cbf, st.astype(BF16))
        dec = eac_e[T - 1:T, 512 * g:512 * (g + 1)]
        bgt = bg.T.astype(BF16)
        st_ref[g] = st * dec + _dot(bgt, xdtw[:, 512 * g:512 * (g + 1)])
        for q in range(2):
            ms_list = []
            for r in range(4):
                hh = 8 * g + 4 * q + r
                seg = ac[:, hh:hh + 1] - ac_rows[hh:hh + 1, :]
                lh = jnp.exp(jnp.where(causal, seg, -jnp.inf))
                ms_list.append((cb * lh).astype(BF16))
            lhs = jnp.concatenate(ms_list, axis=1)
            lo = 512 * g + 256 * q
            x4 = xdt[:, lo:lo + 256]
            rhs = jnp.concatenate(
                [jnp.where(cblk == r, x4, jnp.zeros_like(x4)) for r in range(4)], axis=0)
            yd = _dot(lhs, rhs)
            y_parts.append(yd + eac_e[:, lo:lo + 256] * yoff[:, 256 * q:256 * (q + 1)])
    y = jnp.concatenate(y_parts, axis=1) + dskip_ref[...] * xs

    v = y * _silu(z)
    outs = []
    for g in range(N_BC_GROUPS):
        vg = v[:, 512 * g:512 * (g + 1)]
        msg = jnp.mean(vg * vg, axis=-1, keepdims=True)
        outs.append(vg * lax.rsqrt(msg + EPS))
    ssd_out = (jnp.concatenate(outs, axis=1) * ssdn_ref[...]).astype(BF16)

    u = scc * scv
    sbuf[8:8 + T, :] = u
    sw = scw_ref[...]
    conv = sw[2:3, :] * u + sw[1:2, :] * sbuf[7:7 + T, :] + sw[0:1, :] * sbuf[6:6 + T, :]
    sc = scb * conv
    gs = _dot((sc * sc).astype(BF16), gsum_ref[...])
    rstd = jnp.where(hmask, lax.rsqrt(gs * (1.0 / HEAD_DIM) + EPS), 0.0)
    rstd_e = _dot(_pack3(rstd), e3_ref[...])
    sc_out = (sc * rstd_e * scn_ref[...]).astype(BF16)

    mix = _dot(ssd_out, wout_ref[0:1024, :]) + _dot(sc_out, wout_ref[1024:2048, :])
    h = x + mix
    h_ref[0] = h

    ms2 = jnp.mean(h * h, axis=-1, keepdims=True)
    hn2 = h * lax.rsqrt(ms2 + EPS) * gffn_ref[...]
    for s in range(ROW_TILES):
        hn2_ref[pl.ds(s, T, stride=ROW_TILES), :] = hn2[:, LANES * s:LANES * (s + 1)]
    hi = hn2.astype(BF16)
    lo_ = (hn2 - hi.astype(F32)).astype(BF16)
    logits = _dot(hi, wrh_ref[...]) + (_dot(lo_, wrh_ref[...]) + _dot(hi, wrl_ref[...]))

    lanef = lax.broadcasted_iota(I32, (T, LANES), 1)
    lane_f = lanef.astype(F32)
    neg = -jnp.inf
    big = 1e9
    gl = jnp.where(lanef < N_GROUPS, logits, neg)
    gmax = jnp.max(gl, axis=-1, keepdims=True)
    gidx = jnp.min(jnp.where(gl == gmax, lane_f, big), axis=-1, keepdims=True)
    gsum = jnp.sum(jnp.where(lanef < N_GROUPS, jnp.exp(logits - gmax), 0.0), axis=-1, keepdims=True)
    gw = 1.0 / gsum
    egrp = lax.shift_right_logical(lanef - N_GROUPS, 3).astype(F32)
    in_grp = (lanef >= N_GROUPS) & (lanef < N_GROUPS + N_EXPERTS) & (egrp == gidx)
    el = jnp.where(in_grp, logits, neg)
    v1 = jnp.max(el, axis=-1, keepdims=True)
    i1 = jnp.min(jnp.where(el == v1, lane_f, big), axis=-1, keepdims=True)
    el2 = jnp.where(lane_f == i1, neg, el)
    v2 = jnp.max(el2, axis=-1, keepdims=True)
    i2 = jnp.min(jnp.where(el2 == v2, lane_f, big), axis=-1, keepdims=True)
    p = jnp.exp(v2 - v1)
    s1 = 1.0 / (1.0 + p)
    gate1 = gw * s1
    gate2 = gw * (p * s1)
    g_ref[...] = jnp.where(lanef == 0, gate1, jnp.where(lanef == 1, gate2, 0.0))
    emat = jnp.where(lanef == 0, i1 - N_GROUPS, jnp.where(lanef == 1, i2 - N_GROUPS, 0.0))
    e_ref[...] = emat.T[0:8, :].astype(I32)


def _mixer_call(x, gmix, wbig, wdt, convw, convb, dtb, alog, dskip, ssdn, scw, scn, wout, gffn,
                wrh, wrl, tri, e3, gsum):
    B, L, D = x.shape
    T = T_MIX
    nt = L // T
    n_tok = B * L

    def const(shape):
        return pl.BlockSpec(shape, lambda b, t: (0,) * len(shape))

    in_specs = [
        pl.BlockSpec((1, T, D), lambda b, t: (b, t, 0)),
        const((1, D)),
        const(wbig.shape), const(wdt.shape), const(convw.shape), const(convb.shape),
        const(dtb.shape), const(alog.shape), const(dskip.shape), const(ssdn.shape),
        const(scw.shape), const(scn.shape), const(wout.shape), const(gffn.shape),
        const(wrh.shape), const(wrl.shape), const(tri.shape), const(e3.shape), const(gsum.shape),
    ]
    out_shape = [
        jax.ShapeDtypeStruct((B, L, D), F32),
        jax.ShapeDtypeStruct((n_tok * ROW_TILES, LANES), F32),
        jax.ShapeDtypeStruct((8, n_tok), I32),
        jax.ShapeDtypeStruct((n_tok, LANES), F32),
    ]
    out_specs = [
        pl.BlockSpec((1, T, D), lambda b, t: (b, t, 0)),
        pl.BlockSpec((T * ROW_TILES, LANES), lambda b, t: (b * nt + t, 0)),
        pl.BlockSpec((8, T), lambda b, t: (0, b * nt + t)),
        pl.BlockSpec((T, LANES), lambda b, t: (b * nt + t, 0)),
    ]
    return pl.pallas_call(
        _mixer_kernel,
        grid=(B, nt),
        in_specs=in_specs,
        out_specs=out_specs,
        out_shape=out_shape,
        scratch_shapes=[
            pltpu.VMEM((T + 8, XBC), F32),
            pltpu.VMEM((T + 8, SC_WIDTH), F32),
            pltpu.VMEM((N_BC_GROUPS, STATE, 512), F32),
        ],
        compiler_params=pltpu.CompilerParams(
            dimension_semantics=("arbitrary", "arbitrary"),
            vmem_limit_bytes=VMEM_LIMIT),
        name="mixer",
    )(x, gmix, wbig, wdt, convw, convb, dtb, alog, dskip, ssdn, scw, scn, wout, gffn, wrh, wrl,
      tri, e3, gsum)


def _moe_kernel(bexp_ref, nact_ref, inv_cur_ref, inv_nxt_ref, hn2_ref, wg_ref, wu_ref, wd_ref,
                y2_ref, xbuf, ybuf, wgb, wub, wdb, gsem, ssem, *, n_tok):
    R = R_BLK
    slab = R * ROW_TILES
    b = pl.program_id(0)
    nact = nact_ref[0]
    slot = lax.rem(b, 2)
    dummy_base = 2 * n_tok

    def gather_start(inv_ref, sl):
        for j in range(R):
            a = inv_ref[0, 0, j]
            tok = jnp.where(a >= 0, jnp.bitwise_and(a, n_tok - 1), 0)
            pltpu.make_async_copy(
                hn2_ref.at[pl.ds(pl.multiple_of(tok * ROW_TILES, ROW_TILES), ROW_TILES), :],
                xbuf.at[pl.ds(pl.multiple_of(sl * slab + j * ROW_TILES, ROW_TILES), ROW_TILES), :],
                gsem.at[sl]).start()

    def gather_wait(sl):
        pltpu.make_async_copy(
            hn2_ref.at[pl.ds(0, slab), :],
            xbuf.at[pl.ds(pl.multiple_of(sl * slab, slab), slab), :],
            gsem.at[sl]).wait()

    def scatter_start(inv_ref, sl):
        for j in range(R):
            a = inv_ref[0, 0, j]
            dst = jnp.where(a >= 0, a, dummy_base + j)
            pltpu.make_async_copy(
                ybuf.at[pl.ds(pl.multiple_of(sl * slab + j * ROW_TILES, ROW_TILES), ROW_TILES), :],
                y2_ref.at[pl.ds(pl.multiple_of(dst * ROW_TILES, ROW_TILES), ROW_TILES), :],
                ssem.at[sl]).start()

    def scatter_wait(sl):
        pltpu.make_async_copy(
            ybuf.at[pl.ds(pl.multiple_of(sl * slab, slab), slab), :],
            y2_ref.at[pl.ds(0, slab), :],
            ssem.at[sl]).wait()

    @pl.when(b == 0)
    def _():
        gather_start(inv_cur_ref, 0)
        ybuf[pl.ds(slab, slab), :] = jnp.zeros((slab, LANES), F32)
        fill = pltpu.make_async_copy(
            ybuf.at[pl.ds(slab, slab), :],
            y2_ref.at[pl.ds(dummy_base * ROW_TILES, slab), :],
            ssem.at[1])
        fill.start()
        fill.wait()

    prev_e = bexp_ref[jnp.maximum(b - 1, 0)]

    @pl.when((b < nact) & ((b == 0) | (bexp_ref[b] != prev_e)))
    def _():
        wgb[...] = wg_ref[0].astype(BF16)
        wub[...] = wu_ref[0].astype(BF16)
        wdb[...] = wd_ref[0].astype(BF16)

    @pl.when((b < nact) & (b >= 2))
    def _():
        scatter_wait(slot)

    @pl.when(b < nact)
    def _():
        gather_wait(slot)
        gather_start(inv_nxt_ref, 1 - slot)
        base = pl.multiple_of(slot * slab, slab)
        xcat = jnp.concatenate(
            [xbuf[pl.ds(base + s, R, stride=ROW_TILES), :] for s in range(ROW_TILES)], axis=1)
        xb = xcat.astype(BF16)
        gg = _dot(xb, wgb[...])
        uu = _dot(xb, wub[...])
        act = (_silu(gg) * uu).astype(BF16)
        yy = _dot(act, wdb[...])
        for s in range(ROW_TILES):
            ybuf[pl.ds(base + s, R, stride=ROW_TILES), :] = yy[:, LANES * s:LANES * (s + 1)]
        scatter_start(inv_cur_ref, slot)

    @pl.when(b == nact)
    def _():
        gather_wait(slot)
        scatter_wait(1 - slot)

        @pl.when(nact >= 2)
        def _():
            scatter_wait(slot)


def _moe_call(bexp, nact, inv3, hn2, wg, wu, wd, n_tok):
    nb = inv3.shape[0]
    R = R_BLK
    slab = R * ROW_TILES
    grid_spec = pltpu.PrefetchScalarGridSpec(
        num_scalar_prefetch=2,
        grid=(nb,),
        in_specs=[
            pl.BlockSpec((1, 1, R), lambda b, be, na: (b, 0, 0), memory_space=pltpu.SMEM),
            pl.BlockSpec((1, 1, R), lambda b, be, na: (jnp.minimum(b + 1, nb - 1), 0, 0),
                         memory_space=pltpu.SMEM),
            pl.BlockSpec(memory_space=pl.ANY),
            pl.BlockSpec((1, D_MODEL, D_FF), lambda b, be, na: (be[b], 0, 0)),
            pl.BlockSpec((1, D_MODEL, D_FF), lambda b, be, na: (be[b], 0, 0)),
            pl.BlockSpec((1, D_FF, D_MODEL), lambda b, be, na: (be[b], 0, 0)),
        ],
        out_specs=pl.BlockSpec(memory_space=pl.ANY),
        scratch_shapes=[
            pltpu.VMEM((2 * slab, LANES), F32),
            pltpu.VMEM((2 * slab, LANES), F32),
            pltpu.VMEM((D_MODEL, D_FF), BF16),
            pltpu.VMEM((D_MODEL, D_FF), BF16),
            pltpu.VMEM((D_FF, D_MODEL), BF16),
            pltpu.SemaphoreType.DMA((2,)),
            pltpu.SemaphoreType.DMA((2,)),
        ],
    )
    return pl.pallas_call(
        functools.partial(_moe_kernel, n_tok=n_tok),
        grid_spec=grid_spec,
        out_shape=jax.ShapeDtypeStruct(((2 * n_tok + R) * ROW_TILES, LANES), F32),
        compiler_params=pltpu.CompilerParams(
            dimension_semantics=("arbitrary",),
            vmem_limit_bytes=VMEM_LIMIT),
        name="moe",
    )(bexp, nact, inv3, inv3, hn2, wg, wu, wd)


def _combine_kernel(h_ref, y0_ref, y1_ref, g_ref, fn_ref, o_ref):
    T = T_CMB
    h = h_ref[...]
    y0 = jnp.concatenate([y0_ref[pl.ds(s, T, stride=ROW_TILES), :] for s in range(ROW_TILES)], axis=1)
    y1 = jnp.concatenate([y1_ref[pl.ds(s, T, stride=ROW_TILES), :] for s in range(ROW_TILES)], axis=1)
    g = g_ref[...]
    v = h + (y0 * g[:, 0:1] + y1 * g[:, 1:2])
    ms = jnp.mean(v * v, axis=-1, keepdims=True)
    o_ref[...] = v * lax.rsqrt(ms + EPS) * fn_ref[...]


def _combine_call(h2d, y2, gates, fnorm):
    n_tok, D = h2d.shape
    T = T_CMB
    nt = n_tok // T
    return pl.pallas_call(
        _combine_kernel,
        grid=(nt,),
        in_specs=[
            pl.BlockSpec((T, D), lambda i: (i, 0)),
            pl.BlockSpec((T * ROW_TILES, LANES), lambda i: (i, 0)),
            pl.BlockSpec((T * ROW_TILES, LANES), lambda i: (nt + i, 0)),
            pl.BlockSpec((T, LANES), lambda i: (i, 0)),
            pl.BlockSpec((1, D), lambda i: (0, 0)),
        ],
        out_specs=pl.BlockSpec((T, D), lambda i: (i, 0)),
        out_shape=jax.ShapeDtypeStruct((n_tok, D), F32),
        compiler_params=pltpu.CompilerParams(dimension_semantics=("arbitrary",)),
        name="combine",
    )(h2d, y2, y2, gates, fnorm)


def _plan(e_rows, n_tok, nb):
    R = R_BLK
    e = e_rows[0:2].reshape(-1)
    oh = (e[:, None] == jnp.arange(N_EXPERTS, dtype=I32)[None, :]).astype(I32)
    csum = jnp.cumsum(oh, axis=0)
    rank = jnp.sum(csum * oh, axis=1) - 1
    counts = csum[-1]
    padded = ((counts + R - 1) // R) * R
    pend = jnp.cumsum(padded)
    pstart = pend - padded
    dest = pstart[e] + rank
    inv = jnp.full((nb * R,), -1, I32).at[dest].set(jnp.arange(2 * n_tok, dtype=I32))
    nact = (pend[-1] // R).astype(I32)
    blk = jnp.arange(nb, dtype=I32)
    bexp = jnp.clip(jnp.searchsorted(pend, blk * R, side="right"), 0, N_EXPERTS - 1).astype(I32)
    last = bexp[jnp.maximum(nact - 1, 0)]
    bexp = jnp.where(blk < nact, bexp, last)
    return bexp, nact.reshape(1), inv.reshape(nb, 1, R)


def kernel(x, norm_mix, w_in, ssd_conv_w, ssd_conv_b, dt_bias, a_log, d_skip, ssd_norm, sc_conv_w,
           sc_norm, w_out, norm_ffn, w_router_group, w_router_expert, w_gate, w_up, w_down, final_norm):
    B, L, D = x.shape
    n_tok = B * L
    depth = norm_mix.shape[0]
    assert depth == 1 and D == D_MODEL and (n_tok & (n_tok - 1)) == 0
    nb = (2 * n_tok) // R_BLK + N_EXPERTS + 1

    o1 = SSD_WIDTH
    o2 = o1 + XBC
    o3 = o2 + N_HEADS
    wi = w_in[0]
    wbig = jnp.concatenate([wi[:, 0:o2], wi[:, o3:]], axis=1).astype(BF16)
    wdt = jnp.pad(wi[:, o2:o3], ((0, 0), (0, LANES - N_HEADS))).astype(BF16)
    pad_h = (0, LANES - N_HEADS)
    dtb = jnp.pad(dt_bias[0], pad_h).reshape(1, LANES)
    alog = jnp.pad(a_log[0], pad_h).reshape(1, LANES)
    dskip = jnp.repeat(d_skip[0], HEAD_DIM).reshape(1, SSD_WIDTH)

    wre = jnp.transpose(w_router_expert[0], (1, 0, 2)).reshape(D, N_EXPERTS)
    wr = jnp.pad(jnp.concatenate([w_router_group[0], wre], axis=1),
                 ((0, 0), (0, LANES - N_GROUPS - N_EXPERTS)))
    wrh = wr.astype(BF16)
    wrl = (wr - wrh.astype(F32)).astype(BF16)

    ri = jnp.arange(T_MIX)
    tri = (ri[:, None] >= ri[None, :]).astype(BF16)
    er = jnp.arange(LANES)
    ec = jnp.arange(SSD_WIDTH)
    e3 = ((er[:, None] < 48) & ((er[:, None] % 16) == (ec[None, :] // HEAD_DIM))).astype(BF16)
    gsum = ((ec[:, None] // HEAD_DIM) == er[None, :]).astype(BF16)

    h, hn2, e_rows, gates = _mixer_call(
        x, norm_mix[0].reshape(1, D), wbig, wdt, ssd_conv_w[0], ssd_conv_b[0].reshape(1, XBC), dtb, alog,
        dskip, ssd_norm[0].reshape(1, SSD_WIDTH), sc_conv_w[0], sc_norm[0].reshape(1, SC_WIDTH),
        w_out[0].astype(BF16), norm_ffn[0].reshape(1, D), wrh, wrl, tri, e3, gsum)

    bexp, nact, inv3 = _plan(e_rows, n_tok, nb)
    y2 = _moe_call(bexp, nact, inv3, hn2, w_gate[0], w_up[0], w_down[0], n_tok)
    out = _combine_call(h.reshape(n_tok, D), y2, gates, final_norm.reshape(1, D))
    return out.reshape(B, L, D)
```

```python
import functools

import jax
import jax.numpy as jnp
from jax import lax
from jax.experimental import pallas as pl
from jax.experimental.pallas import tpu as pltpu

F32 = jnp.float32
BF16 = jnp.bfloat16
I32 = jnp.int32

EPS = 1e-6
D_MODEL = 1024
N_HEADS = 16
HEAD_DIM = 64
N_BC_GROUPS = 2
STATE = 128
SSD_WIDTH = 1024
XBC = SSD_WIDTH + 2 * N_BC_GROUPS * STATE
SC_WIDTH = 1024
SC_GROUPS = 16
N_GROUPS = 4
EPG = 8
N_EXPERTS = 32
D_FF = 512

LANES = 128
SUBLANES = 8
ROW_TILES = D_MODEL // LANES

T_MIX = 256
R_BLK = 128
T_CMB = 256

VMEM_LIMIT = 56 * 1024 * 1024


def _dot(a, b):
    return jnp.dot(a, b, preferred_element_type=F32)


def _split3(v):
    p1 = v.astype(BF16).astype(F32)
    r1 = v - p1
    p2 = r1.astype(BF16).astype(F32)
    p3 = (r1 - p2).astype(BF16).astype(F32)
    return p1, p2, p3


def _pack3(v):
    p1, p2, p3 = _split3(v)
    return (p1 + pltpu.roll(p2, 16, 1) + pltpu.roll(p3, 32, 1)).astype(BF16)


def _silu(v):
    return v * jax.nn.sigmoid(v)


def _mixer_kernel(x_ref, gmix_ref, wbig_ref, wdt_ref, convw_ref, convb_ref, dtb_ref, alog_ref,
                  dskip_ref, ssdn_ref, scw_ref, scn_ref, wout_ref, gffn_ref, wrh_ref, wrl_ref,
                  tri_ref, e3_ref, gsum_ref,
                  h_ref, hn2_ref, e_ref, g_ref,
                  cbuf, sbuf, st_ref):
    T = T_MIX
    t = pl.program_id(1)

    @pl.when(t == 0)
    def _():
        cbuf[0:8, :] = jnp.zeros((8, XBC), F32)
        sbuf[0:8, :] = jnp.zeros((8, SC_WIDTH), F32)
        st_ref[...] = jnp.zeros(st_ref.shape, F32)

    @pl.when(t > 0)
    def _():
        cbuf[0:8, :] = cbuf[T:T + 8, :]
        sbuf[0:8, :] = sbuf[T:T + 8, :]

    x = x_ref[0]
    ms = jnp.mean(x * x, axis=-1, keepdims=True)
    hn = (x * lax.rsqrt(ms + EPS) * gmix_ref[...]).astype(BF16)

    z = _dot(hn, wbig_ref[:, 0:1024])
    xbc = _dot(hn, wbig_ref[:, 1024:2560])
    scb = _dot(hn, wbig_ref[:, 2560:3584])
    scc = _dot(hn, wbig_ref[:, 3584:4608])
    scv = _dot(hn, wbig_ref[:, 4608:5632])
    dt_raw = _dot(hn, wdt_ref[...])

    cbuf[8:8 + T, :] = xbc
    cw = convw_ref[...]
    acc = convb_ref[...] + cw[3:4, :] * xbc
    for k in range(3):
        acc = acc + cw[k:k + 1, :] * cbuf[5 + k:5 + k + T, :]
    xact = _silu(acc)
    xs = xact[:, 0:SSD_WIDTH]

    lane = lax.broadcasted_iota(I32, (1, LANES), 1)
    hmask = lane < N_HEADS
    a = jnp.where(hmask, -jnp.exp(alog_ref[...]), 0.0)
    dtv = dt_raw + dtb_ref[...]
    dt = jnp.where(hmask, jnp.maximum(dtv, 0.0) + jnp.log1p(jnp.exp(-jnp.abs(dtv))), 0.0)
    adt = dt * a
    c3 = _dot(tri_ref[...], _pack3(adt))
    ac = jnp.where(hmask, c3 + pltpu.roll(c3, LANES - 16, 1) + pltpu.roll(c3, LANES - 32, 1), 0.0)
    ac_last = ac[T - 1:T, :]
    eac = jnp.where(hmask, jnp.exp(ac), 0.0)
    wdt = dt * jnp.exp(ac_last - ac)

    stacked = jnp.concatenate([_pack3(dt), _pack3(wdt), _pack3(eac)], axis=0)
    ex = _dot(stacked, e3_ref[...])
    dt_e = ex[0:T]
    wdt_e = ex[T:2 * T]
    eac_e = ex[2 * T:3 * T]
    xdt = (xs * dt_e).astype(BF16)
    xdtw = (xs * wdt_e).astype(BF16)

    ac_rows = ac.T
    rr = lax.broadcasted_iota(I32, (T, T), 0)
    cc = lax.broadcasted_iota(I32, (T, T), 1)
    causal = rr >= cc
    cblk = lax.shift_right_logical(lax.broadcasted_iota(I32, (T, 4 * HEAD_DIM), 1), 6)

    y_parts = []
    for g in range(N_BC_GROUPS):
        bg = xact[:, SSD_WIDTH + STATE * g:SSD_WIDTH + STATE * (g + 1)]
        cg = xact[:, SSD_WIDTH + 2 * STATE + STATE * g:SSD_WIDTH + 2 * STATE + STATE * (g + 1)]
        bb = bg.astype(BF16)
        cbf = cg.astype(BF16)
        cb = lax.dot_general(cbf, bb, (((1,), (1,)), ((), ())), preferred_element_type=F32)
        st = st_ref[g]
        yoff = _dot(cbf, st.astype(BF16))
        dec = eac_e[T - 1:T, 512 * g:512 * (g + 1)]
        bgt = bg.T.astype(BF16)
        st_ref[g] = st * dec + _dot(bgt, xdtw[:, 512 * g:512 * (g + 1)])
        for q in range(2):
            ms_list = []
            for r in range(4):
                hh = 8 * g + 4 * q + r
                seg = ac[:, hh:hh + 1] - ac_rows[hh:hh + 1, :]
                lh = jnp.exp(jnp.where(causal, seg, -jnp.inf))
                ms_list.append((cb * lh).astype(BF16))
            lhs = jnp.concatenate(ms_list, axis=1)
            lo = 512 * g + 256 * q
            x4 = xdt[:, lo:lo + 256]
            rhs = jnp.concatenate(
                [jnp.where(cblk == r, x4, jnp.zeros_like(x4)) for r in range(4)], axis=0)
            yd = _dot(lhs, rhs)
            y_parts.append(yd + eac_e[:, lo:lo + 256] * yoff[:, 256 * q:256 * (q + 1)])
    y = jnp.concatenate(y_parts, axis=1) + dskip_ref[...] * xs

    v = y * _silu(z)
    outs = []
    for g in range(N_BC_GROUPS):
        vg = v[:, 512 * g:512 * (g + 1)]
        msg = jnp.mean(vg * vg, axis=-1, keepdims=True)
        outs.append(vg * lax.rsqrt(msg + EPS))
    ssd_out = (jnp.concatenate(outs, axis=1) * ssdn_ref[...]).astype(BF16)

    u = scc * scv
    sbuf[8:8 + T, :] = u
    sw = scw_ref[...]
    conv = sw[2:3, :] * u + sw[1:2, :] * sbuf[7:7 + T, :] + sw[0:1, :] * sbuf[6:6 + T, :]
    sc = scb * conv
    gs = _dot((sc * sc).astype(BF16), gsum_ref[...])
    rstd = jnp.where(hmask, lax.rsqrt(gs * (1.0 / HEAD_DIM) + EPS), 0.0)
    rstd_e = _dot(_pack3(rstd), e3_ref[...])
    sc_out = (sc * rstd_e * scn_ref[...]).astype(BF16)

    mix = _dot(ssd_out, wout_ref[0:1024, :]) + _dot(sc_out, wout_ref[1024:2048, :])
    h = x + mix
    h_ref[0] = h

    ms2 = jnp.mean(h * h, axis=-1, keepdims=True)
    hn2 = h * lax.rsqrt(ms2 + EPS) * gffn_ref[...]
    for s in range(ROW_TILES):
        hn2_ref[pl.ds(s, T, stride=ROW_TILES), :] = hn2[:, LANES * s:LANES * (s + 1)]
    hi = hn2.astype(BF16)
    lo_ = (hn2 - hi.astype(F32)).astype(BF16)
    logits = _dot(hi, wrh_ref[...]) + (_dot(lo_, wrh_ref[...]) + _dot(hi, wrl_ref[...]))

    lanef = lax.broadcasted_iota(I32, (T, LANES), 1)
    lane_f = lanef.astype(F32)
    neg = -jnp.inf
    big = 1e9
    gl = jnp.where(lanef < N_GROUPS, logits, neg)
    gmax = jnp.max(gl, axis=-1, keepdims=True)
    gidx = jnp.min(jnp.where(gl == gmax, lane_f, big), axis=-1, keepdims=True)
    gsum = jnp.sum(jnp.where(lanef < N_GROUPS, jnp.exp(logits - gmax), 0.0), axis=-1, keepdims=True)
    gw = 1.0 / gsum
    egrp = lax.shift_right_logical(lanef - N_GROUPS, 3).astype(F32)
    in_grp = (lanef >= N_GROUPS) & (lanef < N_GROUPS + N_EXPERTS) & (egrp == gidx)
    el = jnp.where(in_grp, logits, neg)
    v1 = jnp.max(el, axis=-1, keepdims=True)
    i1 = jnp.min(jnp.where(el == v1, lane_f, big), axis=-1, keepdims=True)
    el2 = jnp.where(lane_f == i1, neg, el)
    v2 = jnp.max(el2, axis=-1, keepdims=True)
    i2 = jnp.min(jnp.where(el2 == v2, lane_f, big), axis=-1, keepdims=True)
    p = jnp.exp(v2 - v1)
    s1 = 1.0 / (1.0 + p)
    gate1 = gw * s1
    gate2 = gw * (p * s1)
    g_ref[...] = jnp.where(lanef == 0, gate1, jnp.where(lanef == 1, gate2, 0.0))
    emat = jnp.where(lanef == 0, i1 - N_GROUPS, jnp.where(lanef == 1, i2 - N_GROUPS, 0.0))
    e_ref[...] = emat.T[0:8, :].astype(I32)


def _mixer_call(x, gmix, wbig, wdt, convw, convb, dtb, alog, dskip, ssdn, scw, scn, wout, gffn,
                wrh, wrl, tri, e3, gsum):
    B, L, D = x.shape
    T = T_MIX
    nt = L // T
    n_tok = B * L

    def const(shape):
        return pl.BlockSpec(shape, lambda b, t: (0,) * len(shape))

    in_specs = [
        pl.BlockSpec((1, T, D), lambda b, t: (b, t, 0)),
        const((1, D)),
        const(wbig.shape), const(wdt.shape), const(convw.shape), const(convb.shape),
        const(dtb.shape), const(alog.shape), const(dskip.shape), const(ssdn.shape),
        const(scw.shape), const(scn.shape), const(wout.shape), const(gffn.shape),
        const(wrh.shape), const(wrl.shape), const(tri.shape), const(e3.shape), const(gsum.shape),
    ]
    out_shape = [
        jax.ShapeDtypeStruct((B, L, D), F32),
        jax.ShapeDtypeStruct((n_tok * ROW_TILES, LANES), F32),
        jax.ShapeDtypeStruct((8, n_tok), I32),
        jax.ShapeDtypeStruct((n_tok, LANES), F32),
    ]
    out_specs = [
        pl.BlockSpec((1, T, D), lambda b, t: (b, t, 0)),
        pl.BlockSpec((T * ROW_TILES, LANES), lambda b, t: (b * nt + t, 0)),
        pl.BlockSpec((8, T), lambda b, t: (0, b * nt + t)),
        pl.BlockSpec((T, LANES), lambda b, t: (b * nt + t, 0)),
    ]
    return pl.pallas_call(
        _mixer_kernel,
        grid=(B, nt),
        in_specs=in_specs,
        out_specs=out_specs,
        out_shape=out_shape,
        scratch_shapes=[
            pltpu.VMEM((T + 8, XBC), F32),
            pltpu.VMEM((T + 8, SC_WIDTH), F32),
            pltpu.VMEM((N_BC_GROUPS, STATE, 512), F32),
        ],
        compiler_params=pltpu.CompilerParams(
            dimension_semantics=("arbitrary", "arbitrary"),
            vmem_limit_bytes=VMEM_LIMIT),
        name="mixer",
    )(x, gmix, wbig, wdt, convw, convb, dtb, alog, dskip, ssdn, scw, scn, wout, gffn, wrh, wrl,
      tri, e3, gsum)


def _moe_kernel(bexp_ref, nact_ref, inv_cur_ref, inv_nxt_ref, hn2_ref, wg_ref, wu_ref, wd_ref,
                y2_ref, xbuf, ybuf, wgb, wub, wdb, gsem, ssem, *, n_tok):
    R = R_BLK
    slab = R * ROW_TILES
    b = pl.program_id(0)
    nact = nact_ref[0]
    slot = lax.rem(b, 2)
    dummy_base = 2 * n_tok

    def gather_start(inv_ref, sl):
        for j in range(R):
            a = inv_ref[0, 0, j]
            tok = jnp.where(a >= 0, jnp.bitwise_and(a, n_tok - 1), 0)
            pltpu.make_async_copy(
                hn2_ref.at[pl.ds(pl.multiple_of(tok * ROW_TILES, ROW_TILES), ROW_TILES), :],
                xbuf.at[pl.ds(pl.multiple_of(sl * slab + j * ROW_TILES, ROW_TILES), ROW_TILES), :],
                gsem.at[sl]).start()

    def gather_wait(sl):
        pltpu.make_async_copy(
            hn2_ref.at[pl.ds(0, slab), :],
            xbuf.at[pl.ds(pl.multiple_of(sl * slab, slab), slab), :],
            gsem.at[sl]).wait()

    def scatter_start(inv_ref, sl):
        for j in range(R):
            a = inv_ref[0, 0, j]
            dst = jnp.where(a >= 0, a, dummy_base + j)
            pltpu.make_async_copy(
                ybuf.at[pl.ds(pl.multiple_of(sl * slab + j * ROW_TILES, ROW_TILES), ROW_TILES), :],
                y2_ref.at[pl.ds(pl.multiple_of(dst * ROW_TILES, ROW_TILES), ROW_TILES), :],
                ssem.at[sl]).start()

    def scatter_wait(sl):
        pltpu.make_async_copy(
            ybuf.at[pl.ds(pl.multiple_of(sl * slab, slab), slab), :],
            y2_ref.at[pl.ds(0, slab), :],
            ssem.at[sl]).wait()

    @pl.when(b == 0)
    def _():
        gather_start(inv_cur_ref, 0)
        ybuf[pl.ds(slab, slab), :] = jnp.zeros((slab, LANES), F32)
        fill = pltpu.make_async_copy(
            ybuf.at[pl.ds(slab, slab), :],
            y2_ref.at[pl.ds(dummy_base * ROW_TILES, slab), :],
            ssem.at[1])
        fill.start()
        fill.wait()

    prev_e = bexp_ref[jnp.maximum(b - 1, 0)]

    @pl.when((b < nact) & ((b == 0) | (bexp_ref[b] != prev_e)))
    def _():
        wgb[...] = wg_ref[0].astype(BF16)
        wub[...] = wu_ref[0].astype(BF16)
        wdb[...] = wd_ref[0].astype(BF16)

    @pl.when((b < nact) & (b >= 2))
    def _():
        scatter_wait(slot)

    @pl.when(b < nact)
    def _():
        gather_wait(slot)
        gather_start(inv_nxt_ref, 1 - slot)
        base = pl.multiple_of(slot * slab, slab)
        xcat = jnp.concatenate(
            [xbuf[pl.ds(base + s, R, stride=ROW_TILES), :] for s in range(ROW_TILES)], axis=1)
        xb = xcat.astype(BF16)
        gg = _dot(xb, wgb[...])
        uu = _dot(xb, wub[...])
        act = (_silu(gg) * uu).astype(BF16)
        yy = _dot(act, wdb[...])
        for s in range(ROW_TILES):
            ybuf[pl.ds(base + s, R, stride=ROW_TILES), :] = yy[:, LANES * s:LANES * (s + 1)]
        scatter_start(inv_cur_ref, slot)

    @pl.when(b == nact)
    def _():
        gather_wait(slot)
        scatter_wait(1 - slot)

        @pl.when(nact >= 2)
        def _():
            scatter_wait(slot)


def _moe_call(bexp, nact, inv3, hn2, wg, wu, wd, n_tok):
    nb = inv3.shape[0]
    R = R_BLK
    slab = R * ROW_TILES
    grid_spec = pltpu.PrefetchScalarGridSpec(
        num_scalar_prefetch=2,
        grid=(nb,),
        in_specs=[
            pl.BlockSpec((1, 1, R), lambda b, be, na: (b, 0, 0), memory_space=pltpu.SMEM),
            pl.BlockSpec((1, 1, R), lambda b, be, na: (jnp.minimum(b + 1, nb - 1), 0, 0),
                         memory_space=pltpu.SMEM),
            pl.BlockSpec(memory_space=pl.ANY),
            pl.BlockSpec((1, D_MODEL, D_FF), lambda b, be, na: (be[b], 0, 0)),
            pl.BlockSpec((1, D_MODEL, D_FF), lambda b, be, na: (be[b], 0, 0)),
            pl.BlockSpec((1, D_FF, D_MODEL), lambda b, be, na: (be[b], 0, 0)),
        ],
        out_specs=pl.BlockSpec(memory_space=pl.ANY),
        scratch_shapes=[
            pltpu.VMEM((2 * slab, LANES), F32),
            pltpu.VMEM((2 * slab, LANES), F32),
            pltpu.VMEM((D_MODEL, D_FF), BF16),
            pltpu.VMEM((D_MODEL, D_FF), BF16),
            pltpu.VMEM((D_FF, D_MODEL), BF16),
            pltpu.SemaphoreType.DMA((2,)),
            pltpu.SemaphoreType.DMA((2,)),
        ],
    )
    return pl.pallas_call(
        functools.partial(_moe_kernel, n_tok=n_tok),
        grid_spec=grid_spec,
        out_shape=jax.ShapeDtypeStruct(((2 * n_tok + R) * ROW_TILES, LANES), F32),
        compiler_params=pltpu.CompilerParams(
            dimension_semantics=("arbitrary",),
            vmem_limit_bytes=VMEM_LIMIT),
        name="moe",
    )(bexp, nact, inv3, inv3, hn2, wg, wu, wd)


def _combine_kernel(h_ref, y0_ref, y1_ref, g_ref, fn_ref, o_ref):
    T = T_CMB
    h = h_ref[...]
    y0 = jnp.concatenate([y0_ref[pl.ds(s, T, stride=ROW_TILES), :] for s in range(ROW_TILES)], axis=1)
    y1 = jnp.concatenate([y1_ref[pl.ds(s, T, stride=ROW_TILES), :] for s in range(ROW_TILES)], axis=1)
    g = g_ref[...]
    v = h + (y0 * g[:, 0:1] + y1 * g[:, 1:2])
    ms = jnp.mean(v * v, axis=-1, keepdims=True)
    o_ref[...] = v * lax.rsqrt(ms + EPS) * fn_ref[...]


def _combine_call(h2d, y2, gates, fnorm):
    n_tok, D = h2d.shape
    T = T_CMB
    nt = n_tok // T
    return pl.pallas_call(
        _combine_kernel,
        grid=(nt,),
        in_specs=[
            pl.BlockSpec((T, D), lambda i: (i, 0)),
            pl.BlockSpec((T * ROW_TILES, LANES), lambda i: (i, 0)),
            pl.BlockSpec((T * ROW_TILES, LANES), lambda i: (nt + i, 0)),
            pl.BlockSpec((T, LANES), lambda i: (i, 0)),
            pl.BlockSpec((1, D), lambda i: (0, 0)),
        ],
        out_specs=pl.BlockSpec((T, D), lambda i: (i, 0)),
        out_shape=jax.ShapeDtypeStruct((n_tok, D), F32),
        compiler_params=pltpu.CompilerParams(dimension_semantics=("arbitrary",)),
        name="combine",
    )(h2d, y2, y2, gates, fnorm)


def _plan_kernel(e_ref, ux_ref, ones_ref, lx_ref, dest_ref, bexp_ref, nact_ref):
    rows = e_ref.shape[0]
    ev = e_ref[...]
    bidx = (lax.broadcasted_iota(I32, (SUBLANES, LANES), 0) * LANES
            + lax.broadcasted_iota(I32, (SUBLANES, LANES), 1))
    dest = jnp.zeros((rows, LANES), F32)
    bexp = jnp.zeros((SUBLANES, LANES), I32)
    pstart = jnp.zeros((1, LANES), F32)
    e_last = jnp.zeros((1, LANES), I32)
    for e in range(N_EXPERTS):
        ohb = ev == e
        oh = jnp.where(ohb, 1.0, 0.0).astype(BF16)
        within = _dot(oh, ux_ref[...])
        rtot = _dot(oh, ones_ref[...])
        rpre = _dot(lx_ref[...], rtot.astype(BF16))
        cnt = rpre[rows - 1:rows, :] + rtot[rows - 1:rows, :]
        dest = dest + jnp.where(ohb, within + rpre + pstart, 0.0)
        cnt_i = cnt.astype(I32)
        padded = lax.shift_left(lax.shift_right_logical(cnt_i + (R_BLK - 1), 7), 7)
        pstart = pstart + padded.astype(F32)
        pend_i = pstart.astype(I32)
        bexp = bexp + jnp.where(pend_i[0:1, 0:1] <= bidx * R_BLK, 1, 0)
        e_last = jnp.where(cnt_i > 0, e, e_last)
    nact = lax.shift_right_logical(pstart.astype(I32), 7)
    dest_ref[...] = dest.astype(I32)
    bexp_ref[...] = jnp.where(bidx < nact[0:1, 0:1], jnp.minimum(bexp, N_EXPERTS - 1), e_last[0:1, 0:1])
    nact_ref[...] = jnp.broadcast_to(nact, (SUBLANES, LANES))


def _invert_kernel(dest_ref, inv_ref):
    n_pos = inv_ref.shape[0]
    n_asg = dest_ref.shape[0]

    def fill(i, c):
        inv_ref[i] = -1
        return c

    lax.fori_loop(0, n_pos, fill, 0, unroll=16)

    def put(a, c):
        inv_ref[dest_ref[a]] = a
        return c

    lax.fori_loop(0, n_asg, put, 0, unroll=16)


def _plan(e_rows, n_tok, nb):
    R = R_BLK
    assert R == LANES
    n_asg = 2 * n_tok
    rows = n_asg // LANES
    emat = e_rows[0:2].reshape(rows, LANES)
    li = jnp.arange(LANES)
    ux = (li[:, None] < li[None, :]).astype(BF16)
    ones = jnp.ones((LANES, LANES), BF16)
    ri = jnp.arange(rows)
    lx = (ri[:, None] > ri[None, :]).astype(BF16)
    dest, bexp, nact = pl.pallas_call(
        _plan_kernel,
        out_shape=[jax.ShapeDtypeStruct((rows, LANES), I32),
                   jax.ShapeDtypeStruct((SUBLANES, LANES), I32),
                   jax.ShapeDtypeStruct((SUBLANES, LANES), I32)],
        name="plan",
    )(emat, ux, ones, lx)
    inv = pl.pallas_call(
        _invert_kernel,
        in_specs=[pl.BlockSpec(memory_space=pltpu.SMEM)],
        out_specs=pl.BlockSpec(memory_space=pltpu.SMEM),
        out_shape=jax.ShapeDtypeStruct((nb * R,), I32),
        name="invert",
    )(dest.reshape(n_asg))
    return bexp.reshape(-1)[0:nb], nact[0, 0:1], inv.reshape(nb, 1, R)


def kernel(x, norm_mix, w_in, ssd_conv_w, ssd_conv_b, dt_bias, a_log, d_skip, ssd_norm, sc_conv_w,
           sc_norm, w_out, norm_ffn, w_router_group, w_router_expert, w_gate, w_up, w_down, final_norm):
    B, L, D = x.shape
    n_tok = B * L
    depth = norm_mix.shape[0]
    assert depth == 1 and D == D_MODEL and (n_tok & (n_tok - 1)) == 0
    nb = (2 * n_tok) // R_BLK + N_EXPERTS + 1

    o1 = SSD_WIDTH
    o2 = o1 + XBC
    o3 = o2 + N_HEADS
    wi = w_in[0]
    wbig = jnp.concatenate([wi[:, 0:o2], wi[:, o3:]], axis=1).astype(BF16)
    wdt = jnp.pad(wi[:, o2:o3], ((0, 0), (0, LANES - N_HEADS))).astype(BF16)
    pad_h = (0, LANES - N_HEADS)
    dtb = jnp.pad(dt_bias[0], pad_h).reshape(1, LANES)
    alog = jnp.pad(a_log[0], pad_h).reshape(1, LANES)
    dskip = jnp.repeat(d_skip[0], HEAD_DIM).reshape(1, SSD_WIDTH)

    wre = jnp.transpose(w_router_expert[0], (1, 0, 2)).reshape(D, N_EXPERTS)
    wr = jnp.pad(jnp.concatenate([w_router_group[0], wre], axis=1),
                 ((0, 0), (0, LANES - N_GROUPS - N_EXPERTS)))
    wrh = wr.astype(BF16)
    wrl = (wr - wrh.astype(F32)).astype(BF16)

    ri = jnp.arange(T_MIX)
    tri = (ri[:, None] >= ri[None, :]).astype(BF16)
    er = jnp.arange(LANES)
    ec = jnp.arange(SSD_WIDTH)
    e3 = ((er[:, None] < 48) & ((er[:, None] % 16) == (ec[None, :] // HEAD_DIM))).astype(BF16)
    gsum = ((ec[:, None] // HEAD_DIM) == er[None, :]).astype(BF16)

    h, hn2, e_rows, gates = _mixer_call(
        x, norm_mix[0].reshape(1, D), wbig, wdt, ssd_conv_w[0], ssd_conv_b[0].reshape(1, XBC), dtb, alog,
        dskip, ssd_norm[0].reshape(1, SSD_WIDTH), sc_conv_w[0], sc_norm[0].reshape(1, SC_WIDTH),
        w_out[0].astype(BF16), norm_ffn[0].reshape(1, D), wrh, wrl, tri, e3, gsum)

    bexp, nact, inv3 = _plan(e_rows, n_tok, nb)
    y2 = _moe_call(bexp, nact, inv3, hn2, w_gate[0], w_up[0], w_down[0], n_tok)
    out = _combine_call(h.reshape(n_tok, D), y2, gates, final_norm.reshape(1, D))
    return out.reshape(B, L, D)
```

```python
import functools

import jax
import jax.numpy as jnp
from jax import lax
from jax.experimental import pallas as pl
from jax.experimental.pallas import tpu as pltpu

F32 = jnp.float32
BF16 = jnp.bfloat16
I32 = jnp.int32

EPS = 1e-6
D_MODEL = 1024
N_HEADS = 16
HEAD_DIM = 64
N_BC_GROUPS = 2
STATE = 128
SSD_WIDTH = 1024
XBC = SSD_WIDTH + 2 * N_BC_GROUPS * STATE
SC_WIDTH = 1024
SC_GROUPS = 16
N_GROUPS = 4
EPG = 8
N_EXPERTS = 32
D_FF = 512

LANES = 128
SUBLANES = 8
ROW_TILES = D_MODEL // LANES

T_MIX = 256
R_BLK = 128
T_CMB = 512

VMEM_LIMIT = 56 * 1024 * 1024


def _dot(a, b):
    return jnp.dot(a, b, preferred_element_type=F32)


def _split3(v):
    p1 = v.astype(BF16).astype(F32)
    r1 = v - p1
    p2 = r1.astype(BF16).astype(F32)
    p3 = (r1 - p2).astype(BF16).astype(F32)
    return p1, p2, p3


def _pack3(v):
    p1, p2, p3 = _split3(v)
    return (p1 + pltpu.roll(p2, 16, 1) + pltpu.roll(p3, 32, 1)).astype(BF16)


def _silu(v):
    return v * jax.nn.sigmoid(v)


def _mixer_kernel(x_ref, gmix_ref, wbig_ref, wdt_ref, convw_ref, convb_ref, dtb_ref, alog_ref,
                  dskip_ref, ssdn_ref, scw_ref, scn_ref, wout_ref, gffn_ref, wrh_ref, wrl_ref,
                  tri_ref, e3_ref, gsum_ref,
                  h_ref, hn2_ref, e_ref, g_ref,
                  cbuf, sbuf, st_ref):
    T = T_MIX
    t = pl.program_id(1)

    @pl.when(t == 0)
    def _():
        cbuf[0:8, :] = jnp.zeros((8, XBC), F32)
        sbuf[0:8, :] = jnp.zeros((8, SC_WIDTH), F32)
        st_ref[...] = jnp.zeros(st_ref.shape, F32)

    @pl.when(t > 0)
    def _():
        cbuf[0:8, :] = cbuf[T:T + 8, :]
        sbuf[0:8, :] = sbuf[T:T + 8, :]

    x = x_ref[0]
    ms = jnp.mean(x * x, axis=-1, keepdims=True)
    hn = (x * lax.rsqrt(ms + EPS) * gmix_ref[...]).astype(BF16)

    z = _dot(hn, wbig_ref[:, 0:1024])
    xbc = _dot(hn, wbig_ref[:, 1024:2560])
    scb = _dot(hn, wbig_ref[:, 2560:3584])
    scc = _dot(hn, wbig_ref[:, 3584:4608])
    scv = _dot(hn, wbig_ref[:, 4608:5632])
    dt_raw = _dot(hn, wdt_ref[...])

    cbuf[8:8 + T, :] = xbc
    cw = convw_ref[...]
    acc = convb_ref[...] + cw[3:4, :] * xbc
    for k in range(3):
        acc = acc + cw[k:k + 1, :] * cbuf[5 + k:5 + k + T, :]
    xact = _silu(acc)
    xs = xact[:, 0:SSD_WIDTH]

    lane = lax.broadcasted_iota(I32, (1, LANES), 1)
    hmask = lane < N_HEADS
    a = jnp.where(hmask, -jnp.exp(alog_ref[...]), 0.0)
    dtv = dt_raw + dtb_ref[...]
    dt = jnp.where(hmask, jnp.maximum(dtv, 0.0) + jnp.log1p(jnp.exp(-jnp.abs(dtv))), 0.0)
    adt = dt * a
    c3 = _dot(tri_ref[...], _pack3(adt))
    ac = jnp.where(hmask, c3 + pltpu.roll(c3, LANES - 16, 1) + pltpu.roll(c3, LANES - 32, 1), 0.0)
    ac_last = ac[T - 1:T, :]
    eac = jnp.where(hmask, jnp.exp(ac), 0.0)
    wdt = dt * jnp.exp(ac_last - ac)

    stacked = jnp.concatenate([_pack3(dt), _pack3(wdt), _pack3(eac)], axis=0)
    ex = _dot(stacked, e3_ref[...])
    dt_e = ex[0:T]
    wdt_e = ex[T:2 * T]
    eac_e = ex[2 * T:3 * T]
    xdt = (xs * dt_e).astype(BF16)
    xdtw = (xs * wdt_e).astype(BF16)

    ac_rows = ac.T
    rr = lax.broadcasted_iota(I32, (T, T), 0)
    cc = lax.broadcasted_iota(I32, (T, T), 1)
    causal = rr >= cc
    cblk = lax.shift_right_logical(lax.broadcasted_iota(I32, (T, 4 * HEAD_DIM), 1), 6)

    y_parts = []
    for g in range(N_BC_GROUPS):
        bg = xact[:, SSD_WIDTH + STATE * g:SSD_WIDTH + STATE * (g + 1)]
        cg = xact[:, SSD_WIDTH + 2 * STATE + STATE * g:SSD_WIDTH + 2 * STATE + STATE * (g + 1)]
        bb = bg.astype(BF16)
        cbf = cg.astype(BF16)
        cb = lax.dot_general(cbf, bb, (((1,), (1,)), ((), ())), preferred_element_type=F32)
        st = st_ref[g]
        yoff = _dot(cbf, st.astype(BF16))
        dec = eac_e[T - 1:T, 512 * g:512 * (g + 1)]
        bgt = bg.T.astype(BF16)
        st_ref[g] = st * dec + _dot(bgt, xdtw[:, 512 * g:512 * (g + 1)])
        for q in range(2):
            ms_list = []
            for r in range(4):
                hh = 8 * g + 4 * q + r
                seg = ac[:, hh:hh + 1] - ac_rows[hh:hh + 1, :]
                lh = jnp.exp(jnp.where(causal, seg, -jnp.inf))
                ms_list.append((cb * lh).astype(BF16))
            lhs = jnp.concatenate(ms_list, axis=1)
            lo = 512 * g + 256 * q
            x4 = xdt[:, lo:lo + 256]
            rhs = jnp.concatenate(
                [jnp.where(cblk == r, x4, jnp.zeros_like(x4)) for r in range(4)], axis=0)
            yd = _dot(lhs, rhs)
            y_parts.append(yd + eac_e[:, lo:lo + 256] * yoff[:, 256 * q:256 * (q + 1)])
    y = jnp.concatenate(y_parts, axis=1) + dskip_ref[...] * xs

    v = y * _silu(z)
    outs = []
    for g in range(N_BC_GROUPS):
        vg = v[:, 512 * g:512 * (g + 1)]
        msg = jnp.mean(vg * vg, axis=-1, keepdims=True)
        outs.append(vg * lax.rsqrt(msg + EPS))
    ssd_out = (jnp.concatenate(outs, axis=1) * ssdn_ref[...]).astype(BF16)

    u = scc * scv
    sbuf[8:8 + T, :] = u
    sw = scw_ref[...]
    conv = sw[2:3, :] * u + sw[1:2, :] * sbuf[7:7 + T, :] + sw[0:1, :] * sbuf[6:6 + T, :]
    sc = scb * conv
    gs = _dot((sc * sc).astype(BF16), gsum_ref[...])
    rstd = jnp.where(hmask, lax.rsqrt(gs * (1.0 / HEAD_DIM) + EPS), 0.0)
    rstd_e = _dot(_pack3(rstd), e3_ref[...])
    sc_out = (sc * rstd_e * scn_ref[...]).astype(BF16)

    mix = _dot(ssd_out, wout_ref[0:1024, :]) + _dot(sc_out, wout_ref[1024:2048, :])
    h = x + mix
    h_ref[0] = h

    ms2 = jnp.mean(h * h, axis=-1, keepdims=True)
    hn2 = h * lax.rsqrt(ms2 + EPS) * gffn_ref[...]
    for s in range(ROW_TILES):
        hn2_ref[pl.ds(s, T, stride=ROW_TILES), :] = hn2[:, LANES * s:LANES * (s + 1)]
    hi = hn2.astype(BF16)
    lo_ = (hn2 - hi.astype(F32)).astype(BF16)
    logits = _dot(hi, wrh_ref[...]) + (_dot(lo_, wrh_ref[...]) + _dot(hi, wrl_ref[...]))

    lanef = lax.broadcasted_iota(I32, (T, LANES), 1)
    lane_f = lanef.astype(F32)
    neg = -jnp.inf
    big = 1e9
    gl = jnp.where(lanef < N_GROUPS, logits, neg)
    gmax = jnp.max(gl, axis=-1, keepdims=True)
    gidx = jnp.min(jnp.where(gl == gmax, lane_f, big), axis=-1, keepdims=True)
    gsum = jnp.sum(jnp.where(lanef < N_GROUPS, jnp.exp(logits - gmax), 0.0), axis=-1, keepdims=True)
    gw = 1.0 / gsum
    egrp = lax.shift_right_logical(lanef - N_GROUPS, 3).astype(F32)
    in_grp = (lanef >= N_GROUPS) & (lanef < N_GROUPS + N_EXPERTS) & (egrp == gidx)
    el = jnp.where(in_grp, logits, neg)
    v1 = jnp.max(el, axis=-1, keepdims=True)
    i1 = jnp.min(jnp.where(el == v1, lane_f, big), axis=-1, keepdims=True)
    el2 = jnp.where(lane_f == i1, neg, el)
    v2 = jnp.max(el2, axis=-1, keepdims=True)
    i2 = jnp.min(jnp.where(el2 == v2, lane_f, big), axis=-1, keepdims=True)
    p = jnp.exp(v2 - v1)
    s1 = 1.0 / (1.0 + p)
    gate1 = gw * s1
    gate2 = gw * (p * s1)
    g_ref[...] = jnp.where(lanef == 0, gate1, jnp.where(lanef == 1, gate2, 0.0))
    emat = jnp.where(lanef == 0, i1 - N_GROUPS, jnp.where(lanef == 1, i2 - N_GROUPS, 0.0))
    e_ref[...] = emat.T[0:8, :].astype(I32)


def _mixer_call(x, gmix, wbig, wdt, convw, convb, dtb, alog, dskip, ssdn, scw, scn, wout, gffn,
                wrh, wrl, tri, e3, gsum):
    B, L, D = x.shape
    T = T_MIX
    nt = L // T
    n_tok = B * L

    def const(shape):
        return pl.BlockSpec(shape, lambda b, t: (0,) * len(shape))

    in_specs = [
        pl.BlockSpec((1, T, D), lambda b, t: (b, t, 0)),
        const((1, D)),
        const(wbig.shape), const(wdt.shape), const(convw.shape), const(convb.shape),
        const(dtb.shape), const(alog.shape), const(dskip.shape), const(ssdn.shape),
        const(scw.shape), const(scn.shape), const(wout.shape), const(gffn.shape),
        const(wrh.shape), const(wrl.shape), const(tri.shape), const(e3.shape), const(gsum.shape),
    ]
    out_shape = [
        jax.ShapeDtypeStruct((B, L, D), F32),
        jax.ShapeDtypeStruct((n_tok * ROW_TILES, LANES), F32),
        jax.ShapeDtypeStruct((8, n_tok), I32),
        jax.ShapeDtypeStruct((n_tok, LANES), F32),
    ]
    out_specs = [
        pl.BlockSpec((1, T, D), lambda b, t: (b, t, 0)),
        pl.BlockSpec((T * ROW_TILES, LANES), lambda b, t: (b * nt + t, 0)),
        pl.BlockSpec((8, T), lambda b, t: (0, b * nt + t)),
        pl.BlockSpec((T, LANES), lambda b, t: (b * nt + t, 0)),
    ]
    return pl.pallas_call(
        _mixer_kernel,
        grid=(B, nt),
        in_specs=in_specs,
        out_specs=out_specs,
        out_shape=out_shape,
        scratch_shapes=[
            pltpu.VMEM((T + 8, XBC), F32),
            pltpu.VMEM((T + 8, SC_WIDTH), F32),
            pltpu.VMEM((N_BC_GROUPS, STATE, 512), F32),
        ],
        compiler_params=pltpu.CompilerParams(
            dimension_semantics=("arbitrary", "arbitrary"),
            vmem_limit_bytes=VMEM_LIMIT),
        name="mixer",
    )(x, gmix, wbig, wdt, convw, convb, dtb, alog, dskip, ssdn, scw, scn, wout, gffn, wrh, wrl,
      tri, e3, gsum)


def _moe_kernel(bexp_ref, nact_ref, inv_cur_ref, inv_nxt_ref, hn2_ref, wg_ref, wu_ref, wd_ref,
                y2_ref, xbuf, ybuf, wgb, wub, wdb, gsem, ssem, *, n_tok):
    R = R_BLK
    slab = R * ROW_TILES
    b = pl.program_id(0)
    nact = nact_ref[0]
    slot = lax.rem(b, 2)
    dummy_base = 2 * n_tok

    def gather_start(inv_ref, sl):
        for j in range(R):
            a = inv_ref[0, 0, j]
            tok = jnp.where(a >= 0, jnp.bitwise_and(a, n_tok - 1), 0)
            pltpu.make_async_copy(
                hn2_ref.at[pl.ds(pl.multiple_of(tok * ROW_TILES, ROW_TILES), ROW_TILES), :],
                xbuf.at[pl.ds(pl.multiple_of(sl * slab + j * ROW_TILES, ROW_TILES), ROW_TILES), :],
                gsem.at[sl]).start(priority=j % 2)

    def gather_wait(sl):
        pltpu.make_async_copy(
            hn2_ref.at[pl.ds(0, slab), :],
            xbuf.at[pl.ds(pl.multiple_of(sl * slab, slab), slab), :],
            gsem.at[sl]).wait()

    def scatter_start(inv_ref, sl):
        for j in range(R):
            a = inv_ref[0, 0, j]
            dst = jnp.where(a >= 0, a, dummy_base + j)
            pltpu.make_async_copy(
                ybuf.at[pl.ds(pl.multiple_of(sl * slab + j * ROW_TILES, ROW_TILES), ROW_TILES), :],
                y2_ref.at[pl.ds(pl.multiple_of(dst * ROW_TILES, ROW_TILES), ROW_TILES), :],
                ssem.at[sl]).start(priority=j % 2)

    def scatter_wait(sl):
        pltpu.make_async_copy(
            ybuf.at[pl.ds(pl.multiple_of(sl * slab, slab), slab), :],
            y2_ref.at[pl.ds(0, slab), :],
            ssem.at[sl]).wait()

    @pl.when(b == 0)
    def _():
        gather_start(inv_cur_ref, 0)
        ybuf[pl.ds(slab, slab), :] = jnp.zeros((slab, LANES), F32)
        fill = pltpu.make_async_copy(
            ybuf.at[pl.ds(slab, slab), :],
            y2_ref.at[pl.ds(dummy_base * ROW_TILES, slab), :],
            ssem.at[1])
        fill.start()
        fill.wait()

    prev_e = bexp_ref[jnp.maximum(b - 1, 0)]

    @pl.when((b < nact) & ((b == 0) | (bexp_ref[b] != prev_e)))
    def _():
        wgb[...] = wg_ref[0].astype(BF16)
        wub[...] = wu_ref[0].astype(BF16)
        wdb[...] = wd_ref[0].astype(BF16)

    @pl.when((b < nact) & (b >= 2))
    def _():
        scatter_wait(slot)

    @pl.when(b < nact)
    def _():
        gather_wait(slot)
        gather_start(inv_nxt_ref, 1 - slot)
        base = pl.multiple_of(slot * slab, slab)
        xcat = jnp.concatenate(
            [xbuf[pl.ds(base + s, R, stride=ROW_TILES), :] for s in range(ROW_TILES)], axis=1)
        xb = xcat.astype(BF16)
        gg = _dot(xb, wgb[...])
        uu = _dot(xb, wub[...])
        act = (_silu(gg) * uu).astype(BF16)
        yy = _dot(act, wdb[...])
        for s in range(ROW_TILES):
            ybuf[pl.ds(base + s, R, stride=ROW_TILES), :] = yy[:, LANES * s:LANES * (s + 1)]
        scatter_start(inv_cur_ref, slot)

    @pl.when(b == nact)
    def _():
        gather_wait(slot)
        scatter_wait(1 - slot)

        @pl.when(nact >= 2)
        def _():
            scatter_wait(slot)


def _moe_call(bexp, nact, inv3, hn2, wg, wu, wd, n_tok):
    nb = inv3.shape[0]
    R = R_BLK
    slab = R * ROW_TILES
    grid_spec = pltpu.PrefetchScalarGridSpec(
        num_scalar_prefetch=2,
        grid=(nb,),
        in_specs=[
            pl.BlockSpec((1, 1, R), lambda b, be, na: (b, 0, 0), memory_space=pltpu.SMEM),
            pl.BlockSpec((1, 1, R), lambda b, be, na: (jnp.minimum(b + 1, nb - 1), 0, 0),
                         memory_space=pltpu.SMEM),
            pl.BlockSpec(memory_space=pl.ANY),
            pl.BlockSpec((1, D_MODEL, D_FF), lambda b, be, na: (be[b], 0, 0)),
            pl.BlockSpec((1, D_MODEL, D_FF), lambda b, be, na: (be[b], 0, 0)),
            pl.BlockSpec((1, D_FF, D_MODEL), lambda b, be, na: (be[b], 0, 0)),
        ],
        out_specs=pl.BlockSpec(memory_space=pl.ANY),
        scratch_shapes=[
            pltpu.VMEM((2 * slab, LANES), F32),
            pltpu.VMEM((2 * slab, LANES), F32),
            pltpu.VMEM((D_MODEL, D_FF), BF16),
            pltpu.VMEM((D_MODEL, D_FF), BF16),
            pltpu.VMEM((D_FF, D_MODEL), BF16),
            pltpu.SemaphoreType.DMA((2,)),
            pltpu.SemaphoreType.DMA((2,)),
        ],
    )
    return pl.pallas_call(
        functools.partial(_moe_kernel, n_tok=n_tok),
        grid_spec=grid_spec,
        out_shape=jax.ShapeDtypeStruct(((2 * n_tok + R) * ROW_TILES, LANES), F32),
        compiler_params=pltpu.CompilerParams(
            dimension_semantics=("arbitrary",),
            vmem_limit_bytes=VMEM_LIMIT),
        name="moe",
    )(bexp, nact, inv3, inv3, hn2, wg, wu, wd)


def _combine_kernel(h_ref, y0_ref, y1_ref, g_ref, fn_ref, o_ref):
    T = T_CMB
    h = h_ref[...]
    y0 = jnp.concatenate([y0_ref[pl.ds(s, T, stride=ROW_TILES), :] for s in range(ROW_TILES)], axis=1)
    y1 = jnp.concatenate([y1_ref[pl.ds(s, T, stride=ROW_TILES), :] for s in range(ROW_TILES)], axis=1)
    g = g_ref[...]
    v = h + (y0 * g[:, 0:1] + y1 * g[:, 1:2])
    ms = jnp.mean(v * v, axis=-1, keepdims=True)
    o_ref[...] = v * lax.rsqrt(ms + EPS) * fn_ref[...]


def _combine_call(h2d, y2, gates, fnorm):
    n_tok, D = h2d.shape
    T = T_CMB
    nt = n_tok // T
    return pl.pallas_call(
        _combine_kernel,
        grid=(nt,),
        in_specs=[
            pl.BlockSpec((T, D), lambda i: (i, 0)),
            pl.BlockSpec((T * ROW_TILES, LANES), lambda i: (i, 0)),
            pl.BlockSpec((T * ROW_TILES, LANES), lambda i: (nt + i, 0)),
            pl.BlockSpec((T, LANES), lambda i: (i, 0)),
            pl.BlockSpec((1, D), lambda i: (0, 0)),
        ],
        out_specs=pl.BlockSpec((T, D), lambda i: (i, 0)),
        out_shape=jax.ShapeDtypeStruct((n_tok, D), F32),
        compiler_params=pltpu.CompilerParams(dimension_semantics=("arbitrary",)),
        name="combine",
    )(h2d, y2, y2, gates, fnorm)


def _plan_kernel(e_ref, ux_ref, ones_ref, lx_ref, dest_ref, bexp_ref, nact_ref):
    rows = e_ref.shape[0]
    ev = e_ref[...]
    bidx = (lax.broadcasted_iota(I32, (SUBLANES, LANES), 0) * LANES
            + lax.broadcasted_iota(I32, (SUBLANES, LANES), 1))
    dest = jnp.zeros((rows, LANES), F32)
    bexp = jnp.zeros((SUBLANES, LANES), I32)
    pstart = jnp.zeros((1, LANES), F32)
    e_last = jnp.zeros((1, LANES), I32)
    for e in range(N_EXPERTS):
        ohb = ev == e
        oh = jnp.where(ohb, 1.0, 0.0).astype(BF16)
        within = _dot(oh, ux_ref[...])
        rtot = _dot(oh, ones_ref[...])
        rpre = _dot(lx_ref[...], rtot.astype(BF16))
        cnt = rpre[rows - 1:rows, :] + rtot[rows - 1:rows, :]
        dest = dest + jnp.where(ohb, within + rpre + pstart, 0.0)
        cnt_i = cnt.astype(I32)
        padded = lax.shift_left(lax.shift_right_logical(cnt_i + (R_BLK - 1), 7), 7)
        pstart = pstart + padded.astype(F32)
        pend_i = pstart.astype(I32)
        bexp = bexp + jnp.where(pend_i[0:1, 0:1] <= bidx * R_BLK, 1, 0)
        e_last = jnp.where(cnt_i > 0, e, e_last)
    nact = lax.shift_right_logical(pstart.astype(I32), 7)
    dest_ref[...] = dest.astype(I32)
    bexp_ref[...] = jnp.where(bidx < nact[0:1, 0:1], jnp.minimum(bexp, N_EXPERTS - 1), e_last[0:1, 0:1])
    nact_ref[...] = jnp.broadcast_to(nact, (SUBLANES, LANES))


def _invert_kernel(dest_ref, neg_ref, inv_ref):
    n_asg = dest_ref.shape[0]
    pltpu.sync_copy(neg_ref, inv_ref)

    def put(a, c):
        inv_ref[dest_ref[a]] = a
        return c

    lax.fori_loop(0, n_asg, put, 0, unroll=16)


def _plan(e_rows, n_tok, nb):
    R = R_BLK
    assert R == LANES
    n_asg = 2 * n_tok
    rows = n_asg // LANES
    emat = e_rows[0:2].reshape(rows, LANES)
    li = jnp.arange(LANES)
    ux = (li[:, None] < li[None, :]).astype(BF16)
    ones = jnp.ones((LANES, LANES), BF16)
    ri = jnp.arange(rows)
    lx = (ri[:, None] > ri[None, :]).astype(BF16)
    dest, bexp, nact = pl.pallas_call(
        _plan_kernel,
        out_shape=[jax.ShapeDtypeStruct((rows, LANES), I32),
                   jax.ShapeDtypeStruct((SUBLANES, LANES), I32),
                   jax.ShapeDtypeStruct((SUBLANES, LANES), I32)],
        name="plan",
    )(emat, ux, ones, lx)
    inv = pl.pallas_call(
        _invert_kernel,
        in_specs=[pl.BlockSpec(memory_space=pltpu.SMEM), pl.BlockSpec(memory_space=pl.ANY)],
        out_specs=pl.BlockSpec(memory_space=pltpu.SMEM),
        out_shape=jax.ShapeDtypeStruct((nb * R,), I32),
        name="invert",
    )(dest.reshape(n_asg), jnp.full((nb * R,), -1, I32))
    return bexp.reshape(-1)[0:nb], nact[0, 0:1], inv.reshape(nb, 1, R)


def kernel(x, norm_mix, w_in, ssd_conv_w, ssd_conv_b, dt_bias, a_log, d_skip, ssd_norm, sc_conv_w,
           sc_norm, w_out, norm_ffn, w_router_group, w_router_expert, w_gate, w_up, w_down, final_norm):
    B, L, D = x.shape
    n_tok = B * L
    depth = norm_mix.shape[0]
    assert depth == 1 and D == D_MODEL and (n_tok & (n_tok - 1)) == 0
    nb = (2 * n_tok) // R_BLK + N_EXPERTS + 1

    o1 = SSD_WIDTH
    o2 = o1 + XBC
    o3 = o2 + N_HEADS
    wi = w_in[0]
    wbig = jnp.concatenate([wi[:, 0:o2], wi[:, o3:]], axis=1).astype(BF16)
    wdt = jnp.pad(wi[:, o2:o3], ((0, 0), (0, LANES - N_HEADS))).astype(BF16)
    pad_h = (0, LANES - N_HEADS)
    dtb = jnp.pad(dt_bias[0], pad_h).reshape(1, LANES)
    alog = jnp.pad(a_log[0], pad_h).reshape(1, LANES)
    dskip = jnp.repeat(d_skip[0], HEAD_DIM).reshape(1, SSD_WIDTH)

    wre = jnp.transpose(w_router_expert[0], (1, 0, 2)).reshape(D, N_EXPERTS)
    wr = jnp.pad(jnp.concatenate([w_router_group[0], wre], axis=1),
                 ((0, 0), (0, LANES - N_GROUPS - N_EXPERTS)))
    wrh = wr.astype(BF16)
    wrl = (wr - wrh.astype(F32)).astype(BF16)

    ri = jnp.arange(T_MIX)
    tri = (ri[:, None] >= ri[None, :]).astype(BF16)
    er = jnp.arange(LANES)
    ec = jnp.arange(SSD_WIDTH)
    e3 = ((er[:, None] < 48) & ((er[:, None] % 16) == (ec[None, :] // HEAD_DIM))).astype(BF16)
    gsum = ((ec[:, None] // HEAD_DIM) == er[None, :]).astype(BF16)

    h, hn2, e_rows, gates = _mixer_call(
        x, norm_mix[0].reshape(1, D), wbig, wdt, ssd_conv_w[0], ssd_conv_b[0].reshape(1, XBC), dtb, alog,
        dskip, ssd_norm[0].reshape(1, SSD_WIDTH), sc_conv_w[0], sc_norm[0].reshape(1, SC_WIDTH),
        w_out[0].astype(BF16), norm_ffn[0].reshape(1, D), wrh, wrl, tri, e3, gsum)

    bexp, nact, inv3 = _plan(e_rows, n_tok, nb)
    y2 = _moe_call(bexp, nact, inv3, hn2, w_gate[0], w_up[0], w_down[0], n_tok)
    out = _combine_call(h.reshape(n_tok, D), y2, gates, final_norm.reshape(1, D))
    return out.reshape(B, L, D)
```

```python
import functools

import jax
import jax.numpy as jnp
from jax import lax
from jax.experimental import pallas as pl
from jax.experimental.pallas import tpu as pltpu

F32 = jnp.float32
BF16 = jnp.bfloat16
I32 = jnp.int32

EPS = 1e-6
D_MODEL = 1024
N_HEADS = 16
HEAD_DIM = 64
N_BC_GROUPS = 2
STATE = 128
SSD_WIDTH = 1024
XBC = SSD_WIDTH + 2 * N_BC_GROUPS * STATE
SC_WIDTH = 1024
SC_GROUPS = 16
N_GROUPS = 4
EPG = 8
N_EXPERTS = 32
D_FF = 512

LANES = 128
SUBLANES = 8
ROW_TILES = D_MODEL // LANES

T_MIX = 256
R_BLK = 256
R_SHIFT = 8
R_HALF = R_BLK // 2
T_CMB = 512

VMEM_LIMIT = 56 * 1024 * 1024


def _dot(a, b):
    return jnp.dot(a, b, preferred_element_type=F32)


def _split3(v):
    p1 = v.astype(BF16).astype(F32)
    r1 = v - p1
    p2 = r1.astype(BF16).astype(F32)
    p3 = (r1 - p2).astype(BF16).astype(F32)
    return p1, p2, p3


def _pack3(v):
    p1, p2, p3 = _split3(v)
    return (p1 + pltpu.roll(p2, 16, 1) + pltpu.roll(p3, 32, 1)).astype(BF16)


def _silu(v):
    return v * jax.nn.sigmoid(v)


def _mixer_kernel(x_ref, gmix_ref, wbig_ref, wdt_ref, convw_ref, convb_ref, dtb_ref, alog_ref,
                  dskip_ref, ssdn_ref, scw_ref, scn_ref, wout_ref, gffn_ref, wrh_ref, wrl_ref,
                  tri_ref, e3_ref, gsum_ref,
                  h_ref, hn2_ref, e_ref, g_ref,
                  cbuf, sbuf, st_ref):
    T = T_MIX
    t = pl.program_id(1)

    @pl.when(t == 0)
    def _():
        cbuf[0:8, :] = jnp.zeros((8, XBC), F32)
        sbuf[0:8, :] = jnp.zeros((8, SC_WIDTH), F32)
        st_ref[...] = jnp.zeros(st_ref.shape, F32)

    @pl.when(t > 0)
    def _():
        cbuf[0:8, :] = cbuf[T:T + 8, :]
        sbuf[0:8, :] = sbuf[T:T + 8, :]

    x = x_ref[0]
    ms = jnp.mean(x * x, axis=-1, keepdims=True)
    hn = (x * lax.rsqrt(ms + EPS) * gmix_ref[...]).astype(BF16)

    z = _dot(hn, wbig_ref[:, 0:1024])
    xbc = _dot(hn, wbig_ref[:, 1024:2560])
    scb = _dot(hn, wbig_ref[:, 2560:3584])
    scc = _dot(hn, wbig_ref[:, 3584:4608])
    scv = _dot(hn, wbig_ref[:, 4608:5632])
    dt_raw = _dot(hn, wdt_ref[...])

    cbuf[8:8 + T, :] = xbc
    cw = convw_ref[...]
    acc = convb_ref[...] + cw[3:4, :] * xbc
    for k in range(3):
        acc = acc + cw[k:k + 1, :] * cbuf[5 + k:5 + k + T, :]
    xact = _silu(acc)
    xs = xact[:, 0:SSD_WIDTH]

    lane = lax.broadcasted_iota(I32, (1, LANES), 1)
    hmask = lane < N_HEADS
    a = jnp.where(hmask, -jnp.exp(alog_ref[...]), 0.0)
    dtv = dt_raw + dtb_ref[...]
    dt = jnp.where(hmask, jnp.maximum(dtv, 0.0) + jnp.log1p(jnp.exp(-jnp.abs(dtv))), 0.0)
    adt = dt * a
    c3 = _dot(tri_ref[...], _pack3(adt))
    ac = jnp.where(hmask, c3 + pltpu.roll(c3, LANES - 16, 1) + pltpu.roll(c3, LANES - 32, 1), 0.0)
    ac_last = ac[T - 1:T, :]
    eac = jnp.where(hmask, jnp.exp(ac), 0.0)
    wdt = dt * jnp.exp(ac_last - ac)

    stacked = jnp.concatenate([_pack3(dt), _pack3(wdt), _pack3(eac)], axis=0)
    ex = _dot(stacked, e3_ref[...])
    dt_e = ex[0:T]
    wdt_e = ex[T:2 * T]
    eac_e = ex[2 * T:3 * T]
    xdt = (xs * dt_e).astype(BF16)
    xdtw = (xs * wdt_e).astype(BF16)

    ac_rows = ac.T
    rr = lax.broadcasted_iota(I32, (T, T), 0)
    cc = lax.broadcasted_iota(I32, (T, T), 1)
    causal = rr >= cc
    cblk = lax.shift_right_logical(lax.broadcasted_iota(I32, (T, 4 * HEAD_DIM), 1), 6)

    y_parts = []
    for g in range(N_BC_GROUPS):
        bg = xact[:, SSD_WIDTH + STATE * g:SSD_WIDTH + STATE * (g + 1)]
        cg = xact[:, SSD_WIDTH + 2 * STATE + STATE * g:SSD_WIDTH + 2 * STATE + STATE * (g + 1)]
        bb = bg.astype(BF16)
        cbf = cg.astype(BF16)
        cb = lax.dot_general(cbf, bb, (((1,), (1,)), ((), ())), preferred_element_type=F32)
        st = st_ref[g]
        yoff = _dot(cbf, st.astype(BF16))
        dec = eac_e[T - 1:T, 512 * g:512 * (g + 1)]
        bgt = bg.T.astype(BF16)
        st_ref[g] = st * dec + _dot(bgt, xdtw[:, 512 * g:512 * (g + 1)])
        for q in range(2):
            ms_list = []
            for r in range(4):
                hh = 8 * g + 4 * q + r
                seg = ac[:, hh:hh + 1] - ac_rows[hh:hh + 1, :]
                lh = jnp.exp(jnp.where(causal, seg, -jnp.inf))
                ms_list.append((cb * lh).astype(BF16))
            lhs = jnp.concatenate(ms_list, axis=1)
            lo = 512 * g + 256 * q
            x4 = xdt[:, lo:lo + 256]
            rhs = jnp.concatenate(
                [jnp.where(cblk == r, x4, jnp.zeros_like(x4)) for r in range(4)], axis=0)
            yd = _dot(lhs, rhs)
            y_parts.append(yd + eac_e[:, lo:lo + 256] * yoff[:, 256 * q:256 * (q + 1)])
    y = jnp.concatenate(y_parts, axis=1) + dskip_ref[...] * xs

    v = y * _silu(z)
    outs = []
    for g in range(N_BC_GROUPS):
        vg = v[:, 512 * g:512 * (g + 1)]
        msg = jnp.mean(vg * vg, axis=-1, keepdims=True)
        outs.append(vg * lax.rsqrt(msg + EPS))
    ssd_out = (jnp.concatenate(outs, axis=1) * ssdn_ref[...]).astype(BF16)

    u = scc * scv
    sbuf[8:8 + T, :] = u
    sw = scw_ref[...]
    conv = sw[2:3, :] * u + sw[1:2, :] * sbuf[7:7 + T, :] + sw[0:1, :] * sbuf[6:6 + T, :]
    sc = scb * conv
    gs = _dot((sc * sc).astype(BF16), gsum_ref[...])
    rstd = jnp.where(hmask, lax.rsqrt(gs * (1.0 / HEAD_DIM) + EPS), 0.0)
    rstd_e = _dot(_pack3(rstd), e3_ref[...])
    sc_out = (sc * rstd_e * scn_ref[...]).astype(BF16)

    mix = _dot(ssd_out, wout_ref[0:1024, :]) + _dot(sc_out, wout_ref[1024:2048, :])
    h = x + mix
    h_ref[0] = h

    ms2 = jnp.mean(h * h, axis=-1, keepdims=True)
    hn2 = h * lax.rsqrt(ms2 + EPS) * gffn_ref[...]
    for s in range(ROW_TILES):
        hn2_ref[pl.ds(s, T, stride=ROW_TILES), :] = hn2[:, LANES * s:LANES * (s + 1)]
    hi = hn2.astype(BF16)
    lo_ = (hn2 - hi.astype(F32)).astype(BF16)
    logits = _dot(hi, wrh_ref[...]) + (_dot(lo_, wrh_ref[...]) + _dot(hi, wrl_ref[...]))

    lanef = lax.broadcasted_iota(I32, (T, LANES), 1)
    lane_f = lanef.astype(F32)
    neg = -jnp.inf
    big = 1e9
    gl = jnp.where(lanef < N_GROUPS, logits, neg)
    gmax = jnp.max(gl, axis=-1, keepdims=True)
    gidx = jnp.min(jnp.where(gl == gmax, lane_f, big), axis=-1, keepdims=True)
    gsum = jnp.sum(jnp.where(lanef < N_GROUPS, jnp.exp(logits - gmax), 0.0), axis=-1, keepdims=True)
    gw = 1.0 / gsum
    egrp = lax.shift_right_logical(lanef - N_GROUPS, 3).astype(F32)
    in_grp = (lanef >= N_GROUPS) & (lanef < N_GROUPS + N_EXPERTS) & (egrp == gidx)
    el = jnp.where(in_grp, logits, neg)
    v1 = jnp.max(el, axis=-1, keepdims=True)
    i1 = jnp.min(jnp.where(el == v1, lane_f, big), axis=-1, keepdims=True)
    el2 = jnp.where(lane_f == i1, neg, el)
    v2 = jnp.max(el2, axis=-1, keepdims=True)
    i2 = jnp.min(jnp.where(el2 == v2, lane_f, big), axis=-1, keepdims=True)
    p = jnp.exp(v2 - v1)
    s1 = 1.0 / (1.0 + p)
    gate1 = gw * s1
    gate2 = gw * (p * s1)
    g_ref[...] = jnp.where(lanef == 0, gate1, jnp.where(lanef == 1, gate2, 0.0))
    emat = jnp.where(lanef == 0, i1 - N_GROUPS, jnp.where(lanef == 1, i2 - N_GROUPS, 0.0))
    e_ref[...] = emat.T[0:8, :].astype(I32)


def _mixer_call(x, gmix, wbig, wdt, convw, convb, dtb, alog, dskip, ssdn, scw, scn, wout, gffn,
                wrh, wrl, tri, e3, gsum):
    B, L, D = x.shape
    T = T_MIX
    nt = L // T
    n_tok = B * L

    def const(shape):
        return pl.BlockSpec(shape, lambda b, t: (0,) * len(shape))

    in_specs = [
        pl.BlockSpec((1, T, D), lambda b, t: (b, t, 0)),
        const((1, D)),
        const(wbig.shape), const(wdt.shape), const(convw.shape), const(convb.shape),
        const(dtb.shape), const(alog.shape), const(dskip.shape), const(ssdn.shape),
        const(scw.shape), const(scn.shape), const(wout.shape), const(gffn.shape),
        const(wrh.shape), const(wrl.shape), const(tri.shape), const(e3.shape), const(gsum.shape),
    ]
    out_shape = [
        jax.ShapeDtypeStruct((B, L, D), F32),
        jax.ShapeDtypeStruct((n_tok * ROW_TILES, LANES), F32),
        jax.ShapeDtypeStruct((8, n_tok), I32),
        jax.ShapeDtypeStruct((n_tok, LANES), F32),
    ]
    out_specs = [
        pl.BlockSpec((1, T, D), lambda b, t: (b, t, 0)),
        pl.BlockSpec((T * ROW_TILES, LANES), lambda b, t: (b * nt + t, 0)),
        pl.BlockSpec((8, T), lambda b, t: (0, b * nt + t)),
        pl.BlockSpec((T, LANES), lambda b, t: (b * nt + t, 0)),
    ]
    return pl.pallas_call(
        _mixer_kernel,
        grid=(B, nt),
        in_specs=in_specs,
        out_specs=out_specs,
        out_shape=out_shape,
        scratch_shapes=[
            pltpu.VMEM((T + 8, XBC), F32),
            pltpu.VMEM((T + 8, SC_WIDTH), F32),
            pltpu.VMEM((N_BC_GROUPS, STATE, 512), F32),
        ],
        compiler_params=pltpu.CompilerParams(
            dimension_semantics=("arbitrary", "arbitrary"),
            vmem_limit_bytes=VMEM_LIMIT),
        name="mixer",
    )(x, gmix, wbig, wdt, convw, convb, dtb, alog, dskip, ssdn, scw, scn, wout, gffn, wrh, wrl,
      tri, e3, gsum)


def _moe_kernel(bstart_ref, nblk_ref, inv_ref, hn2_ref, wg_ref, wu_ref, wd_ref,
                y2_ref, xbuf, ybuf, wgb, wub, wdb, gsem, ssem, *, n_tok):
    R = R_BLK
    H = R_HALF
    slab = R * ROW_TILES
    e = pl.program_id(0)
    g0 = bstart_ref[e]
    nb_e = nblk_ref[e]

    def gather_start(g, sl, lo, hi):
        for j in range(lo, hi):
            a = inv_ref[g * R + j]
            tok = jnp.bitwise_and(a, n_tok - 1)
            pltpu.make_async_copy(
                hn2_ref.at[tok],
                xbuf.at[pl.ds(pl.multiple_of(sl * slab + j * ROW_TILES, ROW_TILES), ROW_TILES), :],
                gsem.at[sl]).start(priority=j % 2)

    def scatter_start(g, sl, lo, hi):
        for j in range(lo, hi):
            a = inv_ref[g * R + j]
            pltpu.make_async_copy(
                ybuf.at[pl.ds(pl.multiple_of(sl * slab + j * ROW_TILES, ROW_TILES), ROW_TILES), :],
                y2_ref.at[a],
                ssem.at[sl]).start(priority=j % 2)

    def slab_wait(buf, sem, sl):
        view = buf.at[pl.ds(pl.multiple_of(sl * slab, slab), slab), :]
        pltpu.make_async_copy(view, view, sem.at[sl]).wait()

    @pl.when(e == 0)
    def _():
        gather_start(0, 0, 0, R)
        ybuf[...] = jnp.zeros(ybuf.shape, F32)
        for sl in range(2):
            for j in range(R):
                pltpu.make_async_copy(
                    ybuf.at[pl.ds(sl * slab + j * ROW_TILES, ROW_TILES), :],
                    y2_ref.at[2 * n_tok + sl * R + j],
                    ssem.at[sl]).start(priority=j % 2)

    @pl.when(nb_e > 0)
    def _():
        wgb[...] = wg_ref[0].astype(BF16)
        wub[...] = wu_ref[0].astype(BF16)
        wdb[...] = wd_ref[0].astype(BF16)

    def block(i, carry):
        g = g0 + i
        slot = jnp.bitwise_and(g, 1)
        base = pl.multiple_of(slot * slab, slab)
        slab_wait(xbuf, gsem, slot)
        slab_wait(ybuf, ssem, slot)
        xs = []
        for hf in range(2):
            xs.append(jnp.concatenate(
                [xbuf[pl.ds(base + hf * H * ROW_TILES + s, H, stride=ROW_TILES), :]
                 for s in range(ROW_TILES)], axis=1).astype(BF16))
        for hf in range(2):
            gather_start(g + 1, 1 - slot, hf * H, (hf + 1) * H)
            gg = _dot(xs[hf], wgb[...])
            uu = _dot(xs[hf], wub[...])
            act = (_silu(gg) * uu).astype(BF16)
            yy = _dot(act, wdb[...])
            for s in range(ROW_TILES):
                ybuf[pl.ds(base + hf * H * ROW_TILES + s, H, stride=ROW_TILES), :] = (
                    yy[:, LANES * s:LANES * (s + 1)])
            scatter_start(g, slot, hf * H, (hf + 1) * H)
        return carry

    lax.fori_loop(0, nb_e, block, 0)

    @pl.when(e == pl.num_programs(0) - 1)
    def _():
        g_end = g0 + nb_e
        slab_wait(xbuf, gsem, jnp.bitwise_and(g_end, 1))
        slab_wait(ybuf, ssem, 0)
        slab_wait(ybuf, ssem, 1)


def _moe_call(bstart, nblk, inv, hn2, wg, wu, wd, n_tok):
    R = R_BLK
    slab = R * ROW_TILES
    grid_spec = pltpu.PrefetchScalarGridSpec(
        num_scalar_prefetch=2,
        grid=(N_EXPERTS,),
        in_specs=[
            pl.BlockSpec(memory_space=pltpu.SMEM),
            pl.BlockSpec(memory_space=pl.ANY),
            pl.BlockSpec((1, D_MODEL, D_FF), lambda e, bs, nb: (e, 0, 0)),
            pl.BlockSpec((1, D_MODEL, D_FF), lambda e, bs, nb: (e, 0, 0)),
            pl.BlockSpec((1, D_FF, D_MODEL), lambda e, bs, nb: (e, 0, 0)),
        ],
        out_specs=pl.BlockSpec(memory_space=pl.ANY),
        scratch_shapes=[
            pltpu.VMEM((2 * slab, LANES), F32),
            pltpu.VMEM((2 * slab, LANES), F32),
            pltpu.VMEM((D_MODEL, D_FF), BF16),
            pltpu.VMEM((D_MODEL, D_FF), BF16),
            pltpu.VMEM((D_FF, D_MODEL), BF16),
            pltpu.SemaphoreType.DMA((2,)),
            pltpu.SemaphoreType.DMA((2,)),
        ],
    )
    return pl.pallas_call(
        functools.partial(_moe_kernel, n_tok=n_tok),
        grid_spec=grid_spec,
        out_shape=jax.ShapeDtypeStruct((2 * n_tok + 2 * R, ROW_TILES, LANES), F32),
        compiler_params=pltpu.CompilerParams(
            dimension_semantics=("arbitrary",),
            vmem_limit_bytes=VMEM_LIMIT),
        name="moe",
    )(bstart, nblk, inv, hn2, wg, wu, wd)


def _combine_kernel(h_ref, y0_ref, y1_ref, g_ref, fn_ref, o_ref):
    T = T_CMB
    h = h_ref[...]
    y0 = jnp.concatenate([y0_ref[pl.ds(s, T, stride=ROW_TILES), :] for s in range(ROW_TILES)], axis=1)
    y1 = jnp.concatenate([y1_ref[pl.ds(s, T, stride=ROW_TILES), :] for s in range(ROW_TILES)], axis=1)
    g = g_ref[...]
    v = h + (y0 * g[:, 0:1] + y1 * g[:, 1:2])
    ms = jnp.mean(v * v, axis=-1, keepdims=True)
    o_ref[...] = v * lax.rsqrt(ms + EPS) * fn_ref[...]


def _combine_call(h2d, y2, gates, fnorm):
    n_tok, D = h2d.shape
    T = T_CMB
    nt = n_tok // T
    return pl.pallas_call(
        _combine_kernel,
        grid=(nt,),
        in_specs=[
            pl.BlockSpec((T, D), lambda i: (i, 0)),
            pl.BlockSpec((T * ROW_TILES, LANES), lambda i: (i, 0)),
            pl.BlockSpec((T * ROW_TILES, LANES), lambda i: (nt + i, 0)),
            pl.BlockSpec((T, LANES), lambda i: (i, 0)),
            pl.BlockSpec((1, D), lambda i: (0, 0)),
        ],
        out_specs=pl.BlockSpec((T, D), lambda i: (i, 0)),
        out_shape=jax.ShapeDtypeStruct((n_tok, D), F32),
        compiler_params=pltpu.CompilerParams(dimension_semantics=("arbitrary",)),
        name="combine",
    )(h2d, y2, y2, gates, fnorm)


def _plan_kernel(e_ref, ux_ref, ones_ref, lx_ref, dest_ref, bstart_ref, nblk_ref):
    rows = e_ref.shape[0]
    ev = e_ref[...]
    lane8 = lax.broadcasted_iota(I32, (SUBLANES, LANES), 1)
    dest = jnp.zeros((rows, LANES), F32)
    bstart = jnp.zeros((SUBLANES, LANES), I32)
    nblk = jnp.zeros((SUBLANES, LANES), I32)
    pstart = jnp.zeros((1, LANES), F32)
    for e in range(N_EXPERTS):
        ohb = ev == e
        oh = jnp.where(ohb, 1.0, 0.0).astype(BF16)
        within = _dot(oh, ux_ref[...])
        rtot = _dot(oh, ones_ref[...])
        rpre = _dot(lx_ref[...], rtot.astype(BF16))
        cnt = rpre[rows - 1:rows, :] + rtot[rows - 1:rows, :]
        dest = dest + jnp.where(ohb, within + rpre + pstart, 0.0)
        cnt_i = cnt.astype(I32)
        nb_e = lax.shift_right_logical(cnt_i + (R_BLK - 1), R_SHIFT)
        bstart = jnp.where(lane8 == e, lax.shift_right_logical(pstart.astype(I32), R_SHIFT), bstart)
        nblk = jnp.where(lane8 == e, nb_e, nblk)
        pstart = pstart + lax.shift_left(nb_e, R_SHIFT).astype(F32)
    dest_ref[...] = dest.astype(I32)
    bstart_ref[...] = bstart
    nblk_ref[...] = nblk


def _invert_kernel(dest_ref, init_ref, inv_ref):
    n_asg = dest_ref.shape[0]
    pltpu.sync_copy(init_ref, inv_ref)

    def put(a, c):
        inv_ref[dest_ref[a]] = a
        return c

    lax.fori_loop(0, n_asg, put, 0, unroll=16)


def _plan(e_rows, n_tok, nb):
    R = R_BLK
    n_asg = 2 * n_tok
    rows = n_asg // LANES
    emat = e_rows[0:2].reshape(rows, LANES)
    li = jnp.arange(LANES)
    ux = (li[:, None] < li[None, :]).astype(BF16)
    ones = jnp.ones((LANES, LANES), BF16)
    ri = jnp.arange(rows)
    lx = (ri[:, None] > ri[None, :]).astype(BF16)
    dest, bstart, nblk = pl.pallas_call(
        _plan_kernel,
        out_shape=[jax.ShapeDtypeStruct((rows, LANES), I32),
                   jax.ShapeDtypeStruct((SUBLANES, LANES), I32),
                   jax.ShapeDtypeStruct((SUBLANES, LANES), I32)],
        name="plan",
    )(emat, ux, ones, lx)
    init = n_asg + (jnp.arange(nb * R, dtype=I32) & (R - 1))
    inv = pl.pallas_call(
        _invert_kernel,
        in_specs=[pl.BlockSpec(memory_space=pltpu.SMEM), pl.BlockSpec(memory_space=pl.ANY)],
        out_specs=pl.BlockSpec(memory_space=pltpu.SMEM),
        out_shape=jax.ShapeDtypeStruct((nb * R,), I32),
        name="invert",
    )(dest.reshape(n_asg), init)
    return bstart[0, 0:N_EXPERTS], nblk[0, 0:N_EXPERTS], inv


def kernel(x, norm_mix, w_in, ssd_conv_w, ssd_conv_b, dt_bias, a_log, d_skip, ssd_norm, sc_conv_w,
           sc_norm, w_out, norm_ffn, w_router_group, w_router_expert, w_gate, w_up, w_down, final_norm):
    B, L, D = x.shape
    n_tok = B * L
    depth = norm_mix.shape[0]
    assert depth == 1 and D == D_MODEL and (n_tok & (n_tok - 1)) == 0
    nb = (2 * n_tok) // R_BLK + N_EXPERTS + 1

    o1 = SSD_WIDTH
    o2 = o1 + XBC
    o3 = o2 + N_HEADS
    wi = w_in[0]
    wbig = jnp.concatenate([wi[:, 0:o2], wi[:, o3:]], axis=1).astype(BF16)
    wdt = jnp.pad(wi[:, o2:o3], ((0, 0), (0, LANES - N_HEADS))).astype(BF16)
    pad_h = (0, LANES - N_HEADS)
    dtb = jnp.pad(dt_bias[0], pad_h).reshape(1, LANES)
    alog = jnp.pad(a_log[0], pad_h).reshape(1, LANES)
    dskip = jnp.repeat(d_skip[0], HEAD_DIM).reshape(1, SSD_WIDTH)

    wre = jnp.transpose(w_router_expert[0], (1, 0, 2)).reshape(D, N_EXPERTS)
    wr = jnp.pad(jnp.concatenate([w_router_group[0], wre], axis=1),
                 ((0, 0), (0, LANES - N_GROUPS - N_EXPERTS)))
    wrh = wr.astype(BF16)
    wrl = (wr - wrh.astype(F32)).astype(BF16)

    ri = jnp.arange(T_MIX)
    tri = (ri[:, None] >= ri[None, :]).astype(BF16)
    er = jnp.arange(LANES)
    ec = jnp.arange(SSD_WIDTH)
    e3 = ((er[:, None] < 48) & ((er[:, None] % 16) == (ec[None, :] // HEAD_DIM))).astype(BF16)
    gsum = ((ec[:, None] // HEAD_DIM) == er[None, :]).astype(BF16)

    h, hn2, e_rows, gates = _mixer_call(
        x, norm_mix[0].reshape(1, D), wbig, wdt, ssd_conv_w[0], ssd_conv_b[0].reshape(1, XBC), dtb, alog,
        dskip, ssd_norm[0].reshape(1, SSD_WIDTH), sc_conv_w[0], sc_norm[0].reshape(1, SC_WIDTH),
        w_out[0].astype(BF16), norm_ffn[0].reshape(1, D), wrh, wrl, tri, e3, gsum)

    bstart, nblk, inv = _plan(e_rows, n_tok, nb)
    y2 = _moe_call(bstart, nblk, inv, hn2.reshape(n_tok, ROW_TILES, LANES),
                   w_gate[0], w_up[0], w_down[0], n_tok)
    out = _combine_call(h.reshape(n_tok, D), y2.reshape(-1, LANES), gates, final_norm.reshape(1, D))
    return out.reshape(B, L, D)
```

```python
import functools

import jax
import jax.numpy as jnp
from jax import lax
from jax.experimental import pallas as pl
from jax.experimental.pallas import tpu as pltpu

F32 = jnp.float32
BF16 = jnp.bfloat16
I32 = jnp.int32

EPS = 1e-6
D_MODEL = 1024
N_HEADS = 16
HEAD_DIM = 64
N_BC_GROUPS = 2
STATE = 128
SSD_WIDTH = 1024
XBC = SSD_WIDTH + 2 * N_BC_GROUPS * STATE
SC_WIDTH = 1024
SC_GROUPS = 16
N_GROUPS = 4
EPG = 8
N_EXPERTS = 32
D_FF = 512

LANES = 128
SUBLANES = 8
ROW_TILES = D_MODEL // LANES

T_MIX = 256
PAIR_ROWS = T_MIX // 2
PAIR_SHIFT = 7
R_BLK = 256
R_SHIFT = 8
R_HALF = R_BLK // 2
T_CMB = 512

VMEM_LIMIT = 56 * 1024 * 1024


def _dot(a, b):
    return jnp.dot(a, b, preferred_element_type=F32)


def _split3(v):
    p1 = v.astype(BF16).astype(F32)
    r1 = v - p1
    p2 = r1.astype(BF16).astype(F32)
    p3 = (r1 - p2).astype(BF16).astype(F32)
    return p1, p2, p3


def _pack3(v):
    p1, p2, p3 = _split3(v)
    return (p1 + pltpu.roll(p2, 16, 1) + pltpu.roll(p3, 32, 1)).astype(BF16)


def _silu(v):
    return v * jax.nn.sigmoid(v)


def _mixer_kernel(x_ref, gmix_ref, wbig_ref, wdt_ref, convw_ref, convb_ref, dtb_ref, alog_ref,
                  dskip_ref, ssdn_ref, scw_ref, scn_ref, wout_ref, gffn_ref, wrh_ref, wrl_ref,
                  tri_ref, e3_ref, gsum_ref,
                  h_ref, hn2_ref, e_ref, g_ref,
                  cbuf, sbuf, st_ref, pbuf):
    T = T_MIX
    t = pl.program_id(1)

    @pl.when(t == 0)
    def _():
        cbuf[0:8, :] = jnp.zeros((8, XBC), F32)
        sbuf[0:8, :] = jnp.zeros((8, SC_WIDTH), F32)
        st_ref[...] = jnp.zeros(st_ref.shape, F32)

    @pl.when(t > 0)
    def _():
        cbuf[0:8, :] = cbuf[T:T + 8, :]
        sbuf[0:8, :] = sbuf[T:T + 8, :]

    x = x_ref[0]
    ms = jnp.mean(x * x, axis=-1, keepdims=True)
    hn = (x * lax.rsqrt(ms + EPS) * gmix_ref[...]).astype(BF16)

    z = _dot(hn, wbig_ref[:, 0:1024])
    xbc = _dot(hn, wbig_ref[:, 1024:2560])
    scb = _dot(hn, wbig_ref[:, 2560:3584])
    scc = _dot(hn, wbig_ref[:, 3584:4608])
    scv = _dot(hn, wbig_ref[:, 4608:5632])
    dt_raw = _dot(hn, wdt_ref[...])

    cbuf[8:8 + T, :] = xbc
    cw = convw_ref[...]
    acc = convb_ref[...] + cw[3:4, :] * xbc
    for k in range(3):
        acc = acc + cw[k:k + 1, :] * cbuf[5 + k:5 + k + T, :]
    xact = _silu(acc)
    xs = xact[:, 0:SSD_WIDTH]

    lane = lax.broadcasted_iota(I32, (1, LANES), 1)
    hmask = lane < N_HEADS
    a = jnp.where(hmask, -jnp.exp(alog_ref[...]), 0.0)
    dtv = dt_raw + dtb_ref[...]
    dt = jnp.where(hmask, jnp.maximum(dtv, 0.0) + jnp.log1p(jnp.exp(-jnp.abs(dtv))), 0.0)
    adt = dt * a
    c3 = _dot(tri_ref[...], _pack3(adt))
    ac = jnp.where(hmask, c3 + pltpu.roll(c3, LANES - 16, 1) + pltpu.roll(c3, LANES - 32, 1), 0.0)
    ac_last = ac[T - 1:T, :]
    eac = jnp.where(hmask, jnp.exp(ac), 0.0)
    wdt = dt * jnp.exp(ac_last - ac)

    stacked = jnp.concatenate([_pack3(dt), _pack3(wdt), _pack3(eac)], axis=0)
    ex = _dot(stacked, e3_ref[...])
    dt_e = ex[0:T]
    wdt_e = ex[T:2 * T]
    eac_e = ex[2 * T:3 * T]
    xdt = (xs * dt_e).astype(BF16)
    xdtw = (xs * wdt_e).astype(BF16)

    ac_rows = ac.T
    rr = lax.broadcasted_iota(I32, (T, T), 0)
    cc = lax.broadcasted_iota(I32, (T, T), 1)
    causal = rr >= cc
    cblk = lax.shift_right_logical(lax.broadcasted_iota(I32, (T, 4 * HEAD_DIM), 1), 6)

    y_parts = []
    for g in range(N_BC_GROUPS):
        bg = xact[:, SSD_WIDTH + STATE * g:SSD_WIDTH + STATE * (g + 1)]
        cg = xact[:, SSD_WIDTH + 2 * STATE + STATE * g:SSD_WIDTH + 2 * STATE + STATE * (g + 1)]
        bb = bg.astype(BF16)
        cbf = cg.astype(BF16)
        cb = lax.dot_general(cbf, bb, (((1,), (1,)), ((), ())), preferred_element_type=F32)
        st = st_ref[g]
        yoff = _dot(cbf, st.astype(BF16))
        dec = eac_e[T - 1:T, 512 * g:512 * (g + 1)]
        bgt = bg.T.astype(BF16)
        st_ref[g] = st * dec + _dot(bgt, xdtw[:, 512 * g:512 * (g + 1)])
        for q in range(2):
            ms_list = []
            for r in range(4):
                hh = 8 * g + 4 * q + r
                seg = ac[:, hh:hh + 1] - ac_rows[hh:hh + 1, :]
                lh = jnp.exp(jnp.where(causal, seg, -jnp.inf))
                ms_list.append((cb * lh).astype(BF16))
            lhs = jnp.concatenate(ms_list, axis=1)
            lo = 512 * g + 256 * q
            x4 = xdt[:, lo:lo + 256]
            rhs = jnp.concatenate(
                [jnp.where(cblk == r, x4, jnp.zeros_like(x4)) for r in range(4)], axis=0)
            yd = _dot(lhs, rhs)
            y_parts.append(yd + eac_e[:, lo:lo + 256] * yoff[:, 256 * q:256 * (q + 1)])
    y = jnp.concatenate(y_parts, axis=1) + dskip_ref[...] * xs

    v = y * _silu(z)
    outs = []
    for g in range(N_BC_GROUPS):
        vg = v[:, 512 * g:512 * (g + 1)]
        msg = jnp.mean(vg * vg, axis=-1, keepdims=True)
        outs.append(vg * lax.rsqrt(msg + EPS))
    ssd_out = (jnp.concatenate(outs, axis=1) * ssdn_ref[...]).astype(BF16)

    u = scc * scv
    sbuf[8:8 + T, :] = u
    sw = scw_ref[...]
    conv = sw[2:3, :] * u + sw[1:2, :] * sbuf[7:7 + T, :] + sw[0:1, :] * sbuf[6:6 + T, :]
    sc = scb * conv
    gs = _dot((sc * sc).astype(BF16), gsum_ref[...])
    rstd = jnp.where(hmask, lax.rsqrt(gs * (1.0 / HEAD_DIM) + EPS), 0.0)
    rstd_e = _dot(_pack3(rstd), e3_ref[...])
    sc_out = (sc * rstd_e * scn_ref[...]).astype(BF16)

    mix = _dot(ssd_out, wout_ref[0:1024, :]) + _dot(sc_out, wout_ref[1024:2048, :])
    h = x + mix
    h_ref[0] = h

    ms2 = jnp.mean(h * h, axis=-1, keepdims=True)
    hn2 = h * lax.rsqrt(ms2 + EPS) * gffn_ref[...]
    hi = hn2.astype(BF16)
    bits = pltpu.bitcast(hi.astype(F32), jnp.uint32)
    for s in range(ROW_TILES):
        pbuf[pl.ds(s * T, T), :] = bits[:, LANES * s:LANES * (s + 1)]
    for s in range(ROW_TILES):
        even = pbuf[pl.ds(s * T, PAIR_ROWS, stride=2), :]
        odd = pbuf[pl.ds(s * T + 1, PAIR_ROWS, stride=2), :]
        hn2_ref[pl.ds(s, PAIR_ROWS, stride=ROW_TILES), :] = jnp.bitwise_or(
            jnp.right_shift(even, jnp.uint32(16)), odd)
    lo_ = (hn2 - hi.astype(F32)).astype(BF16)
    logits = _dot(hi, wrh_ref[...]) + (_dot(lo_, wrh_ref[...]) + _dot(hi, wrl_ref[...]))

    lanef = lax.broadcasted_iota(I32, (T, LANES), 1)
    lane_f = lanef.astype(F32)
    neg = -jnp.inf
    big = 1e9
    gl = jnp.where(lanef < N_GROUPS, logits, neg)
    gmax = jnp.max(gl, axis=-1, keepdims=True)
    gidx = jnp.min(jnp.where(gl == gmax, lane_f, big), axis=-1, keepdims=True)
    gsum = jnp.sum(jnp.where(lanef < N_GROUPS, jnp.exp(logits - gmax), 0.0), axis=-1, keepdims=True)
    gw = 1.0 / gsum
    egrp = lax.shift_right_logical(lanef - N_GROUPS, 3).astype(F32)
    in_grp = (lanef >= N_GROUPS) & (lanef < N_GROUPS + N_EXPERTS) & (egrp == gidx)
    el = jnp.where(in_grp, logits, neg)
    v1 = jnp.max(el, axis=-1, keepdims=True)
    i1 = jnp.min(jnp.where(el == v1, lane_f, big), axis=-1, keepdims=True)
    el2 = jnp.where(lane_f == i1, neg, el)
    v2 = jnp.max(el2, axis=-1, keepdims=True)
    i2 = jnp.min(jnp.where(el2 == v2, lane_f, big), axis=-1, keepdims=True)
    p = jnp.exp(v2 - v1)
    s1 = 1.0 / (1.0 + p)
    gate1 = gw * s1
    gate2 = gw * (p * s1)
    g_ref[...] = jnp.where(lanef == 0, gate1, jnp.where(lanef == 1, gate2, 0.0))
    emat = jnp.where(lanef == 0, i1 - N_GROUPS, jnp.where(lanef == 1, i2 - N_GROUPS, 0.0))
    e_ref[...] = emat.T[0:8, :].astype(I32)


def _mixer_call(x, gmix, wbig, wdt, convw, convb, dtb, alog, dskip, ssdn, scw, scn, wout, gffn,
                wrh, wrl, tri, e3, gsum):
    B, L, D = x.shape
    T = T_MIX
    nt = L // T
    n_tok = B * L

    def const(shape):
        return pl.BlockSpec(shape, lambda b, t: (0,) * len(shape))

    in_specs = [
        pl.BlockSpec((1, T, D), lambda b, t: (b, t, 0)),
        const((1, D)),
        const(wbig.shape), const(wdt.shape), const(convw.shape), const(convb.shape),
        const(dtb.shape), const(alog.shape), const(dskip.shape), const(ssdn.shape),
        const(scw.shape), const(scn.shape), const(wout.shape), const(gffn.shape),
        const(wrh.shape), const(wrl.shape), const(tri.shape), const(e3.shape), const(gsum.shape),
    ]
    out_shape = [
        jax.ShapeDtypeStruct((B, L, D), F32),
        jax.ShapeDtypeStruct((n_tok // 2 * ROW_TILES, LANES), jnp.uint32),
        jax.ShapeDtypeStruct((8, n_tok), I32),
        jax.ShapeDtypeStruct((n_tok, LANES), F32),
    ]
    out_specs = [
        pl.BlockSpec((1, T, D), lambda b, t: (b, t, 0)),
        pl.BlockSpec((PAIR_ROWS * ROW_TILES, LANES), lambda b, t: (b * nt + t, 0)),
        pl.BlockSpec((8, T), lambda b, t: (0, b * nt + t)),
        pl.BlockSpec((T, LANES), lambda b, t: (b * nt + t, 0)),
    ]
    return pl.pallas_call(
        _mixer_kernel,
        grid=(B, nt),
        in_specs=in_specs,
        out_specs=out_specs,
        out_shape=out_shape,
        scratch_shapes=[
            pltpu.VMEM((T + 8, XBC), F32),
            pltpu.VMEM((T + 8, SC_WIDTH), F32),
            pltpu.VMEM((N_BC_GROUPS, STATE, 512), F32),
            pltpu.VMEM((ROW_TILES * T, LANES), jnp.uint32),
        ],
        compiler_params=pltpu.CompilerParams(
            dimension_semantics=("arbitrary", "arbitrary"),
            vmem_limit_bytes=VMEM_LIMIT),
        name="mixer",
    )(x, gmix, wbig, wdt, convw, convb, dtb, alog, dskip, ssdn, scw, scn, wout, gffn, wrh, wrl,
      tri, e3, gsum)


def _moe_kernel(bstart_ref, nblk_ref, pk_ref, hn2p_ref, wg_ref, wu_ref, wd_ref,
                y2_ref, hn2v, xbuf, ybuf, wgb, wub, wdb, idx_ref, lsem, ssem, isem, *, n_tok):
    R = R_BLK
    H = R_HALF
    slab = R * ROW_TILES
    e = pl.program_id(0)
    g0 = bstart_ref[e]
    nb_e = nblk_ref[e]

    def idx_word(sl, field, j):
        return idx_ref[sl, 2 * field + j // LANES, j % LANES]

    def idx_copy(g, sl):
        return pltpu.make_async_copy(pk_ref.at[g], idx_ref.at[sl], isem.at[sl])

    def gather_rows(sl, lo, hi):
        for j in range(lo, hi):
            row = pl.multiple_of(idx_word(sl, 0, j), ROW_TILES)
            sh = idx_word(sl, 1, j).astype(jnp.uint32)
            w = hn2v[pl.ds(row, ROW_TILES), :]
            w = jnp.left_shift(jnp.right_shift(w, sh), jnp.uint32(16))
            xbuf[pl.ds(j * ROW_TILES, ROW_TILES), :] = pltpu.bitcast(w, F32)

    def scatter_start(sl, lo, hi):
        for j in range(lo, hi):
            pltpu.make_async_copy(
                ybuf.at[pl.ds(sl * slab + j * ROW_TILES, ROW_TILES), :],
                y2_ref.at[idx_word(sl, 2, j)],
                ssem.at[sl]).start(priority=j % 2)

    def slab_wait(buf, sem, sl):
        view = buf.at[pl.ds(sl * slab, slab), :]
        pltpu.make_async_copy(view, view, sem.at[sl]).wait()

    @pl.when(e == 0)
    def _():
        load = pltpu.make_async_copy(hn2p_ref, hn2v, lsem.at[0])
        load.start()
        idx_copy(0, 0).start()
        ybuf[...] = jnp.zeros(ybuf.shape, F32)
        for sl in range(2):
            for j in range(R):
                pltpu.make_async_copy(
                    ybuf.at[pl.ds(sl * slab + j * ROW_TILES, ROW_TILES), :],
                    y2_ref.at[2 * n_tok + sl * R + j],
                    ssem.at[sl]).start(priority=j % 2)
        load.wait()

    @pl.when(nb_e > 0)
    def _():
        wgb[...] = wg_ref[0].astype(BF16)
        wub[...] = wu_ref[0].astype(BF16)
        wdb[...] = wd_ref[0].astype(BF16)

    def run_block(g, sl):
        idx_copy(g, sl).wait()
        idx_copy(g + 1, 1 - sl).start()
        slab_wait(ybuf, ssem, sl)
        for hf in range(2):
            gather_rows(sl, hf * H, (hf + 1) * H)
            xh = jnp.concatenate(
                [xbuf[pl.ds(hf * H * ROW_TILES + s, H, stride=ROW_TILES), :]
                 for s in range(ROW_TILES)], axis=1).astype(BF16)
            gg = _dot(xh, wgb[...])
            uu = _dot(xh, wub[...])
            act = (_silu(gg) * uu).astype(BF16)
            yy = _dot(act, wdb[...])
            for s in range(ROW_TILES):
                ybuf[pl.ds(sl * slab + hf * H * ROW_TILES + s, H, stride=ROW_TILES), :] = (
                    yy[:, LANES * s:LANES * (s + 1)])
            scatter_start(sl, hf * H, (hf + 1) * H)

    def block(i, carry):
        g = g0 + i
        par = jnp.bitwise_and(g, 1)

        @pl.when(par == 0)
        def _():
            run_block(g, 0)

        @pl.when(par == 1)
        def _():
            run_block(g, 1)

        return carry

    lax.fori_loop(0, nb_e, block, 0)

    @pl.when(e == pl.num_programs(0) - 1)
    def _():
        g_end = g0 + nb_e

        @pl.when(jnp.bitwise_and(g_end, 1) == 0)
        def _():
            idx_copy(g_end, 0).wait()

        @pl.when(jnp.bitwise_and(g_end, 1) == 1)
        def _():
            idx_copy(g_end, 1).wait()

        slab_wait(ybuf, ssem, 0)
        slab_wait(ybuf, ssem, 1)


def _moe_call(bstart, nblk, pk, hn2p, wg, wu, wd, n_tok):
    R = R_BLK
    slab = R * ROW_TILES
    grid_spec = pltpu.PrefetchScalarGridSpec(
        num_scalar_prefetch=2,
        grid=(N_EXPERTS,),
        in_specs=[
            pl.BlockSpec(memory_space=pl.ANY),
            pl.BlockSpec(memory_space=pl.ANY),
            pl.BlockSpec((1, D_MODEL, D_FF), lambda e, bs, nb: (e, 0, 0)),
            pl.BlockSpec((1, D_MODEL, D_FF), lambda e, bs, nb: (e, 0, 0)),
            pl.BlockSpec((1, D_FF, D_MODEL), lambda e, bs, nb: (e, 0, 0)),
        ],
        out_specs=pl.BlockSpec(memory_space=pl.ANY),
        scratch_shapes=[
            pltpu.VMEM(hn2p.shape, jnp.uint32),
            pltpu.VMEM((slab, LANES), F32),
            pltpu.VMEM((2 * slab, LANES), F32),
            pltpu.VMEM((D_MODEL, D_FF), BF16),
            pltpu.VMEM((D_MODEL, D_FF), BF16),
            pltpu.VMEM((D_FF, D_MODEL), BF16),
            pltpu.SMEM((2, SUBLANES, LANES), I32),
            pltpu.SemaphoreType.DMA((1,)),
            pltpu.SemaphoreType.DMA((2,)),
            pltpu.SemaphoreType.DMA((2,)),
        ],
    )
    return pl.pallas_call(
        functools.partial(_moe_kernel, n_tok=n_tok),
        grid_spec=grid_spec,
        out_shape=jax.ShapeDtypeStruct((2 * n_tok + 2 * R, ROW_TILES, LANES), F32),
        compiler_params=pltpu.CompilerParams(
            dimension_semantics=("arbitrary",),
            vmem_limit_bytes=VMEM_LIMIT),
        name="moe",
    )(bstart, nblk, pk, hn2p, wg, wu, wd)


def _combine_kernel(h_ref, y0_ref, y1_ref, g_ref, fn_ref, o_ref):
    T = T_CMB
    h = h_ref[...]
    y0 = jnp.concatenate([y0_ref[pl.ds(s, T, stride=ROW_TILES), :] for s in range(ROW_TILES)], axis=1)
    y1 = jnp.concatenate([y1_ref[pl.ds(s, T, stride=ROW_TILES), :] for s in range(ROW_TILES)], axis=1)
    g = g_ref[...]
    v = h + (y0 * g[:, 0:1] + y1 * g[:, 1:2])
    ms = jnp.mean(v * v, axis=-1, keepdims=True)
    o_ref[...] = v * lax.rsqrt(ms + EPS) * fn_ref[...]


def _combine_call(h2d, y2, gates, fnorm):
    n_tok, D = h2d.shape
    T = T_CMB
    nt = n_tok // T
    return pl.pallas_call(
        _combine_kernel,
        grid=(nt,),
        in_specs=[
            pl.BlockSpec((T, D), lambda i: (i, 0)),
            pl.BlockSpec((T * ROW_TILES, LANES), lambda i: (i, 0)),
            pl.BlockSpec((T * ROW_TILES, LANES), lambda i: (nt + i, 0)),
            pl.BlockSpec((T, LANES), lambda i: (i, 0)),
            pl.BlockSpec((1, D), lambda i: (0, 0)),
        ],
        out_specs=pl.BlockSpec((T, D), lambda i: (i, 0)),
        out_shape=jax.ShapeDtypeStruct((n_tok, D), F32),
        compiler_params=pltpu.CompilerParams(dimension_semantics=("arbitrary",)),
        name="combine",
    )(h2d, y2, y2, gates, fnorm)


def _plan_kernel(e_ref, ux_ref, ones_ref, lx_ref, dest_ref, bstart_ref, nblk_ref):
    rows = e_ref.shape[0]
    ev = e_ref[...]
    lane8 = lax.broadcasted_iota(I32, (SUBLANES, LANES), 1)
    dest = jnp.zeros((rows, LANES), F32)
    bstart = jnp.zeros((SUBLANES, LANES), I32)
    nblk = jnp.zeros((SUBLANES, LANES), I32)
    pstart = jnp.zeros((1, LANES), F32)
    for e in range(N_EXPERTS):
        ohb = ev == e
        oh = jnp.where(ohb, 1.0, 0.0).astype(BF16)
        within = _dot(oh, ux_ref[...])
        rtot = _dot(oh, ones_ref[...])
        rpre = _dot(lx_ref[...], rtot.astype(BF16))
        cnt = rpre[rows - 1:rows, :] + rtot[rows - 1:rows, :]
        dest = dest + jnp.where(ohb, within + rpre + pstart, 0.0)
        cnt_i = cnt.astype(I32)
        nb_e = lax.shift_right_logical(cnt_i + (R_BLK - 1), R_SHIFT)
        bstart = jnp.where(lane8 == e, lax.shift_right_logical(pstart.astype(I32), R_SHIFT), bstart)
        nblk = jnp.where(lane8 == e, nb_e, nblk)
        pstart = pstart + lax.shift_left(nb_e, R_SHIFT).astype(F32)
    dest_ref[...] = dest.astype(I32)
    bstart_ref[...] = bstart
    nblk_ref[...] = nblk


def _invert_kernel(dest_ref, init_ref, inv_ref):
    n_asg = dest_ref.shape[0]
    pltpu.sync_copy(init_ref, inv_ref)

    def put(a, c):
        inv_ref[dest_ref[a]] = a
        return c

    lax.fori_loop(0, n_asg, put, 0, unroll=16)


def _index_tiles_kernel(inv_ref, pk_ref, *, n_tok):
    nb = pk_ref.shape[0] // SUBLANES
    per_blk = R_BLK // LANES
    assert per_blk == 2
    pk_ref[...] = jnp.zeros(pk_ref.shape, I32)
    for half in range(per_blk):
        a = inv_ref[pl.ds(half, nb, stride=per_blk), :]
        t = jnp.bitwise_and(a, n_tok - 1)
        pk_ref[pl.ds(half, nb, stride=SUBLANES), :] = lax.shift_left(lax.shift_right_logical(t, 1), 3)
        pk_ref[pl.ds(2 + half, nb, stride=SUBLANES), :] = lax.shift_left(jnp.bitwise_and(a, 1), 4)
        pk_ref[pl.ds(4 + half, nb, stride=SUBLANES), :] = a


def _plan(e_rows, n_tok, nb):
    R = R_BLK
    n_asg = 2 * n_tok
    rows = n_asg // LANES
    emat = e_rows[0:2].reshape(rows, LANES)
    li = jnp.arange(LANES)
    ux = (li[:, None] < li[None, :]).astype(BF16)
    ones = jnp.ones((LANES, LANES), BF16)
    ri = jnp.arange(rows)
    lx = (ri[:, None] > ri[None, :]).astype(BF16)
    dest, bstart, nblk = pl.pallas_call(
        _plan_kernel,
        out_shape=[jax.ShapeDtypeStruct((rows, LANES), I32),
                   jax.ShapeDtypeStruct((SUBLANES, LANES), I32),
                   jax.ShapeDtypeStruct((SUBLANES, LANES), I32)],
        name="plan",
    )(emat, ux, ones, lx)
    init = n_asg + (jnp.arange(nb * R, dtype=I32) & (R - 1))
    inv = pl.pallas_call(
        _invert_kernel,
        in_specs=[pl.BlockSpec(memory_space=pltpu.SMEM), pl.BlockSpec(memory_space=pl.ANY)],
        out_specs=pl.BlockSpec(memory_space=pltpu.SMEM),
        out_shape=jax.ShapeDtypeStruct((nb * R,), I32),
        name="invert",
    )(dest.reshape(n_asg), init)
    pk = pl.pallas_call(
        functools.partial(_index_tiles_kernel, n_tok=n_tok),
        out_shape=jax.ShapeDtypeStruct((nb * SUBLANES, LANES), I32),
        name="index_tiles",
    )(inv.reshape(nb * R // LANES, LANES))
    return bstart[0, 0:N_EXPERTS], nblk[0, 0:N_EXPERTS], pk.reshape(nb, SUBLANES, LANES)


def kernel(x, norm_mix, w_in, ssd_conv_w, ssd_conv_b, dt_bias, a_log, d_skip, ssd_norm, sc_conv_w,
           sc_norm, w_out, norm_ffn, w_router_group, w_router_expert, w_gate, w_up, w_down, final_norm):
    B, L, D = x.shape
    n_tok = B * L
    depth = norm_mix.shape[0]
    assert depth == 1 and D == D_MODEL and (n_tok & (n_tok - 1)) == 0
    nb = -(-((2 * n_tok) // R_BLK + N_EXPERTS + 1) // SUBLANES) * SUBLANES

    o1 = SSD_WIDTH
    o2 = o1 + XBC
    o3 = o2 + N_HEADS
    wi = w_in[0]
    wbig = jnp.concatenate([wi[:, 0:o2], wi[:, o3:]], axis=1).astype(BF16)
    wdt = jnp.pad(wi[:, o2:o3], ((0, 0), (0, LANES - N_HEADS))).astype(BF16)
    pad_h = (0, LANES - N_HEADS)
    dtb = jnp.pad(dt_bias[0], pad_h).reshape(1, LANES)
    alog = jnp.pad(a_log[0], pad_h).reshape(1, LANES)
    dskip = jnp.repeat(d_skip[0], HEAD_DIM).reshape(1, SSD_WIDTH)

    wre = jnp.transpose(w_router_expert[0], (1, 0, 2)).reshape(D, N_EXPERTS)
    wr = jnp.pad(jnp.concatenate([w_router_group[0], wre], axis=1),
                 ((0, 0), (0, LANES - N_GROUPS - N_EXPERTS)))
    wrh = wr.astype(BF16)
    wrl = (wr - wrh.astype(F32)).astype(BF16)

    ri = jnp.arange(T_MIX)
    tri = (ri[:, None] >= ri[None, :]).astype(BF16)
    er = jnp.arange(LANES)
    ec = jnp.arange(SSD_WIDTH)
    e3 = ((er[:, None] < 48) & ((er[:, None] % 16) == (ec[None, :] // HEAD_DIM))).astype(BF16)
    gsum = ((ec[:, None] // HEAD_DIM) == er[None, :]).astype(BF16)

    h, hn2, e_rows, gates = _mixer_call(
        x, norm_mix[0].reshape(1, D), wbig, wdt, ssd_conv_w[0], ssd_conv_b[0].reshape(1, XBC), dtb, alog,
        dskip, ssd_norm[0].reshape(1, SSD_WIDTH), sc_conv_w[0], sc_norm[0].reshape(1, SC_WIDTH),
        w_out[0].astype(BF16), norm_ffn[0].reshape(1, D), wrh, wrl, tri, e3, gsum)

    bstart, nblk, pk = _plan(e_rows, n_tok, nb)
    y2 = _moe_call(bstart, nblk, pk, hn2, w_gate[0], w_up[0], w_down[0], n_tok)
    out = _combine_call(h.reshape(n_tok, D), y2.reshape(-1, LANES), gates, final_norm.reshape(1, D))
    return out.reshape(B, L, D)
```

```python
import functools

import jax
import jax.numpy as jnp
from jax import lax
from jax.experimental import pallas as pl
from jax.experimental.pallas import tpu as pltpu

F32 = jnp.float32
BF16 = jnp.bfloat16
I32 = jnp.int32

EPS = 1e-6
D_MODEL = 1024
N_HEADS = 16
HEAD_DIM = 64
N_BC_GROUPS = 2
STATE = 128
SSD_WIDTH = 1024
XBC = SSD_WIDTH + 2 * N_BC_GROUPS * STATE
SC_WIDTH = 1024
SC_GROUPS = 16
N_GROUPS = 4
EPG = 8
N_EXPERTS = 32
D_FF = 512

LANES = 128
SUBLANES = 8
ROW_TILES = D_MODEL // LANES

T_MIX = 256
PAIR_ROWS = T_MIX // 2
PAIR_SHIFT = 7
R_BLK = 256
R_SHIFT = 8
R_HALF = R_BLK // 2
T_CMB = 512

VMEM_LIMIT = 56 * 1024 * 1024


def _dot(a, b):
    return jnp.dot(a, b, preferred_element_type=F32)


def _split3(v):
    p1 = v.astype(BF16).astype(F32)
    r1 = v - p1
    p2 = r1.astype(BF16).astype(F32)
    p3 = (r1 - p2).astype(BF16).astype(F32)
    return p1, p2, p3


def _pack3(v):
    p1, p2, p3 = _split3(v)
    return (p1 + pltpu.roll(p2, 16, 1) + pltpu.roll(p3, 32, 1)).astype(BF16)


def _silu(v):
    return v * jax.nn.sigmoid(v)


def _mixer_kernel(x_ref, gmix_ref, wa_ref, wb_ref, wdt_ref, convw_ref, convb_ref, dtb_ref, alog_ref,
                  dskip_ref, ssdn_ref, scw_ref, scn_ref, wout_ref, gffn_ref, wrh_ref, wrl_ref,
                  tri_ref, e3_ref, gsum_ref,
                  h_ref, hn2_ref, e_ref, g_ref,
                  cbuf, sbuf, st_ref, pbuf):
    T = T_MIX
    t = pl.program_id(1)

    @pl.when(t == 0)
    def _():
        cbuf[0:8, :] = jnp.zeros((8, XBC), F32)
        sbuf[0:8, :] = jnp.zeros((8, SC_WIDTH), F32)
        st_ref[...] = jnp.zeros(st_ref.shape, F32)

    @pl.when(t > 0)
    def _():
        cbuf[0:8, :] = cbuf[T:T + 8, :]
        sbuf[0:8, :] = sbuf[T:T + 8, :]

    x = x_ref[0]
    ms = jnp.mean(x * x, axis=-1, keepdims=True)
    hn = (x * lax.rsqrt(ms + EPS) * gmix_ref[...]).astype(BF16)

    z = _dot(hn, wa_ref[:, 0:1024])
    xbc = _dot(hn, wa_ref[:, 1024:2560])
    scb = _dot(hn, wb_ref[:, 0:1024])
    scc = _dot(hn, wb_ref[:, 1024:2048])
    scv = _dot(hn, wb_ref[:, 2048:3072])
    dt_raw = _dot(hn, wdt_ref[...])

    cbuf[8:8 + T, :] = xbc
    cw = convw_ref[...]
    acc = convb_ref[...] + cw[3:4, :] * xbc
    for k in range(3):
        acc = acc + cw[k:k + 1, :] * cbuf[5 + k:5 + k + T, :]
    xact = _silu(acc)
    xs = xact[:, 0:SSD_WIDTH]

    lane = lax.broadcasted_iota(I32, (1, LANES), 1)
    hmask = lane < N_HEADS
    a = jnp.where(hmask, -jnp.exp(alog_ref[...]), 0.0)
    dtv = dt_raw + dtb_ref[...]
    dt = jnp.where(hmask, jnp.maximum(dtv, 0.0) + jnp.log1p(jnp.exp(-jnp.abs(dtv))), 0.0)
    adt = dt * a
    c3 = _dot(tri_ref[...], _pack3(adt))
    ac = jnp.where(hmask, c3 + pltpu.roll(c3, LANES - 16, 1) + pltpu.roll(c3, LANES - 32, 1), 0.0)
    ac_last = ac[T - 1:T, :]
    eac = jnp.where(hmask, jnp.exp(ac), 0.0)
    wdt = dt * jnp.exp(ac_last - ac)

    stacked = jnp.concatenate([_pack3(dt), _pack3(wdt), _pack3(eac)], axis=0)
    ex = _dot(stacked, e3_ref[...])
    dt_e = ex[0:T]
    wdt_e = ex[T:2 * T]
    eac_e = ex[2 * T:3 * T]
    xdt = (xs * dt_e).astype(BF16)
    xdtw = (xs * wdt_e).astype(BF16)

    ac_rows = ac.T
    rr = lax.broadcasted_iota(I32, (T, T), 0)
    cc = lax.broadcasted_iota(I32, (T, T), 1)
    causal = rr >= cc
    cblk = lax.shift_right_logical(lax.broadcasted_iota(I32, (T, 4 * HEAD_DIM), 1), 6)

    y_parts = []
    for g in range(N_BC_GROUPS):
        bg = xact[:, SSD_WIDTH + STATE * g:SSD_WIDTH + STATE * (g + 1)]
        cg = xact[:, SSD_WIDTH + 2 * STATE + STATE * g:SSD_WIDTH + 2 * STATE + STATE * (g + 1)]
        bb = bg.astype(BF16)
        cbf = cg.astype(BF16)
        cb = lax.dot_general(cbf, bb, (((1,), (1,)), ((), ())), preferred_element_type=F32)
        st = st_ref[g]
        yoff = _dot(cbf, st.astype(BF16))
        dec = eac_e[T - 1:T, 512 * g:512 * (g + 1)]
        bgt = bg.T.astype(BF16)
        st_ref[g] = st * dec + _dot(bgt, xdtw[:, 512 * g:512 * (g + 1)])
        for q in range(2):
            ms_list = []
            for r in range(4):
                hh = 8 * g + 4 * q + r
                seg = ac[:, hh:hh + 1] - ac_rows[hh:hh + 1, :]
                lh = jnp.exp(jnp.where(causal, seg, -jnp.inf))
                ms_list.append((cb * lh).astype(BF16))
            lhs = jnp.concatenate(ms_list, axis=1)
            lo = 512 * g + 256 * q
            x4 = xdt[:, lo:lo + 256]
            rhs = jnp.concatenate(
                [jnp.where(cblk == r, x4, jnp.zeros_like(x4)) for r in range(4)], axis=0)
            yd = _dot(lhs, rhs)
            y_parts.append(yd + eac_e[:, lo:lo + 256] * yoff[:, 256 * q:256 * (q + 1)])
    y = jnp.concatenate(y_parts, axis=1) + dskip_ref[...] * xs

    v = y * _silu(z)
    outs = []
    for g in range(N_BC_GROUPS):
        vg = v[:, 512 * g:512 * (g + 1)]
        msg = jnp.mean(vg * vg, axis=-1, keepdims=True)
        outs.append(vg * lax.rsqrt(msg + EPS))
    ssd_out = (jnp.concatenate(outs, axis=1) * ssdn_ref[...]).astype(BF16)

    u = scc * scv
    sbuf[8:8 + T, :] = u
    sw = scw_ref[...]
    conv = sw[2:3, :] * u + sw[1:2, :] * sbuf[7:7 + T, :] + sw[0:1, :] * sbuf[6:6 + T, :]
    sc = scb * conv
    gs = _dot((sc * sc).astype(BF16), gsum_ref[...])
    rstd = jnp.where(hmask, lax.rsqrt(gs * (1.0 / HEAD_DIM) + EPS), 0.0)
    rstd_e = _dot(_pack3(rstd), e3_ref[...])
    sc_out = (sc * rstd_e * scn_ref[...]).astype(BF16)

    mix = _dot(ssd_out, wout_ref[0:1024, :]) + _dot(sc_out, wout_ref[1024:2048, :])
    h = x + mix
    h_ref[0] = h

    ms2 = jnp.mean(h * h, axis=-1, keepdims=True)
    hn2 = h * lax.rsqrt(ms2 + EPS) * gffn_ref[...]
    hi = hn2.astype(BF16)
    bits = pltpu.bitcast(hi.astype(F32), jnp.uint32)
    for s in range(ROW_TILES):
        pbuf[pl.ds(s * T, T), :] = bits[:, LANES * s:LANES * (s + 1)]
    for s in range(ROW_TILES):
        even = pbuf[pl.ds(s * T, PAIR_ROWS, stride=2), :]
        odd = pbuf[pl.ds(s * T + 1, PAIR_ROWS, stride=2), :]
        hn2_ref[pl.ds(s, PAIR_ROWS, stride=ROW_TILES), :] = jnp.bitwise_or(
            jnp.right_shift(even, jnp.uint32(16)), odd)
    lo_ = (hn2 - hi.astype(F32)).astype(BF16)
    logits = _dot(hi, wrh_ref[...]) + (_dot(lo_, wrh_ref[...]) + _dot(hi, wrl_ref[...]))

    lanef = lax.broadcasted_iota(I32, (T, LANES), 1)
    lane_f = lanef.astype(F32)
    neg = -jnp.inf
    big = 1e9
    gl = jnp.where(lanef < N_GROUPS, logits, neg)
    gmax = jnp.max(gl, axis=-1, keepdims=True)
    gidx = jnp.min(jnp.where(gl == gmax, lane_f, big), axis=-1, keepdims=True)
    gsum = jnp.sum(jnp.where(lanef < N_GROUPS, jnp.exp(logits - gmax), 0.0), axis=-1, keepdims=True)
    gw = 1.0 / gsum
    egrp = lax.shift_right_logical(lanef - N_GROUPS, 3).astype(F32)
    in_grp = (lanef >= N_GROUPS) & (lanef < N_GROUPS + N_EXPERTS) & (egrp == gidx)
    el = jnp.where(in_grp, logits, neg)
    v1 = jnp.max(el, axis=-1, keepdims=True)
    i1 = jnp.min(jnp.where(el == v1, lane_f, big), axis=-1, keepdims=True)
    el2 = jnp.where(lane_f == i1, neg, el)
    v2 = jnp.max(el2, axis=-1, keepdims=True)
    i2 = jnp.min(jnp.where(el2 == v2, lane_f, big), axis=-1, keepdims=True)
    p = jnp.exp(v2 - v1)
    s1 = 1.0 / (1.0 + p)
    gate1 = gw * s1
    gate2 = gw * (p * s1)
    g_ref[...] = jnp.where(lanef == 0, gate1, jnp.where(lanef == 1, gate2, 0.0))
    emat = jnp.where(lanef == 0, i1 - N_GROUPS, jnp.where(lanef == 1, i2 - N_GROUPS, 0.0))
    e_ref[...] = emat.T[0:8, :].astype(I32)


def _mixer_call(x, gmix, wa, wb, wdt, convw, convb, dtb, alog, dskip, ssdn, scw, scn, wout, gffn,
                wrh, wrl, tri, e3, gsum):
    B, L, D = x.shape
    T = T_MIX
    nt = L // T
    n_tok = B * L

    def const(shape):
        return pl.BlockSpec(shape, lambda b, t: (0,) * len(shape), pipeline_mode=pl.Buffered(1))

    in_specs = [
        pl.BlockSpec((1, T, D), lambda b, t: (b, t, 0)),
        const((1, D)),
        const(wa.shape), const(wb.shape), const(wdt.shape), const(convw.shape), const(convb.shape),
        const(dtb.shape), const(alog.shape), const(dskip.shape), const(ssdn.shape),
        const(scw.shape), const(scn.shape), const(wout.shape), const(gffn.shape),
        const(wrh.shape), const(wrl.shape), const(tri.shape), const(e3.shape), const(gsum.shape),
    ]
    out_shape = [
        jax.ShapeDtypeStruct((B, L, D), F32),
        jax.ShapeDtypeStruct((n_tok // 2 * ROW_TILES, LANES), jnp.uint32),
        jax.ShapeDtypeStruct((8, n_tok), I32),
        jax.ShapeDtypeStruct((n_tok, LANES), F32),
    ]
    out_specs = [
        pl.BlockSpec((1, T, D), lambda b, t: (b, t, 0)),
        pl.BlockSpec((PAIR_ROWS * ROW_TILES, LANES), lambda b, t: (b * nt + t, 0)),
        pl.BlockSpec((8, T), lambda b, t: (0, b * nt + t)),
        pl.BlockSpec((T, LANES), lambda b, t: (b * nt + t, 0)),
    ]
    return pl.pallas_call(
        _mixer_kernel,
        grid=(B, nt),
        in_specs=in_specs,
        out_specs=out_specs,
        out_shape=out_shape,
        scratch_shapes=[
            pltpu.VMEM((T + 8, XBC), F32),
            pltpu.VMEM((T + 8, SC_WIDTH), F32),
            pltpu.VMEM((N_BC_GROUPS, STATE, 512), F32),
            pltpu.VMEM((ROW_TILES * T, LANES), jnp.uint32),
        ],
        compiler_params=pltpu.CompilerParams(
            dimension_semantics=("arbitrary", "arbitrary"),
            vmem_limit_bytes=VMEM_LIMIT),
        name="mixer",
    )(x, gmix, wa, wb, wdt, convw, convb, dtb, alog, dskip, ssdn, scw, scn, wout, gffn, wrh, wrl,
      tri, e3, gsum)


def _moe_kernel(bstart_ref, nblk_ref, pk_ref, hn2p_ref, wg_ref, wu_ref, wd_ref,
                y2_ref, hn2v, xbuf, ybuf, wgb, wub, wdb, idx_ref, lsem, ssem, isem, *, n_tok):
    R = R_BLK
    H = R_HALF
    slab = R * ROW_TILES
    e = pl.program_id(0)
    g0 = bstart_ref[e]
    nb_e = nblk_ref[e]

    def idx_word(sl, field, j):
        return idx_ref[sl, 2 * field + j // LANES, j % LANES]

    def idx_copy(g, sl):
        return pltpu.make_async_copy(pk_ref.at[g], idx_ref.at[sl], isem.at[sl])

    def gather_rows(sl, lo, hi):
        for j in range(lo, hi):
            row = pl.multiple_of(idx_word(sl, 0, j), ROW_TILES)
            sh = idx_word(sl, 1, j).astype(jnp.uint32)
            w = hn2v[pl.ds(row, ROW_TILES), :]
            w = jnp.left_shift(jnp.right_shift(w, sh), jnp.uint32(16))
            xbuf[pl.ds(j * ROW_TILES, ROW_TILES), :] = pltpu.bitcast(w, F32)

    def scatter_start(sl, lo, hi):
        for j in range(lo, hi):
            pltpu.make_async_copy(
                ybuf.at[pl.ds(sl * slab + j * ROW_TILES, ROW_TILES), :],
                y2_ref.at[idx_word(sl, 2, j)],
                ssem.at[sl]).start(priority=j % 2)

    def slab_wait(buf, sem, sl):
        view = buf.at[pl.ds(sl * slab, slab), :]
        pltpu.make_async_copy(view, view, sem.at[sl]).wait()

    n_idx_blocks = pk_ref.shape[0]

    @pl.when(e == 0)
    def _():
        load = pltpu.make_async_copy(hn2p_ref, hn2v, lsem.at[0])
        load.start()
        idx_copy(0, 0).start()
        prev = idx_copy(n_idx_blocks - 1, 1)
        prev.start()
        ybuf[...] = jnp.zeros(ybuf.shape, F32)
        for j in range(R):
            pltpu.make_async_copy(
                ybuf.at[pl.ds(j * ROW_TILES, ROW_TILES), :],
                y2_ref.at[2 * n_tok + j],
                ssem.at[0]).start(priority=j % 2)
        for j in range(H):
            pltpu.make_async_copy(
                ybuf.at[pl.ds(slab + j * ROW_TILES, ROW_TILES), :],
                y2_ref.at[2 * n_tok + R + j],
                ssem.at[1]).start(priority=j % 2)
        prev.wait()
        load.wait()

    @pl.when(nb_e > 0)
    def _():
        wgb[...] = wg_ref[0].astype(BF16)
        wub[...] = wu_ref[0].astype(BF16)
        wdb[...] = wd_ref[0].astype(BF16)

    def load_x(hf):
        return jnp.concatenate(
            [xbuf[pl.ds(hf * H * ROW_TILES + s, H, stride=ROW_TILES), :]
             for s in range(ROW_TILES)], axis=1).astype(BF16)

    def store_y(sl, hf, yy):
        for s in range(ROW_TILES):
            ybuf[pl.ds(sl * slab + hf * H * ROW_TILES + s, H, stride=ROW_TILES), :] = (
                yy[:, LANES * s:LANES * (s + 1)])

    c1, c2 = H // 3, 2 * (H // 3)

    def run_block(g, sl):
        idx_copy(g, sl).wait()
        slab_wait(ybuf, ssem, sl)
        gather_rows(sl, 0, H)
        xa = load_x(0)
        scatter_start(1 - sl, H, H + c1)
        gg = _dot(xa, wgb[...])
        gather_rows(sl, H, R)
        scatter_start(1 - sl, H + c1, H + c2)
        uu = _dot(xa, wub[...])
        act = (_silu(gg) * uu).astype(BF16)
        scatter_start(1 - sl, H + c2, R)
        idx_copy(g + 1, 1 - sl).start()
        store_y(sl, 0, _dot(act, wdb[...]))
        xb = load_x(1)
        scatter_start(sl, 0, c1)
        gg = _dot(xb, wgb[...])
        scatter_start(sl, c1, c2)
        uu = _dot(xb, wub[...])
        act = (_silu(gg) * uu).astype(BF16)
        scatter_start(sl, c2, H)
        store_y(sl, 1, _dot(act, wdb[...]))

    def block(i, carry):
        g = g0 + i
        par = jnp.bitwise_and(g, 1)

        @pl.when(par == 0)
        def _():
            run_block(g, 0)

        @pl.when(par == 1)
        def _():
            run_block(g, 1)

        return carry

    lax.fori_loop(0, nb_e, block, 0)

    @pl.when(e == pl.num_programs(0) - 1)
    def _():
        g_end = g0 + nb_e

        @pl.when(jnp.bitwise_and(g_end, 1) == 0)
        def _():
            scatter_start(1, H, R)
            idx_copy(g_end, 0).wait()

        @pl.when(jnp.bitwise_and(g_end, 1) == 1)
        def _():
            scatter_start(0, H, R)
            idx_copy(g_end, 1).wait()

        slab_wait(ybuf, ssem, 0)
        slab_wait(ybuf, ssem, 1)


def _moe_call(bstart, nblk, pk, hn2p, wg, wu, wd, n_tok):
    R = R_BLK
    slab = R * ROW_TILES
    grid_spec = pltpu.PrefetchScalarGridSpec(
        num_scalar_prefetch=2,
        grid=(N_EXPERTS,),
        in_specs=[
            pl.BlockSpec(memory_space=pl.ANY),
            pl.BlockSpec(memory_space=pl.ANY),
            pl.BlockSpec((1, D_MODEL, D_FF), lambda e, bs, nb: (e, 0, 0)),
            pl.BlockSpec((1, D_MODEL, D_FF), lambda e, bs, nb: (e, 0, 0)),
            pl.BlockSpec((1, D_FF, D_MODEL), lambda e, bs, nb: (e, 0, 0)),
        ],
        out_specs=pl.BlockSpec(memory_space=pl.ANY),
        scratch_shapes=[
            pltpu.VMEM(hn2p.shape, jnp.uint32),
            pltpu.VMEM((slab, LANES), F32),
            pltpu.VMEM((2 * slab, LANES), F32),
            pltpu.VMEM((D_MODEL, D_FF), BF16),
            pltpu.VMEM((D_MODEL, D_FF), BF16),
            pltpu.VMEM((D_FF, D_MODEL), BF16),
            pltpu.SMEM((2, SUBLANES, LANES), I32),
            pltpu.SemaphoreType.DMA((1,)),
            pltpu.SemaphoreType.DMA((2,)),
            pltpu.SemaphoreType.DMA((2,)),
        ],
    )
    return pl.pallas_call(
        functools.partial(_moe_kernel, n_tok=n_tok),
        grid_spec=grid_spec,
        out_shape=jax.ShapeDtypeStruct((2 * n_tok + 2 * R, ROW_TILES, LANES), F32),
        compiler_params=pltpu.CompilerParams(
            dimension_semantics=("arbitrary",),
            vmem_limit_bytes=VMEM_LIMIT),
        name="moe",
    )(bstart, nblk, pk, hn2p, wg, wu, wd)


def _combine_kernel(h_ref, y0_ref, y1_ref, g_ref, fn_ref, o_ref):
    T = T_CMB
    h = h_ref[...]
    y0 = jnp.concatenate([y0_ref[pl.ds(s, T, stride=ROW_TILES), :] for s in range(ROW_TILES)], axis=1)
    y1 = jnp.concatenate([y1_ref[pl.ds(s, T, stride=ROW_TILES), :] for s in range(ROW_TILES)], axis=1)
    g = g_ref[...]
    v = h + (y0 * g[:, 0:1] + y1 * g[:, 1:2])
    ms = jnp.mean(v * v, axis=-1, keepdims=True)
    o_ref[...] = v * lax.rsqrt(ms + EPS) * fn_ref[...]


def _combine_call(h2d, y2, gates, fnorm):
    n_tok, D = h2d.shape
    T = T_CMB
    nt = n_tok // T
    return pl.pallas_call(
        _combine_kernel,
        grid=(nt,),
        in_specs=[
            pl.BlockSpec((T, D), lambda i: (i, 0)),
            pl.BlockSpec((T * ROW_TILES, LANES), lambda i: (i, 0)),
            pl.BlockSpec((T * ROW_TILES, LANES), lambda i: (nt + i, 0)),
            pl.BlockSpec((T, LANES), lambda i: (i, 0)),
            pl.BlockSpec((1, D), lambda i: (0, 0)),
        ],
        out_specs=pl.BlockSpec((T, D), lambda i: (i, 0)),
        out_shape=jax.ShapeDtypeStruct((n_tok, D), F32),
        compiler_params=pltpu.CompilerParams(dimension_semantics=("arbitrary",)),
        name="combine",
    )(h2d, y2, y2, gates, fnorm)


def _plan_kernel(e_ref, ux_ref, ones_ref, lx_ref, dest_ref, bstart_ref, nblk_ref):
    rows = e_ref.shape[0]
    ev = e_ref[...]
    lane8 = lax.broadcasted_iota(I32, (SUBLANES, LANES), 1)
    dest = jnp.zeros((rows, LANES), F32)
    bstart = jnp.zeros((SUBLANES, LANES), I32)
    nblk = jnp.zeros((SUBLANES, LANES), I32)
    pstart = jnp.zeros((1, LANES), F32)
    for e in range(N_EXPERTS):
        ohb = ev == e
        oh = jnp.where(ohb, 1.0, 0.0).astype(BF16)
        within = _dot(oh, ux_ref[...])
        rtot = _dot(oh, ones_ref[...])
        rpre = _dot(lx_ref[...], rtot.astype(BF16))
        cnt = rpre[rows - 1:rows, :] + rtot[rows - 1:rows, :]
        dest = dest + jnp.where(ohb, within + rpre + pstart, 0.0)
        cnt_i = cnt.astype(I32)
        nb_e = lax.shift_right_logical(cnt_i + (R_BLK - 1), R_SHIFT)
        bstart = jnp.where(lane8 == e, lax.shift_right_logical(pstart.astype(I32), R_SHIFT), bstart)
        nblk = jnp.where(lane8 == e, nb_e, nblk)
        pstart = pstart + lax.shift_left(nb_e, R_SHIFT).astype(F32)
    dest_ref[...] = dest.astype(I32)
    bstart_ref[...] = bstart
    nblk_ref[...] = nblk


def _invert_kernel(dest_ref, init_ref, inv_ref):
    n_asg = dest_ref.shape[0]
    pltpu.sync_copy(init_ref, inv_ref)

    def put(a, c):
        inv_ref[dest_ref[a]] = a
        return c

    lax.fori_loop(0, n_asg, put, 0, unroll=16)


def _index_tiles_kernel(inv_ref, pk_ref, *, n_tok):
    nb = pk_ref.shape[0] // SUBLANES
    per_blk = R_BLK // LANES
    assert per_blk == 2
    pk_ref[...] = jnp.zeros(pk_ref.shape, I32)
    for half in range(per_blk):
        a = inv_ref[pl.ds(half, nb, stride=per_blk), :]
        t = jnp.bitwise_and(a, n_tok - 1)
        pk_ref[pl.ds(half, nb, stride=SUBLANES), :] = lax.shift_left(lax.shift_right_logical(t, 1), 3)
        pk_ref[pl.ds(2 + half, nb, stride=SUBLANES), :] = lax.shift_left(jnp.bitwise_and(a, 1), 4)
        pk_ref[pl.ds(4 + half, nb, stride=SUBLANES), :] = a


def _plan(e_rows, n_tok, nb):
    R = R_BLK
    n_asg = 2 * n_tok
    rows = n_asg // LANES
    emat = e_rows[0:2].reshape(rows, LANES)
    li = jnp.arange(LANES)
    ux = (li[:, None] < li[None, :]).astype(BF16)
    ones = jnp.ones((LANES, LANES), BF16)
    ri = jnp.arange(rows)
    lx = (ri[:, None] > ri[None, :]).astype(BF16)
    dest, bstart, nblk = pl.pallas_call(
        _plan_kernel,
        out_shape=[jax.ShapeDtypeStruct((rows, LANES), I32),
                   jax.ShapeDtypeStruct((SUBLANES, LANES), I32),
                   jax.ShapeDtypeStruct((SUBLANES, LANES), I32)],
        name="plan",
    )(emat, ux, ones, lx)
    pos = jnp.arange(nb * R, dtype=I32)
    init = n_asg + (pos & (R - 1)) + jnp.where(pos >= (nb - 1) * R, R, 0)
    inv = pl.pallas_call(
        _invert_kernel,
        in_specs=[pl.BlockSpec(memory_space=pltpu.SMEM), pl.BlockSpec(memory_space=pl.ANY)],
        out_specs=pl.BlockSpec(memory_space=pltpu.SMEM),
        out_shape=jax.ShapeDtypeStruct((nb * R,), I32),
        name="invert",
    )(dest.reshape(n_asg), init)
    pk = pl.pallas_call(
        functools.partial(_index_tiles_kernel, n_tok=n_tok),
        out_shape=jax.ShapeDtypeStruct((nb * SUBLANES, LANES), I32),
        name="index_tiles",
    )(inv.reshape(nb * R // LANES, LANES))
    return bstart[0, 0:N_EXPERTS], nblk[0, 0:N_EXPERTS], pk.reshape(nb, SUBLANES, LANES)


def kernel(x, norm_mix, w_in, ssd_conv_w, ssd_conv_b, dt_bias, a_log, d_skip, ssd_norm, sc_conv_w,
           sc_norm, w_out, norm_ffn, w_router_group, w_router_expert, w_gate, w_up, w_down, final_norm):
    B, L, D = x.shape
    n_tok = B * L
    depth = norm_mix.shape[0]
    assert depth == 1 and D == D_MODEL and (n_tok & (n_tok - 1)) == 0
    nb = -(-((2 * n_tok) // R_BLK + N_EXPERTS + 1) // SUBLANES) * SUBLANES

    o1 = SSD_WIDTH
    o2 = o1 + XBC
    o3 = o2 + N_HEADS
    wi = w_in.reshape(D, -1)
    wa = wi[:, 0:o2].astype(BF16)
    wb = wi[:, o3:].astype(BF16)
    wdt = jnp.pad(wi[:, o2:o3], ((0, 0), (0, LANES - N_HEADS))).astype(BF16)
    pad_h = (0, LANES - N_HEADS)
    dtb = jnp.pad(dt_bias[0], pad_h).reshape(1, LANES)
    alog = jnp.pad(a_log[0], pad_h).reshape(1, LANES)
    dskip = jnp.repeat(d_skip[0], HEAD_DIM).reshape(1, SSD_WIDTH)

    wre = jnp.transpose(w_router_expert[0], (1, 0, 2)).reshape(D, N_EXPERTS)
    wr = jnp.pad(jnp.concatenate([w_router_group[0], wre], axis=1),
                 ((0, 0), (0, LANES - N_GROUPS - N_EXPERTS)))
    wrh = wr.astype(BF16)
    wrl = (wr - wrh.astype(F32)).astype(BF16)

    ri = jnp.arange(T_MIX)
    tri = (ri[:, None] >= ri[None, :]).astype(BF16)
    er = jnp.arange(LANES)
    ec = jnp.arange(SSD_WIDTH)
    e3 = ((er[:, None] < 48) & ((er[:, None] % 16) == (ec[None, :] // HEAD_DIM))).astype(BF16)
    gsum = ((ec[:, None] // HEAD_DIM) == er[None, :]).astype(BF16)

    h, hn2, e_rows, gates = _mixer_call(
        x, norm_mix[0].reshape(1, D), wa, wb, wdt, ssd_conv_w[0], ssd_conv_b[0].reshape(1, XBC), dtb, alog,
        dskip, ssd_norm[0].reshape(1, SSD_WIDTH), sc_conv_w[0], sc_norm[0].reshape(1, SC_WIDTH),
        w_out.reshape(-1, D).astype(BF16), norm_ffn[0].reshape(1, D), wrh, wrl, tri, e3, gsum)

    bstart, nblk, pk = _plan(e_rows, n_tok, nb)
    y2 = _moe_call(bstart, nblk, pk, hn2, w_gate.reshape(N_EXPERTS, D, D_FF),
                   w_up.reshape(N_EXPERTS, D, D_FF), w_down.reshape(N_EXPERTS, D_FF, D), n_tok)
    out = _combine_call(h.reshape(n_tok, D), y2.reshape(-1, LANES), gates, final_norm.reshape(1, D))
    return out.reshape(B, L, D)
```

```python
import functools

import jax
import jax.numpy as jnp
from jax import lax
from jax.experimental import pallas as pl
from jax.experimental.pallas import tpu as pltpu

F32 = jnp.float32
BF16 = jnp.bfloat16
I32 = jnp.int32

EPS = 1e-6
D_MODEL = 1024
N_HEADS = 16
HEAD_DIM = 64
N_BC_GROUPS = 2
STATE = 128
SSD_WIDTH = 1024
XBC = SSD_WIDTH + 2 * N_BC_GROUPS * STATE
SC_WIDTH = 1024
SC_GROUPS = 16
N_GROUPS = 4
EPG = 8
N_EXPERTS = 32
D_FF = 512

LANES = 128
SUBLANES = 8
ROW_TILES = D_MODEL // LANES

T_MIX = 512
PAIR_ROWS = T_MIX // 2
R_BLK = 256
R_SHIFT = 8
N_SLOTS = 3
T_CMB = 512

VMEM_LIMIT = 56 * 1024 * 1024


def _dot(a, b):
    return jnp.dot(a, b, preferred_element_type=F32)


def _split3(v):
    p1 = v.astype(BF16).astype(F32)
    r1 = v - p1
    p2 = r1.astype(BF16).astype(F32)
    p3 = (r1 - p2).astype(BF16).astype(F32)
    return p1, p2, p3


def _pack3(v):
    p1, p2, p3 = _split3(v)
    return (p1 + pltpu.roll(p2, 16, 1) + pltpu.roll(p3, 32, 1)).astype(BF16)


def _silu(v):
    return v * jax.nn.sigmoid(v)


def _mixer_kernel(x_ref, gmix_ref, wa_ref, wb_ref, wdt_ref, convw_ref, convb_ref, dtb_ref, alog_ref,
                  dskip_ref, ssdn_ref, scw_ref, scn_ref, wout_ref, gffn_ref, wrh_ref, wrl_ref,
                  tri_ref, e3_ref, gsum_ref,
                  h_ref, hn2_ref, e_ref, g_ref,
                  cbuf, sbuf, st_ref, pbuf):
    T = T_MIX
    t = pl.program_id(1)

    @pl.when(t == 0)
    def _():
        cbuf[0:8, :] = jnp.zeros((8, XBC), F32)
        sbuf[0:8, :] = jnp.zeros((8, SC_WIDTH), F32)
        st_ref[...] = jnp.zeros(st_ref.shape, F32)

    @pl.when(t > 0)
    def _():
        cbuf[0:8, :] = cbuf[T:T + 8, :]
        sbuf[0:8, :] = sbuf[T:T + 8, :]

    x = x_ref[0]
    ms = jnp.mean(x * x, axis=-1, keepdims=True)
    hn = (x * lax.rsqrt(ms + EPS) * gmix_ref[...]).astype(BF16)

    z = _dot(hn, wa_ref[:, 0:1024])
    xbc = _dot(hn, wa_ref[:, 1024:2560])
    scb = _dot(hn, wb_ref[:, 0:1024])
    scc = _dot(hn, wb_ref[:, 1024:2048])
    scv = _dot(hn, wb_ref[:, 2048:3072])
    dt_raw = _dot(hn, wdt_ref[...])

    cbuf[8:8 + T, :] = xbc
    cw = convw_ref[...]
    acc = convb_ref[...] + cw[3:4, :] * xbc
    for k in range(3):
        acc = acc + cw[k:k + 1, :] * cbuf[5 + k:5 + k + T, :]
    xact = _silu(acc)
    xs = xact[:, 0:SSD_WIDTH]

    lane = lax.broadcasted_iota(I32, (1, LANES), 1)
    hmask = lane < N_HEADS
    a = jnp.where(hmask, -jnp.exp(alog_ref[...]), 0.0)
    dtv = dt_raw + dtb_ref[...]
    dt = jnp.where(hmask, jnp.maximum(dtv, 0.0) + jnp.log1p(jnp.exp(-jnp.abs(dtv))), 0.0)
    adt = dt * a
    c3 = _dot(tri_ref[...], _pack3(adt))
    ac = jnp.where(hmask, c3 + pltpu.roll(c3, LANES - 16, 1) + pltpu.roll(c3, LANES - 32, 1), 0.0)
    ac_last = ac[T - 1:T, :]
    eac = jnp.where(hmask, jnp.exp(ac), 0.0)
    wdt = dt * jnp.exp(ac_last - ac)

    stacked = jnp.concatenate([_pack3(dt), _pack3(wdt), _pack3(eac)], axis=0)
    ex = _dot(stacked, e3_ref[...])
    dt_e = ex[0:T]
    wdt_e = ex[T:2 * T]
    eac_e = ex[2 * T:3 * T]
    xdt = (xs * dt_e).astype(BF16)
    xdtw = (xs * wdt_e).astype(BF16)

    ac_rows = ac.T
    rr = lax.broadcasted_iota(I32, (T, T), 0)
    cc = lax.broadcasted_iota(I32, (T, T), 1)
    causal = rr >= cc
    cblk = lax.shift_right_logical(lax.broadcasted_iota(I32, (T, 4 * HEAD_DIM), 1), 6)

    y_parts = []
    for g in range(N_BC_GROUPS):
        bg = xact[:, SSD_WIDTH + STATE * g:SSD_WIDTH + STATE * (g + 1)]
        cg = xact[:, SSD_WIDTH + 2 * STATE + STATE * g:SSD_WIDTH + 2 * STATE + STATE * (g + 1)]
        bb = bg.astype(BF16)
        cbf = cg.astype(BF16)
        cb = lax.dot_general(cbf, bb, (((1,), (1,)), ((), ())), preferred_element_type=F32)
        st = st_ref[g]
        yoff = _dot(cbf, st.astype(BF16))
        dec = eac_e[T - 1:T, 512 * g:512 * (g + 1)]
        bgt = bg.T.astype(BF16)
        st_ref[g] = st * dec + _dot(bgt, xdtw[:, 512 * g:512 * (g + 1)])
        for q in range(2):
            ms_list = []
            for r in range(4):
                hh = 8 * g + 4 * q + r
                seg = ac[:, hh:hh + 1] - ac_rows[hh:hh + 1, :]
                lh = jnp.exp(jnp.where(causal, seg, -jnp.inf))
                ms_list.append((cb * lh).astype(BF16))
            lhs = jnp.concatenate(ms_list, axis=1)
            lo = 512 * g + 256 * q
            x4 = xdt[:, lo:lo + 256]
            rhs = jnp.concatenate(
                [jnp.where(cblk == r, x4, jnp.zeros_like(x4)) for r in range(4)], axis=0)
            yd = _dot(lhs, rhs)
            y_parts.append(yd + eac_e[:, lo:lo + 256] * yoff[:, 256 * q:256 * (q + 1)])
    y = jnp.concatenate(y_parts, axis=1) + dskip_ref[...] * xs

    v = y * _silu(z)
    outs = []
    for g in range(N_BC_GROUPS):
        vg = v[:, 512 * g:512 * (g + 1)]
        msg = jnp.mean(vg * vg, axis=-1, keepdims=True)
        outs.append(vg * lax.rsqrt(msg + EPS))
    ssd_out = (jnp.concatenate(outs, axis=1) * ssdn_ref[...]).astype(BF16)

    u = scc * scv
    sbuf[8:8 + T, :] = u
    sw = scw_ref[...]
    conv = sw[2:3, :] * u + sw[1:2, :] * sbuf[7:7 + T, :] + sw[0:1, :] * sbuf[6:6 + T, :]
    sc = scb * conv
    gs = _dot((sc * sc).astype(BF16), gsum_ref[...])
    rstd = jnp.where(hmask, lax.rsqrt(gs * (1.0 / HEAD_DIM) + EPS), 0.0)
    rstd_e = _dot(_pack3(rstd), e3_ref[...])
    sc_out = (sc * rstd_e * scn_ref[...]).astype(BF16)

    mix = _dot(ssd_out, wout_ref[0:1024, :]) + _dot(sc_out, wout_ref[1024:2048, :])
    h = x + mix
    h_ref[0] = h

    ms2 = jnp.mean(h * h, axis=-1, keepdims=True)
    hn2 = h * lax.rsqrt(ms2 + EPS) * gffn_ref[...]
    hi = hn2.astype(BF16)
    bits = pltpu.bitcast(hi.astype(F32), jnp.uint32)
    for s in range(ROW_TILES):
        pbuf[pl.ds(s * T, T), :] = bits[:, LANES * s:LANES * (s + 1)]
    for s in range(ROW_TILES):
        even = pbuf[pl.ds(s * T, PAIR_ROWS, stride=2), :]
        odd = pbuf[pl.ds(s * T + 1, PAIR_ROWS, stride=2), :]
        hn2_ref[pl.ds(s, PAIR_ROWS, stride=ROW_TILES), :] = jnp.bitwise_or(
            jnp.right_shift(even, jnp.uint32(16)), odd)
    lo_ = (hn2 - hi.astype(F32)).astype(BF16)
    logits = _dot(hi, wrh_ref[...]) + (_dot(lo_, wrh_ref[...]) + _dot(hi, wrl_ref[...]))

    lanef = lax.broadcasted_iota(I32, (T, LANES), 1)
    lane_f = lanef.astype(F32)
    neg = -jnp.inf
    big = 1e9
    gl = jnp.where(lanef < N_GROUPS, logits, neg)
    gmax = jnp.max(gl, axis=-1, keepdims=True)
    gidx = jnp.min(jnp.where(gl == gmax, lane_f, big), axis=-1, keepdims=True)
    gsum = jnp.sum(jnp.where(lanef < N_GROUPS, jnp.exp(logits - gmax), 0.0), axis=-1, keepdims=True)
    gw = 1.0 / gsum
    egrp = lax.shift_right_logical(lanef - N_GROUPS, 3).astype(F32)
    in_grp = (lanef >= N_GROUPS) & (lanef < N_GROUPS + N_EXPERTS) & (egrp == gidx)
    el = jnp.where(in_grp, logits, neg)
    v1 = jnp.max(el, axis=-1, keepdims=True)
    i1 = jnp.min(jnp.where(el == v1, lane_f, big), axis=-1, keepdims=True)
    el2 = jnp.where(lane_f == i1, neg, el)
    v2 = jnp.max(el2, axis=-1, keepdims=True)
    i2 = jnp.min(jnp.where(el2 == v2, lane_f, big), axis=-1, keepdims=True)
    p = jnp.exp(v2 - v1)
    s1 = 1.0 / (1.0 + p)
    gate1 = gw * s1
    gate2 = gw * (p * s1)
    g_ref[...] = jnp.where(lanef == 0, gate1, jnp.where(lanef == 1, gate2, 0.0))
    emat = jnp.where(lanef == 0, i1 - N_GROUPS, jnp.where(lanef == 1, i2 - N_GROUPS, 0.0))
    e_ref[...] = emat.T[0:8, :].astype(I32)


def _mixer_call(x, gmix, wa, wb, wdt, convw, convb, dtb, alog, dskip, ssdn, scw, scn, wout, gffn,
                wrh, wrl, tri, e3, gsum):
    B, L, D = x.shape
    T = T_MIX
    nt = L // T
    n_tok = B * L

    def const(shape):
        return pl.BlockSpec(shape, lambda b, t: (0,) * len(shape), pipeline_mode=pl.Buffered(1))

    in_specs = [
        pl.BlockSpec((1, T, D), lambda b, t: (b, t, 0)),
        const((1, D)),
        const(wa.shape), const(wb.shape), const(wdt.shape), const(convw.shape), const(convb.shape),
        const(dtb.shape), const(alog.shape), const(dskip.shape), const(ssdn.shape),
        const(scw.shape), const(scn.shape), const(wout.shape), const(gffn.shape),
        const(wrh.shape), const(wrl.shape), const(tri.shape), const(e3.shape), const(gsum.shape),
    ]
    out_shape = [
        jax.ShapeDtypeStruct((B, L, D), F32),
        jax.ShapeDtypeStruct((n_tok // 2 * ROW_TILES, LANES), jnp.uint32),
        jax.ShapeDtypeStruct((8, n_tok), I32),
        jax.ShapeDtypeStruct((n_tok, LANES), F32),
    ]
    out_specs = [
        pl.BlockSpec((1, T, D), lambda b, t: (b, t, 0)),
        pl.BlockSpec((PAIR_ROWS * ROW_TILES, LANES), lambda b, t: (b * nt + t, 0)),
        pl.BlockSpec((8, T), lambda b, t: (0, b * nt + t)),
        pl.BlockSpec((T, LANES), lambda b, t: (b * nt + t, 0)),
    ]
    return pl.pallas_call(
        _mixer_kernel,
        grid=(B, nt),
        in_specs=in_specs,
        out_specs=out_specs,
        out_shape=out_shape,
        scratch_shapes=[
            pltpu.VMEM((T + 8, XBC), F32),
            pltpu.VMEM((T + 8, SC_WIDTH), F32),
            pltpu.VMEM((N_BC_GROUPS, STATE, 512), F32),
            pltpu.VMEM((ROW_TILES * T, LANES), jnp.uint32),
        ],
        compiler_params=pltpu.CompilerParams(
            dimension_semantics=("arbitrary", "arbitrary"),
            vmem_limit_bytes=VMEM_LIMIT),
        name="mixer",
    )(x, gmix, wa, wb, wdt, convw, convb, dtb, alog, dskip, ssdn, scw, scn, wout, gffn, wrh, wrl,
      tri, e3, gsum)


def _moe_kernel(bstart_ref, nblk_ref, pk_ref, hn2p_ref, wg_ref, wu_ref, wd_ref,
                y2_ref, hn2v, xbuf, ybuf, wgb, wub, wdb, idx_ref, lsem, ssem, isem, *, n_tok):
    R = R_BLK
    slab = R * ROW_TILES
    e = pl.program_id(0)
    g0 = bstart_ref[e]
    nb_e = nblk_ref[e]

    def idx_word(sl, field, j):
        return idx_ref[sl, 2 * field + j // LANES, j % LANES]

    def idx_copy(g, sl):
        return pltpu.make_async_copy(pk_ref.at[g], idx_ref.at[sl], isem.at[sl])

    def gather_rows(sl, lo, hi):
        for j in range(lo, hi):
            row = pl.multiple_of(idx_word(sl, 0, j), ROW_TILES)
            sh = idx_word(sl, 1, j).astype(jnp.uint32)
            w = hn2v[pl.ds(row, ROW_TILES), :]
            w = jnp.left_shift(jnp.right_shift(w, sh), jnp.uint32(16))
            xbuf[pl.ds(j * ROW_TILES, ROW_TILES), :] = pltpu.bitcast(w, F32)

    def scatter_start(sl, lo, hi):
        for j in range(lo, hi):
            pltpu.make_async_copy(
                ybuf.at[pl.ds(sl * slab + j * ROW_TILES, ROW_TILES), :],
                y2_ref.at[idx_word(sl, 2, j)],
                ssem.at[sl]).start(priority=j % 2)

    def slab_wait(sl):
        view = ybuf.at[pl.ds(sl * slab, slab), :]
        pltpu.make_async_copy(view, view, ssem.at[sl]).wait()

    n_idx_blocks = pk_ref.shape[0]

    @pl.when(e == 0)
    def _():
        load = pltpu.make_async_copy(hn2p_ref, hn2v, lsem.at[0])
        load.start()
        idx_copy(0, 0).start()
        prev = idx_copy(n_idx_blocks - 1, N_SLOTS - 1)
        prev.start()
        ybuf[...] = jnp.zeros(ybuf.shape, F32)
        for sl, base in ((0, 2 * n_tok), (1, 2 * n_tok + 2 * R)):
            for j in range(R):
                pltpu.make_async_copy(
                    ybuf.at[pl.ds(sl * slab + j * ROW_TILES, ROW_TILES), :],
                    y2_ref.at[base + j],
                    ssem.at[sl]).start(priority=j % 2)
        prev.wait()
        load.wait()

    @pl.when(nb_e > 0)
    def _():
        wgb[...] = wg_ref[0].astype(BF16)
        wub[...] = wu_ref[0].astype(BF16)
        wdb[...] = wd_ref[0].astype(BF16)

    c1, c2 = R // 3, 2 * (R // 3)

    def run_block(g, sl):
        prv = (sl + N_SLOTS - 1) % N_SLOTS
        nxt = (sl + 1) % N_SLOTS
        idx_copy(g, sl).wait()
        idx_copy(g + 1, nxt).start()
        slab_wait(sl)
        gather_rows(sl, 0, R)
        x = jnp.concatenate(
            [xbuf[pl.ds(s, R, stride=ROW_TILES), :] for s in range(ROW_TILES)], axis=1).astype(BF16)
        scatter_start(prv, 0, c1)
        gg = _dot(x, wgb[...])
        scatter_start(prv, c1, c2)
        uu = _dot(x, wub[...])
        act = (_silu(gg) * uu).astype(BF16)
        scatter_start(prv, c2, R)
        yy = _dot(act, wdb[...])
        for s in range(ROW_TILES):
            ybuf[pl.ds(sl * slab + s, R, stride=ROW_TILES), :] = yy[:, LANES * s:LANES * (s + 1)]

    def block(i, carry):
        g = g0 + i
        sl = lax.rem(g, N_SLOTS)
        for k in range(N_SLOTS):
            @pl.when(sl == k)
            def _(k=k):
                run_block(g, k)
        return carry

    lax.fori_loop(0, nb_e, block, 0)

    @pl.when(e == pl.num_programs(0) - 1)
    def _():
        g_end = g0 + nb_e
        sl_end = lax.rem(g_end, N_SLOTS)
        for k in range(N_SLOTS):
            @pl.when(sl_end == k)
            def _(k=k):
                scatter_start((k + N_SLOTS - 1) % N_SLOTS, 0, R)
                idx_copy(g_end, k).wait()
        for k in range(N_SLOTS):
            slab_wait(k)


def _moe_call(bstart, nblk, pk, hn2p, wg, wu, wd, n_tok):
    R = R_BLK
    slab = R * ROW_TILES
    grid_spec = pltpu.PrefetchScalarGridSpec(
        num_scalar_prefetch=2,
        grid=(N_EXPERTS,),
        in_specs=[
            pl.BlockSpec(memory_space=pl.ANY),
            pl.BlockSpec(memory_space=pl.ANY),
            pl.BlockSpec((1, D_MODEL, D_FF), lambda e, bs, nb: (e, 0, 0)),
            pl.BlockSpec((1, D_MODEL, D_FF), lambda e, bs, nb: (e, 0, 0)),
            pl.BlockSpec((1, D_FF, D_MODEL), lambda e, bs, nb: (e, 0, 0)),
        ],
        out_specs=pl.BlockSpec(memory_space=pl.ANY),
        scratch_shapes=[
            pltpu.VMEM(hn2p.shape, jnp.uint32),
            pltpu.VMEM((slab, LANES), F32),
            pltpu.VMEM((N_SLOTS * slab, LANES), F32),
            pltpu.VMEM((D_MODEL, D_FF), BF16),
            pltpu.VMEM((D_MODEL, D_FF), BF16),
            pltpu.VMEM((D_FF, D_MODEL), BF16),
            pltpu.SMEM((N_SLOTS, SUBLANES, LANES), I32),
            pltpu.SemaphoreType.DMA((1,)),
            pltpu.SemaphoreType.DMA((N_SLOTS,)),
            pltpu.SemaphoreType.DMA((N_SLOTS,)),
        ],
    )
    return pl.pallas_call(
        functools.partial(_moe_kernel, n_tok=n_tok),
        grid_spec=grid_spec,
        out_shape=jax.ShapeDtypeStruct((2 * n_tok + N_SLOTS * R, ROW_TILES, LANES), F32),
        compiler_params=pltpu.CompilerParams(
            dimension_semantics=("arbitrary",),
            vmem_limit_bytes=VMEM_LIMIT),
        name="moe",
    )(bstart, nblk, pk, hn2p, wg, wu, wd)


def _combine_kernel(h_ref, y0_ref, y1_ref, g_ref, fn_ref, o_ref):
    T = T_CMB
    h = h_ref[...]
    y0 = jnp.concatenate([y0_ref[pl.ds(s, T, stride=ROW_TILES), :] for s in range(ROW_TILES)], axis=1)
    y1 = jnp.concatenate([y1_ref[pl.ds(s, T, stride=ROW_TILES), :] for s in range(ROW_TILES)], axis=1)
    g = g_ref[...]
    v = h + (y0 * g[:, 0:1] + y1 * g[:, 1:2])
    ms = jnp.mean(v * v, axis=-1, keepdims=True)
    o_ref[...] = v * lax.rsqrt(ms + EPS) * fn_ref[...]


def _combine_call(h2d, y2, gates, fnorm):
    n_tok, D = h2d.shape
    T = T_CMB
    nt = n_tok // T
    return pl.pallas_call(
        _combine_kernel,
        grid=(nt,),
        in_specs=[
            pl.BlockSpec((T, D), lambda i: (i, 0)),
            pl.BlockSpec((T * ROW_TILES, LANES), lambda i: (i, 0)),
            pl.BlockSpec((T * ROW_TILES, LANES), lambda i: (nt + i, 0)),
            pl.BlockSpec((T, LANES), lambda i: (i, 0)),
            pl.BlockSpec((1, D), lambda i: (0, 0)),
        ],
        out_specs=pl.BlockSpec((T, D), lambda i: (i, 0)),
        out_shape=jax.ShapeDtypeStruct((n_tok, D), F32),
        compiler_params=pltpu.CompilerParams(dimension_semantics=("arbitrary",)),
        name="combine",
    )(h2d, y2, y2, gates, fnorm)


def _plan_kernel(e_ref, ux_ref, ones_ref, lx_ref, dest_ref, bstart_ref, nblk_ref):
    rows = e_ref.shape[0]
    ev = e_ref[...]
    lane8 = lax.broadcasted_iota(I32, (SUBLANES, LANES), 1)
    dest = jnp.zeros((rows, LANES), F32)
    bstart = jnp.zeros((SUBLANES, LANES), I32)
    nblk = jnp.zeros((SUBLANES, LANES), I32)
    pstart = jnp.zeros((1, LANES), F32)
    for e in range(N_EXPERTS):
        ohb = ev == e
        oh = jnp.where(ohb, 1.0, 0.0).astype(BF16)
        within = _dot(oh, ux_ref[...])
        rtot = _dot(oh, ones_ref[...])
        rpre = _dot(lx_ref[...], rtot.astype(BF16))
        cnt = rpre[rows - 1:rows, :] + rtot[rows - 1:rows, :]
        dest = dest + jnp.where(ohb, within + rpre + pstart, 0.0)
        cnt_i = cnt.astype(I32)
        nb_e = lax.shift_right_logical(cnt_i + (R_BLK - 1), R_SHIFT)
        bstart = jnp.where(lane8 == e, lax.shift_right_logical(pstart.astype(I32), R_SHIFT), bstart)
        nblk = jnp.where(lane8 == e, nb_e, nblk)
        pstart = pstart + lax.shift_left(nb_e, R_SHIFT).astype(F32)
    dest_ref[...] = dest.astype(I32)
    bstart_ref[...] = bstart
    nblk_ref[...] = nblk


def _invert_kernel(dest_ref, init_ref, inv_ref):
    n_asg = dest_ref.shape[0]
    pltpu.sync_copy(init_ref, inv_ref)

    def put(a, c):
        inv_ref[dest_ref[a]] = a
        return c

    lax.fori_loop(0, n_asg, put, 0, unroll=16)


def _index_tiles_kernel(inv_ref, pk_ref, *, n_tok):
    nb = pk_ref.shape[0] // SUBLANES
    per_blk = R_BLK // LANES
    assert per_blk == 2
    pk_ref[...] = jnp.zeros(pk_ref.shape, I32)
    for half in range(per_blk):
        a = inv_ref[pl.ds(half, nb, stride=per_blk), :]
        t = jnp.bitwise_and(a, n_tok - 1)
        pk_ref[pl.ds(half, nb, stride=SUBLANES), :] = lax.shift_left(lax.shift_right_logical(t, 1), 3)
        pk_ref[pl.ds(2 + half, nb, stride=SUBLANES), :] = lax.shift_left(jnp.bitwise_and(a, 1), 4)
        pk_ref[pl.ds(4 + half, nb, stride=SUBLANES), :] = a


def _plan(e_rows, n_tok, nb):
    R = R_BLK
    n_asg = 2 * n_tok
    rows = n_asg // LANES
    emat = e_rows[0:2].reshape(rows, LANES)
    li = jnp.arange(LANES)
    ux = (li[:, None] < li[None, :]).astype(BF16)
    ones = jnp.ones((LANES, LANES), BF16)
    ri = jnp.arange(rows)
    lx = (ri[:, None] > ri[None, :]).astype(BF16)
    dest, bstart, nblk = pl.pallas_call(
        _plan_kernel,
        out_shape=[jax.ShapeDtypeStruct((rows, LANES), I32),
                   jax.ShapeDtypeStruct((SUBLANES, LANES), I32),
                   jax.ShapeDtypeStruct((SUBLANES, LANES), I32)],
        name="plan",
    )(emat, ux, ones, lx)
    pos = jnp.arange(nb * R, dtype=I32)
    init = n_asg + (pos & (R - 1)) + jnp.where(pos >= (nb - 1) * R, R, 0)
    inv = pl.pallas_call(
        _invert_kernel,
        in_specs=[pl.BlockSpec(memory_space=pltpu.SMEM), pl.BlockSpec(memory_space=pl.ANY)],
        out_specs=pl.BlockSpec(memory_space=pltpu.SMEM),
        out_shape=jax.ShapeDtypeStruct((nb * R,), I32),
        name="invert",
    )(dest.reshape(n_asg), init)
    pk = pl.pallas_call(
        functools.partial(_index_tiles_kernel, n_tok=n_tok),
        out_shape=jax.ShapeDtypeStruct((nb * SUBLANES, LANES), I32),
        name="index_tiles",
    )(inv.reshape(nb * R // LANES, LANES))
    return bstart[0, 0:N_EXPERTS], nblk[0, 0:N_EXPERTS], pk.reshape(nb, SUBLANES, LANES)


def kernel(x, norm_mix, w_in, ssd_conv_w, ssd_conv_b, dt_bias, a_log, d_skip, ssd_norm, sc_conv_w,
           sc_norm, w_out, norm_ffn, w_router_group, w_router_expert, w_gate, w_up, w_down, final_norm):
    B, L, D = x.shape
    n_tok = B * L
    depth = norm_mix.shape[0]
    assert depth == 1 and D == D_MODEL and (n_tok & (n_tok - 1)) == 0
    nb = -(-((2 * n_tok) // R_BLK + N_EXPERTS + 1) // SUBLANES) * SUBLANES

    o1 = SSD_WIDTH
    o2 = o1 + XBC
    o3 = o2 + N_HEADS
    wi = w_in.reshape(D, -1)
    wa = wi[:, 0:o2].astype(BF16)
    wb = wi[:, o3:].astype(BF16)
    wdt = jnp.pad(wi[:, o2:o3], ((0, 0), (0, LANES - N_HEADS))).astype(BF16)
    pad_h = (0, LANES - N_HEADS)
    dtb = jnp.pad(dt_bias[0], pad_h).reshape(1, LANES)
    alog = jnp.pad(a_log[0], pad_h).reshape(1, LANES)
    dskip = jnp.repeat(d_skip[0], HEAD_DIM).reshape(1, SSD_WIDTH)

    wre = jnp.transpose(w_router_expert[0], (1, 0, 2)).reshape(D, N_EXPERTS)
    wr = jnp.pad(jnp.concatenate([w_router_group[0], wre], axis=1),
                 ((0, 0), (0, LANES - N_GROUPS - N_EXPERTS)))
    wrh = wr.astype(BF16)
    wrl = (wr - wrh.astype(F32)).astype(BF16)

    ri = jnp.arange(T_MIX)
    tri = (ri[:, None] >= ri[None, :]).astype(BF16)
    er = jnp.arange(LANES)
    ec = jnp.arange(SSD_WIDTH)
    e3 = ((er[:, None] < 48) & ((er[:, None] % 16) == (ec[None, :] // HEAD_DIM))).astype(BF16)
    gsum = ((ec[:, None] // HEAD_DIM) == er[None, :]).astype(BF16)

    h, hn2, e_rows, gates = _mixer_call(
        x, norm_mix[0].reshape(1, D), wa, wb, wdt, ssd_conv_w[0], ssd_conv_b[0].reshape(1, XBC), dtb, alog,
        dskip, ssd_norm[0].reshape(1, SSD_WIDTH), sc_conv_w[0], sc_norm[0].reshape(1, SC_WIDTH),
        w_out.reshape(-1, D).astype(BF16), norm_ffn[0].reshape(1, D), wrh, wrl, tri, e3, gsum)

    bstart, nblk, pk = _plan(e_rows, n_tok, nb)
    y2 = _moe_call(bstart, nblk, pk, hn2, w_gate.reshape(N_EXPERTS, D, D_FF),
                   w_up.reshape(N_EXPERTS, D, D_FF), w_down.reshape(N_EXPERTS, D_FF, D), n_tok)
    out = _combine_call(h.reshape(n_tok, D), y2.reshape(-1, LANES), gates, final_norm.reshape(1, D))
    return out.reshape(B, L, D)
```

```python
import functools

import jax
import jax.numpy as jnp
from jax import lax
from jax.experimental import pallas as pl
from jax.experimental.pallas import tpu as pltpu

F32 = jnp.float32
BF16 = jnp.bfloat16
I32 = jnp.int32

EPS = 1e-6
D_MODEL = 1024
N_HEADS = 16
HEAD_DIM = 64
N_BC_GROUPS = 2
STATE = 128
SSD_WIDTH = 1024
XBC = SSD_WIDTH + 2 * N_BC_GROUPS * STATE
SC_WIDTH = 1024
SC_GROUPS = 16
N_GROUPS = 4
EPG = 8
N_EXPERTS = 32
D_FF = 512

LANES = 128
SUBLANES = 8
ROW_TILES = D_MODEL // LANES

T_MIX = 256
Q_SSD = 128
PAIR_ROWS = T_MIX // 2
R_BLK = 256
R_SHIFT = 8
ROUTER_ROWS = 48
N_SLOTS = 3
T_CMB = 512

VMEM_LIMIT = 56 * 1024 * 1024


def _dot(a, b):
    return jnp.dot(a, b, preferred_element_type=F32)


def _split3(v):
    p1 = v.astype(BF16).astype(F32)
    r1 = v - p1
    p2 = r1.astype(BF16).astype(F32)
    p3 = (r1 - p2).astype(BF16).astype(F32)
    return p1, p2, p3


def _pack3(v):
    p1, p2, p3 = _split3(v)
    return (p1 + pltpu.roll(p2, 16, 1) + pltpu.roll(p3, 32, 1)).astype(BF16)


def _silu(v):
    return v * jax.nn.sigmoid(v)


def _mixer_kernel(x_ref, gmix_ref, wa_ref, wb_ref, wdt_ref, convw_ref, convb_ref, dtb_ref, alog_ref,
                  dskip_ref, ssdn_ref, scw_ref, scn_ref, wout_ref, gffn_ref, wr_ref,
                  tri_ref, e3_ref, gsum_ref,
                  h_ref, hn2_ref, e_ref, g_ref,
                  cbuf, sbuf, st_ref, pbuf):
    T = T_MIX
    t = pl.program_id(1)

    @pl.when(t == 0)
    def _():
        cbuf[0:8, :] = jnp.zeros((8, XBC), F32)
        sbuf[0:8, :] = jnp.zeros((8, SC_WIDTH), F32)
        st_ref[...] = jnp.zeros(st_ref.shape, F32)

    @pl.when(t > 0)
    def _():
        cbuf[0:8, :] = cbuf[T:T + 8, :]
        sbuf[0:8, :] = sbuf[T:T + 8, :]

    x = x_ref[0]
    ms = jnp.mean(x * x, axis=-1, keepdims=True)
    hn = (x * lax.rsqrt(ms + EPS) * gmix_ref[...]).astype(BF16)

    z = _dot(hn, wa_ref[:, 0:1024])
    xbc = _dot(hn, wa_ref[:, 1024:2560])
    scb = _dot(hn, wb_ref[:, 0:1024])
    scc = _dot(hn, wb_ref[:, 1024:2048])
    scv = _dot(hn, wb_ref[:, 2048:3072])
    dt_raw = _dot(hn, wdt_ref[...])

    cbuf[8:8 + T, :] = xbc
    cw = convw_ref[...]
    acc = convb_ref[...] + cw[3:4, :] * xbc
    for k in range(3):
        acc = acc + cw[k:k + 1, :] * cbuf[5 + k:5 + k + T, :]
    xact = _silu(acc)
    xs = xact[:, 0:SSD_WIDTH]

    lane = lax.broadcasted_iota(I32, (1, LANES), 1)
    hmask = lane < N_HEADS
    a = jnp.where(hmask, -jnp.exp(alog_ref[...]), 0.0)
    dtv = dt_raw + dtb_ref[...]
    dt = jnp.where(hmask, jnp.maximum(dtv, 0.0) + jnp.log1p(jnp.exp(-jnp.abs(dtv))), 0.0)
    adt = dt * a
    c3 = _dot(tri_ref[...], _pack3(adt))
    ac = jnp.where(hmask, c3 + pltpu.roll(c3, LANES - 16, 1) + pltpu.roll(c3, LANES - 32, 1), 0.0)
    Q = Q_SSD
    n_sub = T // Q
    ends = [ac[(c + 1) * Q - 1:(c + 1) * Q, :] for c in range(n_sub)]
    base = jnp.concatenate([jnp.broadcast_to(ends[c - 1] if c else jnp.zeros_like(ends[0]), (Q, LANES))
                            for c in range(n_sub)], axis=0)
    endv = jnp.concatenate([jnp.broadcast_to(ends[c], (Q, LANES)) for c in range(n_sub)], axis=0)
    eac = jnp.where(hmask, jnp.exp(ac - base), 0.0)
    wdt = dt * jnp.exp(endv - ac)

    stacked = jnp.concatenate([_pack3(dt), _pack3(wdt), _pack3(eac)], axis=0)
    ex = _dot(stacked, e3_ref[...])
    dt_e = ex[0:T]
    wdt_e = ex[T:2 * T]
    eac_e = ex[2 * T:3 * T]
    xdt = (xs * dt_e).astype(BF16)
    xdtw = (xs * wdt_e).astype(BF16)

    ac_rows = ac.T
    rr = lax.broadcasted_iota(I32, (Q, Q), 0)
    cc = lax.broadcasted_iota(I32, (Q, Q), 1)
    causal = rr >= cc
    cblk = lax.shift_right_logical(lax.broadcasted_iota(I32, (Q, 4 * HEAD_DIM), 1), 6)

    y_cols = []
    for g in range(N_BC_GROUPS):
        bg = xact[:, SSD_WIDTH + STATE * g:SSD_WIDTH + STATE * (g + 1)]
        cg = xact[:, SSD_WIDTH + 2 * STATE + STATE * g:SSD_WIDTH + 2 * STATE + STATE * (g + 1)]
        st = st_ref[g]
        y_rows = [[], []]
        for c in range(n_sub):
            r0, r1 = c * Q, (c + 1) * Q
            bb = bg[r0:r1].astype(BF16)
            cbf = cg[r0:r1].astype(BF16)
            cb = lax.dot_general(cbf, bb, (((1,), (1,)), ((), ())), preferred_element_type=F32)
            yoff = _dot(cbf, st.astype(BF16))
            dec = eac_e[r1 - 1:r1, 512 * g:512 * (g + 1)]
            bgt = bg[r0:r1].T.astype(BF16)
            st = st * dec + _dot(bgt, xdtw[r0:r1, 512 * g:512 * (g + 1)])
            for q in range(2):
                ms_list = []
                for r in range(4):
                    hh = 8 * g + 4 * q + r
                    seg = ac[r0:r1, hh:hh + 1] - ac_rows[hh:hh + 1, r0:r1]
                    lh = jnp.exp(jnp.where(causal, seg, -jnp.inf))
                    ms_list.append((cb * lh).astype(BF16))
                lhs = jnp.concatenate(ms_list, axis=1)
                lo = 512 * g + 256 * q
                x4 = xdt[r0:r1, lo:lo + 256]
                rhs = jnp.concatenate(
                    [jnp.where(cblk == r, x4, jnp.zeros_like(x4)) for r in range(4)], axis=0)
                yd = _dot(lhs, rhs)
                y_rows[q].append(yd + eac_e[r0:r1, lo:lo + 256] * yoff[:, 256 * q:256 * (q + 1)])
        st_ref[g] = st
        y_cols += [jnp.concatenate(y_rows[0], axis=0), jnp.concatenate(y_rows[1], axis=0)]
    y = jnp.concatenate(y_cols, axis=1) + dskip_ref[...] * xs

    v = y * _silu(z)
    outs = []
    for g in range(N_BC_GROUPS):
        vg = v[:, 512 * g:512 * (g + 1)]
        msg = jnp.mean(vg * vg, axis=-1, keepdims=True)
        outs.append(vg * lax.rsqrt(msg + EPS))
    ssd_out = (jnp.concatenate(outs, axis=1) * ssdn_ref[...]).astype(BF16)

    u = scc * scv
    sbuf[8:8 + T, :] = u
    sw = scw_ref[...]
    conv = sw[2:3, :] * u + sw[1:2, :] * sbuf[7:7 + T, :] + sw[0:1, :] * sbuf[6:6 + T, :]
    sc = scb * conv
    gs = _dot((sc * sc).astype(BF16), gsum_ref[...])
    rstd = jnp.where(hmask, lax.rsqrt(gs * (1.0 / HEAD_DIM) + EPS), 0.0)
    rstd_e = _dot(_pack3(rstd), e3_ref[...])
    sc_out = (sc * rstd_e * scn_ref[...]).astype(BF16)

    mix = _dot(ssd_out, wout_ref[0:1024, :]) + _dot(sc_out, wout_ref[1024:2048, :])
    h = x + mix
    h_ref[0] = h

    ms2 = jnp.mean(h * h, axis=-1, keepdims=True)
    hn2 = h * lax.rsqrt(ms2 + EPS) * gffn_ref[...]
    hi = hn2.astype(BF16)
    bits = pltpu.bitcast(hi.astype(F32), jnp.uint32)
    for s in range(ROW_TILES):
        pbuf[pl.ds(s * T, T), :] = bits[:, LANES * s:LANES * (s + 1)]
    for s in range(ROW_TILES):
        even = pbuf[pl.ds(s * T, PAIR_ROWS, stride=2), :]
        odd = pbuf[pl.ds(s * T + 1, PAIR_ROWS, stride=2), :]
        hn2_ref[pl.ds(s, PAIR_ROWS, stride=ROW_TILES), :] = jnp.bitwise_or(
            jnp.right_shift(even, jnp.uint32(16)), odd)
    lo_ = (hn2 - hi.astype(F32)).astype(BF16)
    RT = ROUTER_ROWS
    nt_dims = (((1,), (1,)), ((), ()))
    both = lax.dot_general(wr_ref[...], hi, nt_dims, preferred_element_type=F32)
    low = lax.dot_general(wr_ref[0:RT, :], lo_, nt_dims, preferred_element_type=F32)
    logits = both[0:RT] + (low + both[RT:2 * RT])

    ri = lax.broadcasted_iota(I32, (RT, T), 0)
    ri_f = ri.astype(F32)
    neg = -jnp.inf
    big = 1e9
    gl = jnp.where(ri < N_GROUPS, logits, neg)
    gmax = jnp.max(gl, axis=0, keepdims=True)
    gidx = jnp.min(jnp.where(gl == gmax, ri_f, big), axis=0, keepdims=True)
    gsum = jnp.sum(jnp.where(ri < N_GROUPS, jnp.exp(logits - gmax), 0.0), axis=0, keepdims=True)
    gw = 1.0 / gsum
    egrp = lax.shift_right_logical(ri - N_GROUPS, 3).astype(F32)
    in_grp = (ri >= N_GROUPS) & (ri < N_GROUPS + N_EXPERTS) & (egrp == gidx)
    el = jnp.where(in_grp, logits, neg)
    v1 = jnp.max(el, axis=0, keepdims=True)
    i1 = jnp.min(jnp.where(el == v1, ri_f, big), axis=0, keepdims=True)
    el2 = jnp.where(ri_f == i1, neg, el)
    v2 = jnp.max(el2, axis=0, keepdims=True)
    i2 = jnp.min(jnp.where(el2 == v2, ri_f, big), axis=0, keepdims=True)
    p = jnp.exp(v2 - v1)
    s1 = 1.0 / (1.0 + p)
    gate1 = gw * s1
    gate2 = gw * (p * s1)
    r8 = lax.broadcasted_iota(I32, (SUBLANES, T), 0)
    e_ref[...] = jnp.where(r8 == 0, i1 - N_GROUPS, jnp.where(r8 == 1, i2 - N_GROUPS, 0.0)).astype(I32)
    rl = lax.broadcasted_iota(I32, (LANES, T), 0)
    g_ref[...] = jnp.where(rl == 0, gate1, jnp.where(rl == 1, gate2, 0.0)).T


def _mixer_call(x, gmix, wa, wb, wdt, convw, convb, dtb, alog, dskip, ssdn, scw, scn, wout, gffn,
                wr, tri, e3, gsum):
    B, L, D = x.shape
    T = T_MIX
    nt = L // T
    n_tok = B * L

    def const(shape):
        return pl.BlockSpec(shape, lambda b, t: (0,) * len(shape), pipeline_mode=pl.Buffered(1))

    in_specs = [
        pl.BlockSpec((1, T, D), lambda b, t: (b, t, 0)),
        const((1, D)),
        const(wa.shape), const(wb.shape), const(wdt.shape), const(convw.shape), const(convb.shape),
        const(dtb.shape), const(alog.shape), const(dskip.shape), const(ssdn.shape),
        const(scw.shape), const(scn.shape), const(wout.shape), const(gffn.shape),
        const(wr.shape), const(tri.shape), const(e3.shape), const(gsum.shape),
    ]
    out_shape = [
        jax.ShapeDtypeStruct((B, L, D), F32),
        jax.ShapeDtypeStruct((n_tok // 2 * ROW_TILES, LANES), jnp.uint32),
        jax.ShapeDtypeStruct((8, n_tok), I32),
        jax.ShapeDtypeStruct((n_tok, LANES), F32),
    ]
    out_specs = [
        pl.BlockSpec((1, T, D), lambda b, t: (b, t, 0)),
        pl.BlockSpec((PAIR_ROWS * ROW_TILES, LANES), lambda b, t: (b * nt + t, 0)),
        pl.BlockSpec((8, T), lambda b, t: (0, b * nt + t)),
        pl.BlockSpec((T, LANES), lambda b, t: (b * nt + t, 0)),
    ]
    return pl.pallas_call(
        _mixer_kernel,
        grid=(B, nt),
        in_specs=in_specs,
        out_specs=out_specs,
        out_shape=out_shape,
        scratch_shapes=[
            pltpu.VMEM((T + 8, XBC), F32),
            pltpu.VMEM((T + 8, SC_WIDTH), F32),
            pltpu.VMEM((N_BC_GROUPS, STATE, 512), F32),
            pltpu.VMEM((ROW_TILES * T, LANES), jnp.uint32),
        ],
        compiler_params=pltpu.CompilerParams(
            dimension_semantics=("arbitrary", "arbitrary"),
            vmem_limit_bytes=VMEM_LIMIT),
        name="mixer",
    )(x, gmix, wa, wb, wdt, convw, convb, dtb, alog, dskip, ssdn, scw, scn, wout, gffn, wr,
      tri, e3, gsum)


def _moe_kernel(bstart_ref, nblk_ref, pk_ref, hn2p_ref, wg_ref, wu_ref, wd_ref,
                y2_ref, hn2v, xbuf, ybuf, wgb, wub, wdb, idx_ref, lsem, ssem, isem, *, n_tok):
    R = R_BLK
    slab = R * ROW_TILES
    e = pl.program_id(0)
    g0 = bstart_ref[e]
    nb_e = nblk_ref[e]

    def idx_word(sl, field, j):
        return idx_ref[sl, 2 * field + j // LANES, j % LANES]

    def idx_copy(g, sl):
        return pltpu.make_async_copy(pk_ref.at[g], idx_ref.at[sl], isem.at[sl])

    def gather_rows(sl, lo, hi):
        for j in range(lo, hi):
            row = pl.multiple_of(idx_word(sl, 0, j), ROW_TILES)
            sh = idx_word(sl, 1, j).astype(jnp.uint32)
            w = hn2v[pl.ds(row, ROW_TILES), :]
            w = jnp.left_shift(jnp.right_shift(w, sh), jnp.uint32(16))
            xbuf[pl.ds(j * ROW_TILES, ROW_TILES), :] = pltpu.bitcast(w, F32)

    def scatter_start(sl, lo, hi):
        for j in range(lo, hi):
            pltpu.make_async_copy(
                ybuf.at[pl.ds(sl * slab + j * ROW_TILES, ROW_TILES), :],
                y2_ref.at[idx_word(sl, 2, j)],
                ssem.at[sl]).start(priority=j % 2)

    def slab_wait(sl):
        view = ybuf.at[pl.ds(sl * slab, slab), :]
        pltpu.make_async_copy(view, view, ssem.at[sl]).wait()

    n_idx_blocks = pk_ref.shape[0]

    @pl.when(e == 0)
    def _():
        load = pltpu.make_async_copy(hn2p_ref, hn2v, lsem.at[0])
        load.start()
        idx_copy(0, 0).start()
        prev = idx_copy(n_idx_blocks - 1, N_SLOTS - 1)
        prev.start()
        ybuf[...] = jnp.zeros(ybuf.shape, F32)
        for sl, base in ((0, 2 * n_tok), (1, 2 * n_tok + 2 * R)):
            for j in range(R):
                pltpu.make_async_copy(
                    ybuf.at[pl.ds(sl * slab + j * ROW_TILES, ROW_TILES), :],
                    y2_ref.at[base + j],
                    ssem.at[sl]).start(priority=j % 2)
        prev.wait()
        load.wait()

    @pl.when(nb_e > 0)
    def _():
        wgb[...] = wg_ref[0].astype(BF16)
        wub[...] = wu_ref[0].astype(BF16)
        wdb[...] = wd_ref[0].astype(BF16)

    c1, c2 = R // 3, 2 * (R // 3)

    def run_block(g, sl):
        prv = (sl + N_SLOTS - 1) % N_SLOTS
        nxt = (sl + 1) % N_SLOTS
        idx_copy(g, sl).wait()
        idx_copy(g + 1, nxt).start()
        slab_wait(sl)
        gather_rows(sl, 0, R)
        x = jnp.concatenate(
            [xbuf[pl.ds(s, R, stride=ROW_TILES), :] for s in range(ROW_TILES)], axis=1).astype(BF16)
        scatter_start(prv, 0, c1)
        gg = _dot(x, wgb[...])
        scatter_start(prv, c1, c2)
        uu = _dot(x, wub[...])
        act = (_silu(gg) * uu).astype(BF16)
        scatter_start(prv, c2, R)
        yy = _dot(act, wdb[...])
        for s in range(ROW_TILES):
            ybuf[pl.ds(sl * slab + s, R, stride=ROW_TILES), :] = yy[:, LANES * s:LANES * (s + 1)]

    def block(i, carry):
        g = g0 + i
        sl = lax.rem(g, N_SLOTS)
        for k in range(N_SLOTS):
            @pl.when(sl == k)
            def _(k=k):
                run_block(g, k)
        return carry

    lax.fori_loop(0, nb_e, block, 0)

    @pl.when(e == pl.num_programs(0) - 1)
    def _():
        g_end = g0 + nb_e
        sl_end = lax.rem(g_end, N_SLOTS)
        for k in range(N_SLOTS):
            @pl.when(sl_end == k)
            def _(k=k):
                scatter_start((k + N_SLOTS - 1) % N_SLOTS, 0, R)
                idx_copy(g_end, k).wait()
        for k in range(N_SLOTS):
            slab_wait(k)


def _moe_call(bstart, nblk, pk, hn2p, wg, wu, wd, n_tok):
    R = R_BLK
    slab = R * ROW_TILES
    grid_spec = pltpu.PrefetchScalarGridSpec(
        num_scalar_prefetch=2,
        grid=(N_EXPERTS,),
        in_specs=[
            pl.BlockSpec(memory_space=pl.ANY),
            pl.BlockSpec(memory_space=pl.ANY),
            pl.BlockSpec((1, D_MODEL, D_FF), lambda e, bs, nb: (e, 0, 0)),
            pl.BlockSpec((1, D_MODEL, D_FF), lambda e, bs, nb: (e, 0, 0)),
            pl.BlockSpec((1, D_FF, D_MODEL), lambda e, bs, nb: (e, 0, 0)),
        ],
        out_specs=pl.BlockSpec(memory_space=pl.ANY),
        scratch_shapes=[
            pltpu.VMEM(hn2p.shape, jnp.uint32),
            pltpu.VMEM((slab, LANES), F32),
            pltpu.VMEM((N_SLOTS * slab, LANES), F32),
            pltpu.VMEM((D_MODEL, D_FF), BF16),
            pltpu.VMEM((D_MODEL, D_FF), BF16),
            pltpu.VMEM((D_FF, D_MODEL), BF16),
            pltpu.SMEM((N_SLOTS, SUBLANES, LANES), I32),
            pltpu.SemaphoreType.DMA((1,)),
            pltpu.SemaphoreType.DMA((N_SLOTS,)),
            pltpu.SemaphoreType.DMA((N_SLOTS,)),
        ],
    )
    return pl.pallas_call(
        functools.partial(_moe_kernel, n_tok=n_tok),
        grid_spec=grid_spec,
        out_shape=jax.ShapeDtypeStruct((2 * n_tok + N_SLOTS * R, ROW_TILES, LANES), F32),
        compiler_params=pltpu.CompilerParams(
            dimension_semantics=("arbitrary",),
            vmem_limit_bytes=VMEM_LIMIT),
        name="moe",
    )(bstart, nblk, pk, hn2p, wg, wu, wd)


def _combine_kernel(h_ref, y0_ref, y1_ref, g_ref, fn_ref, o_ref):
    T = T_CMB
    h = h_ref[...]
    y0 = jnp.concatenate([y0_ref[pl.ds(s, T, stride=ROW_TILES), :] for s in range(ROW_TILES)], axis=1)
    y1 = jnp.concatenate([y1_ref[pl.ds(s, T, stride=ROW_TILES), :] for s in range(ROW_TILES)], axis=1)
    g = g_ref[...]
    v = h + (y0 * g[:, 0:1] + y1 * g[:, 1:2])
    ms = jnp.mean(v * v, axis=-1, keepdims=True)
    o_ref[...] = v * lax.rsqrt(ms + EPS) * fn_ref[...]


def _combine_call(h2d, y2, gates, fnorm):
    n_tok, D = h2d.shape
    T = T_CMB
    nt = n_tok // T
    return pl.pallas_call(
        _combine_kernel,
        grid=(nt,),
        in_specs=[
            pl.BlockSpec((T, D), lambda i: (i, 0)),
            pl.BlockSpec((T * ROW_TILES, LANES), lambda i: (i, 0)),
            pl.BlockSpec((T * ROW_TILES, LANES), lambda i: (nt + i, 0)),
            pl.BlockSpec((T, LANES), lambda i: (i, 0)),
            pl.BlockSpec((1, D), lambda i: (0, 0)),
        ],
        out_specs=pl.BlockSpec((T, D), lambda i: (i, 0)),
        out_shape=jax.ShapeDtypeStruct((n_tok, D), F32),
        compiler_params=pltpu.CompilerParams(dimension_semantics=("arbitrary",)),
        name="combine",
    )(h2d, y2, y2, gates, fnorm)


def _plan_kernel(e_ref, ux_ref, ones_ref, lx_ref, dest_ref, bstart_ref, nblk_ref):
    rows = e_ref.shape[0]
    ev = e_ref[...]
    lane8 = lax.broadcasted_iota(I32, (SUBLANES, LANES), 1)
    dest = jnp.zeros((rows, LANES), F32)
    bstart = jnp.zeros((SUBLANES, LANES), I32)
    nblk = jnp.zeros((SUBLANES, LANES), I32)
    pstart = jnp.zeros((1, LANES), F32)
    for e in range(N_EXPERTS):
        ohb = ev == e
        oh = jnp.where(ohb, 1.0, 0.0).astype(BF16)
        within = _dot(oh, ux_ref[...])
        rtot = _dot(oh, ones_ref[...])
        rpre = _dot(lx_ref[...], rtot.astype(BF16))
        cnt = rpre[rows - 1:rows, :] + rtot[rows - 1:rows, :]
        dest = dest + jnp.where(ohb, within + rpre + pstart, 0.0)
        cnt_i = cnt.astype(I32)
        nb_e = lax.shift_right_logical(cnt_i + (R_BLK - 1), R_SHIFT)
        bstart = jnp.where(lane8 == e, lax.shift_right_logical(pstart.astype(I32), R_SHIFT), bstart)
        nblk = jnp.where(lane8 == e, nb_e, nblk)
        pstart = pstart + lax.shift_left(nb_e, R_SHIFT).astype(F32)
    dest_ref[...] = dest.astype(I32)
    bstart_ref[...] = bstart
    nblk_ref[...] = nblk


def _invert_kernel(dest_ref, init_ref, inv_ref):
    n_asg = dest_ref.shape[0]
    pltpu.sync_copy(init_ref, inv_ref)

    def put(a, c):
        inv_ref[dest_ref[a]] = a
        return c

    lax.fori_loop(0, n_asg, put, 0, unroll=16)


def _index_tiles_kernel(inv_ref, pk_ref, *, n_tok):
    nb = pk_ref.shape[0] // SUBLANES
    per_blk = R_BLK // LANES
    assert per_blk == 2
    pk_ref[...] = jnp.zeros(pk_ref.shape, I32)
    for half in range(per_blk):
        a = inv_ref[pl.ds(half, nb, stride=per_blk), :]
        t = jnp.bitwise_and(a, n_tok - 1)
        pk_ref[pl.ds(half, nb, stride=SUBLANES), :] = lax.shift_left(lax.shift_right_logical(t, 1), 3)
        pk_ref[pl.ds(2 + half, nb, stride=SUBLANES), :] = lax.shift_left(jnp.bitwise_and(a, 1), 4)
        pk_ref[pl.ds(4 + half, nb, stride=SUBLANES), :] = a


def _plan(e_rows, n_tok, nb):
    R = R_BLK
    n_asg = 2 * n_tok
    rows = n_asg // LANES
    emat = e_rows[0:2].reshape(rows, LANES)
    li = jnp.arange(LANES)
    ux = (li[:, None] < li[None, :]).astype(BF16)
    ones = jnp.ones((LANES, LANES), BF16)
    ri = jnp.arange(rows)
    lx = (ri[:, None] > ri[None, :]).astype(BF16)
    dest, bstart, nblk = pl.pallas_call(
        _plan_kernel,
        out_shape=[jax.ShapeDtypeStruct((rows, LANES), I32),
                   jax.ShapeDtypeStruct((SUBLANES, LANES), I32),
                   jax.ShapeDtypeStruct((SUBLANES, LANES), I32)],
        name="plan",
    )(emat, ux, ones, lx)
    pos = jnp.arange(nb * R, dtype=I32)
    init = n_asg + (pos & (R - 1)) + jnp.where(pos >= (nb - 1) * R, R, 0)
    inv = pl.pallas_call(
        _invert_kernel,
        in_specs=[pl.BlockSpec(memory_space=pltpu.SMEM), pl.BlockSpec(memory_space=pl.ANY)],
        out_specs=pl.BlockSpec(memory_space=pltpu.SMEM),
        out_shape=jax.ShapeDtypeStruct((nb * R,), I32),
        name="invert",
    )(dest.reshape(n_asg), init)
    pk = pl.pallas_call(
        functools.partial(_index_tiles_kernel, n_tok=n_tok),
        out_shape=jax.ShapeDtypeStruct((nb * SUBLANES, LANES), I32),
        name="index_tiles",
    )(inv.reshape(nb * R // LANES, LANES))
    return bstart[0, 0:N_EXPERTS], nblk[0, 0:N_EXPERTS], pk.reshape(nb, SUBLANES, LANES)


def kernel(x, norm_mix, w_in, ssd_conv_w, ssd_conv_b, dt_bias, a_log, d_skip, ssd_norm, sc_conv_w,
           sc_norm, w_out, norm_ffn, w_router_group, w_router_expert, w_gate, w_up, w_down, final_norm):
    B, L, D = x.shape
    n_tok = B * L
    depth = norm_mix.shape[0]
    assert depth == 1 and D == D_MODEL and (n_tok & (n_tok - 1)) == 0
    nb = -(-((2 * n_tok) // R_BLK + N_EXPERTS + 1) // SUBLANES) * SUBLANES

    o1 = SSD_WIDTH
    o2 = o1 + XBC
    o3 = o2 + N_HEADS
    wi = w_in.reshape(D, -1)
    wa = wi[:, 0:o2].astype(BF16)
    wb = wi[:, o3:].astype(BF16)
    wdt = jnp.pad(wi[:, o2:o3], ((0, 0), (0, LANES - N_HEADS))).astype(BF16)
    pad_h = (0, LANES - N_HEADS)
    dtb = jnp.pad(dt_bias[0], pad_h).reshape(1, LANES)
    alog = jnp.pad(a_log[0], pad_h).reshape(1, LANES)
    dskip = jnp.repeat(d_skip[0], HEAD_DIM).reshape(1, SSD_WIDTH)

    wre = jnp.transpose(w_router_expert[0], (1, 0, 2)).reshape(D, N_EXPERTS)
    wrt = jnp.pad(jnp.concatenate([w_router_group[0], wre], axis=1).T,
                  ((0, ROUTER_ROWS - N_GROUPS - N_EXPERTS), (0, 0)))
    wrt_hi = wrt.astype(BF16)
    wr = jnp.concatenate([wrt_hi, (wrt - wrt_hi.astype(F32)).astype(BF16)], axis=0)

    ri = jnp.arange(T_MIX)
    tri = (ri[:, None] >= ri[None, :]).astype(BF16)
    er = jnp.arange(LANES)
    ec = jnp.arange(SSD_WIDTH)
    e3 = ((er[:, None] < 48) & ((er[:, None] % 16) == (ec[None, :] // HEAD_DIM))).astype(BF16)
    gsum = ((ec[:, None] // HEAD_DIM) == er[None, :]).astype(BF16)

    h, hn2, e_rows, gates = _mixer_call(
        x, norm_mix[0].reshape(1, D), wa, wb, wdt, ssd_conv_w[0], ssd_conv_b[0].reshape(1, XBC), dtb, alog,
        dskip, ssd_norm[0].reshape(1, SSD_WIDTH), sc_conv_w[0], sc_norm[0].reshape(1, SC_WIDTH),
        w_out.reshape(-1, D).astype(BF16), norm_ffn[0].reshape(1, D), wr, tri, e3, gsum)

    bstart, nblk, pk = _plan(e_rows, n_tok, nb)
    y2 = _moe_call(bstart, nblk, pk, hn2, w_gate.reshape(N_EXPERTS, D, D_FF),
                   w_up.reshape(N_EXPERTS, D, D_FF), w_down.reshape(N_EXPERTS, D_FF, D), n_tok)
    out = _combine_call(h.reshape(n_tok, D), y2.reshape(-1, LANES), gates, final_norm.reshape(1, D))
    return out.reshape(B, L, D)
```

```python
import functools

import jax
import jax.numpy as jnp
from jax import lax
from jax.experimental import pallas as pl
from jax.experimental.pallas import tpu as pltpu

F32 = jnp.float32
BF16 = jnp.bfloat16
I32 = jnp.int32

EPS = 1e-6
D_MODEL = 1024
N_HEADS = 16
HEAD_DIM = 64
N_BC_GROUPS = 2
STATE = 128
SSD_WIDTH = 1024
XBC = SSD_WIDTH + 2 * N_BC_GROUPS * STATE
SC_WIDTH = 1024
SC_GROUPS = 16
N_GROUPS = 4
EPG = 8
N_EXPERTS = 32
D_FF = 512

LANES = 128
SUBLANES = 8
ROW_TILES = D_MODEL // LANES
Y_TILES = ROW_TILES // 2

T_MIX = 256
Q_SSD = 128
PAIR_ROWS = T_MIX // 2
R_BLK = 256
R_SHIFT = 8
ROUTER_ROWS = 48
N_SLOTS = 3
T_CMB = 1024

VMEM_LIMIT = 56 * 1024 * 1024


def _dot(a, b):
    return jnp.dot(a, b, preferred_element_type=F32)


def _split3(v):
    p1 = v.astype(BF16).astype(F32)
    r1 = v - p1
    p2 = r1.astype(BF16).astype(F32)
    p3 = (r1 - p2).astype(BF16).astype(F32)
    return p1, p2, p3


def _pack3(v):
    p1, p2, p3 = _split3(v)
    return (p1 + pltpu.roll(p2, 16, 1) + pltpu.roll(p3, 32, 1)).astype(BF16)


def _silu(v):
    return v * jax.nn.sigmoid(v)


def _mixer_kernel(x_ref, gmix_ref, wa_ref, wb_ref, wdt_ref, convw_ref, convb_ref, dtb_ref, alog_ref,
                  dskip_ref, ssdn_ref, scw_ref, scn_ref, wout_ref, gffn_ref, wr_ref,
                  tri_ref, e3_ref, gsum_ref,
                  h_ref, hn2_ref, e_ref, g_ref,
                  cbuf, sbuf, st_ref, pbuf):
    T = T_MIX
    t = pl.program_id(1)

    @pl.when(t == 0)
    def _():
        cbuf[0:8, :] = jnp.zeros((8, XBC), F32)
        sbuf[0:8, :] = jnp.zeros((8, SC_WIDTH), F32)
        st_ref[...] = jnp.zeros(st_ref.shape, F32)

    @pl.when(t > 0)
    def _():
        cbuf[0:8, :] = cbuf[T:T + 8, :]
        sbuf[0:8, :] = sbuf[T:T + 8, :]

    x = x_ref[0]
    ms = jnp.mean(x * x, axis=-1, keepdims=True)
    hn = (x * lax.rsqrt(ms + EPS) * gmix_ref[...]).astype(BF16)

    z = _dot(hn, wa_ref[:, 0:1024])
    xbc = _dot(hn, wa_ref[:, 1024:2560])
    scb = _dot(hn, wb_ref[:, 0:1024])
    scc = _dot(hn, wb_ref[:, 1024:2048])
    scv = _dot(hn, wb_ref[:, 2048:3072])
    dt_raw = _dot(hn, wdt_ref[...])

    cbuf[8:8 + T, :] = xbc
    cw = convw_ref[...]
    acc = convb_ref[...] + cw[3:4, :] * xbc
    for k in range(3):
        acc = acc + cw[k:k + 1, :] * cbuf[5 + k:5 + k + T, :]
    xact = _silu(acc)
    xs = xact[:, 0:SSD_WIDTH]

    lane = lax.broadcasted_iota(I32, (1, LANES), 1)
    hmask = lane < N_HEADS
    a = jnp.where(hmask, -jnp.exp(alog_ref[...]), 0.0)
    dtv = dt_raw + dtb_ref[...]
    dt = jnp.where(hmask, jnp.maximum(dtv, 0.0) + jnp.log1p(jnp.exp(-jnp.abs(dtv))), 0.0)
    adt = dt * a
    c3 = _dot(tri_ref[...], _pack3(adt))
    ac = jnp.where(hmask, c3 + pltpu.roll(c3, LANES - 16, 1) + pltpu.roll(c3, LANES - 32, 1), 0.0)
    Q = Q_SSD
    n_sub = T // Q
    ends = [ac[(c + 1) * Q - 1:(c + 1) * Q, :] for c in range(n_sub)]
    base = jnp.concatenate([jnp.broadcast_to(ends[c - 1] if c else jnp.zeros_like(ends[0]), (Q, LANES))
                            for c in range(n_sub)], axis=0)
    endv = jnp.concatenate([jnp.broadcast_to(ends[c], (Q, LANES)) for c in range(n_sub)], axis=0)
    eac = jnp.where(hmask, jnp.exp(ac - base), 0.0)
    wdt = dt * jnp.exp(endv - ac)

    stacked = jnp.concatenate([_pack3(dt), _pack3(wdt), _pack3(eac)], axis=0)
    ex = _dot(stacked, e3_ref[...])
    dt_e = ex[0:T]
    wdt_e = ex[T:2 * T]
    eac_e = ex[2 * T:3 * T]
    xdt = (xs * dt_e).astype(BF16)
    xdtw = (xs * wdt_e).astype(BF16)

    ac_rows = ac.T
    rr = lax.broadcasted_iota(I32, (Q, Q), 0)
    cc = lax.broadcasted_iota(I32, (Q, Q), 1)
    causal = rr >= cc
    cblk = lax.shift_right_logical(lax.broadcasted_iota(I32, (Q, 4 * HEAD_DIM), 1), 6)

    y_cols = []
    for g in range(N_BC_GROUPS):
        bg = xact[:, SSD_WIDTH + STATE * g:SSD_WIDTH + STATE * (g + 1)]
        cg = xact[:, SSD_WIDTH + 2 * STATE + STATE * g:SSD_WIDTH + 2 * STATE + STATE * (g + 1)]
        st = st_ref[g]
        y_rows = [[], []]
        for c in range(n_sub):
            r0, r1 = c * Q, (c + 1) * Q
            bb = bg[r0:r1].astype(BF16)
            cbf = cg[r0:r1].astype(BF16)
            cb = lax.dot_general(cbf, bb, (((1,), (1,)), ((), ())), preferred_element_type=F32)
            yoff = _dot(cbf, st.astype(BF16))
            dec = eac_e[r1 - 1:r1, 512 * g:512 * (g + 1)]
            bgt = bg[r0:r1].T.astype(BF16)
            st = st * dec + _dot(bgt, xdtw[r0:r1, 512 * g:512 * (g + 1)])
            for q in range(2):
                ms_list = []
                for r in range(4):
                    hh = 8 * g + 4 * q + r
                    seg = ac[r0:r1, hh:hh + 1] - ac_rows[hh:hh + 1, r0:r1]
                    lh = jnp.exp(jnp.where(causal, seg, -jnp.inf))
                    ms_list.append((cb * lh).astype(BF16))
                lhs = jnp.concatenate(ms_list, axis=1)
                lo = 512 * g + 256 * q
                x4 = xdt[r0:r1, lo:lo + 256]
                rhs = jnp.concatenate(
                    [jnp.where(cblk == r, x4, jnp.zeros_like(x4)) for r in range(4)], axis=0)
                yd = _dot(lhs, rhs)
                y_rows[q].append(yd + eac_e[r0:r1, lo:lo + 256] * yoff[:, 256 * q:256 * (q + 1)])
        st_ref[g] = st
        y_cols += [jnp.concatenate(y_rows[0], axis=0), jnp.concatenate(y_rows[1], axis=0)]
    y = jnp.concatenate(y_cols, axis=1) + dskip_ref[...] * xs

    v = y * _silu(z)
    outs = []
    for g in range(N_BC_GROUPS):
        vg = v[:, 512 * g:512 * (g + 1)]
        msg = jnp.mean(vg * vg, axis=-1, keepdims=True)
        outs.append(vg * lax.rsqrt(msg + EPS))
    ssd_out = (jnp.concatenate(outs, axis=1) * ssdn_ref[...]).astype(BF16)

    u = scc * scv
    sbuf[8:8 + T, :] = u
    sw = scw_ref[...]
    conv = sw[2:3, :] * u + sw[1:2, :] * sbuf[7:7 + T, :] + sw[0:1, :] * sbuf[6:6 + T, :]
    sc = scb * conv
    gs = _dot((sc * sc).astype(BF16), gsum_ref[...])
    rstd = jnp.where(hmask, lax.rsqrt(gs * (1.0 / HEAD_DIM) + EPS), 0.0)
    rstd_e = _dot(_pack3(rstd), e3_ref[...])
    sc_out = (sc * rstd_e * scn_ref[...]).astype(BF16)

    mix = _dot(ssd_out, wout_ref[0:1024, :]) + _dot(sc_out, wout_ref[1024:2048, :])
    h = x + mix
    h_ref[0] = h

    ms2 = jnp.mean(h * h, axis=-1, keepdims=True)
    hn2 = h * lax.rsqrt(ms2 + EPS) * gffn_ref[...]
    hi = hn2.astype(BF16)
    bits = pltpu.bitcast(hi.astype(F32), jnp.uint32)
    for s in range(ROW_TILES):
        pbuf[pl.ds(s * T, T), :] = bits[:, LANES * s:LANES * (s + 1)]
    for s in range(ROW_TILES):
        even = pbuf[pl.ds(s * T, PAIR_ROWS, stride=2), :]
        odd = pbuf[pl.ds(s * T + 1, PAIR_ROWS, stride=2), :]
        hn2_ref[pl.ds(s, PAIR_ROWS, stride=ROW_TILES), :] = jnp.bitwise_or(
            jnp.right_shift(even, jnp.uint32(16)), odd)
    lo_ = (hn2 - hi.astype(F32)).astype(BF16)
    RT = ROUTER_ROWS
    nt_dims = (((1,), (1,)), ((), ()))
    both = lax.dot_general(wr_ref[...], hi, nt_dims, preferred_element_type=F32)
    low = lax.dot_general(wr_ref[0:RT, :], lo_, nt_dims, preferred_element_type=F32)
    logits = both[0:RT] + (low + both[RT:2 * RT])

    ri = lax.broadcasted_iota(I32, (RT, T), 0)
    ri_f = ri.astype(F32)
    neg = -jnp.inf
    big = 1e9
    gl = jnp.where(ri < N_GROUPS, logits, neg)
    gmax = jnp.max(gl, axis=0, keepdims=True)
    gidx = jnp.min(jnp.where(gl == gmax, ri_f, big), axis=0, keepdims=True)
    gsum = jnp.sum(jnp.where(ri < N_GROUPS, jnp.exp(logits - gmax), 0.0), axis=0, keepdims=True)
    gw = 1.0 / gsum
    egrp = lax.shift_right_logical(ri - N_GROUPS, 3).astype(F32)
    in_grp = (ri >= N_GROUPS) & (ri < N_GROUPS + N_EXPERTS) & (egrp == gidx)
    el = jnp.where(in_grp, logits, neg)
    v1 = jnp.max(el, axis=0, keepdims=True)
    i1 = jnp.min(jnp.where(el == v1, ri_f, big), axis=0, keepdims=True)
    el2 = jnp.where(ri_f == i1, neg, el)
    v2 = jnp.max(el2, axis=0, keepdims=True)
    i2 = jnp.min(jnp.where(el2 == v2, ri_f, big), axis=0, keepdims=True)
    p = jnp.exp(v2 - v1)
    s1 = 1.0 / (1.0 + p)
    gate1 = gw * s1
    gate2 = gw * (p * s1)
    r8 = lax.broadcasted_iota(I32, (SUBLANES, T), 0)
    e_ref[...] = jnp.where(r8 == 0, i1 - N_GROUPS, jnp.where(r8 == 1, i2 - N_GROUPS, 0.0)).astype(I32)
    rl = lax.broadcasted_iota(I32, (LANES, T), 0)
    g_ref[...] = jnp.where(rl == 0, gate1, jnp.where(rl == 1, gate2, 0.0)).T


def _mixer_call(x, gmix, wa, wb, wdt, convw, convb, dtb, alog, dskip, ssdn, scw, scn, wout, gffn,
                wr, tri, e3, gsum):
    B, L, D = x.shape
    T = T_MIX
    nt = L // T
    n_tok = B * L

    def const(shape):
        return pl.BlockSpec(shape, lambda b, t: (0,) * len(shape), pipeline_mode=pl.Buffered(1))

    in_specs = [
        pl.BlockSpec((1, T, D), lambda b, t: (b, t, 0)),
        const((1, D)),
        const(wa.shape), const(wb.shape), const(wdt.shape), const(convw.shape), const(convb.shape),
        const(dtb.shape), const(alog.shape), const(dskip.shape), const(ssdn.shape),
        const(scw.shape), const(scn.shape), const(wout.shape), const(gffn.shape),
        const(wr.shape), const(tri.shape), const(e3.shape), const(gsum.shape),
    ]
    out_shape = [
        jax.ShapeDtypeStruct((B, L, D), F32),
        jax.ShapeDtypeStruct((n_tok // 2 * ROW_TILES, LANES), jnp.uint32),
        jax.ShapeDtypeStruct((8, n_tok), I32),
        jax.ShapeDtypeStruct((n_tok, LANES), F32),
    ]
    out_specs = [
        pl.BlockSpec((1, T, D), lambda b, t: (b, t, 0)),
        pl.BlockSpec((PAIR_ROWS * ROW_TILES, LANES), lambda b, t: (b * nt + t, 0)),
        pl.BlockSpec((8, T), lambda b, t: (0, b * nt + t)),
        pl.BlockSpec((T, LANES), lambda b, t: (b * nt + t, 0)),
    ]
    return pl.pallas_call(
        _mixer_kernel,
        grid=(B, nt),
        in_specs=in_specs,
        out_specs=out_specs,
        out_shape=out_shape,
        scratch_shapes=[
            pltpu.VMEM((T + 8, XBC), F32),
            pltpu.VMEM((T + 8, SC_WIDTH), F32),
            pltpu.VMEM((N_BC_GROUPS, STATE, 512), F32),
            pltpu.VMEM((ROW_TILES * T, LANES), jnp.uint32),
        ],
        compiler_params=pltpu.CompilerParams(
            dimension_semantics=("arbitrary", "arbitrary"),
            vmem_limit_bytes=VMEM_LIMIT),
        name="mixer",
    )(x, gmix, wa, wb, wdt, convw, convb, dtb, alog, dskip, ssdn, scw, scn, wout, gffn, wr,
      tri, e3, gsum)


def _moe_kernel(bstart_ref, nblk_ref, pk_ref, hn2p_ref, wg_ref, wu_ref, wd_ref,
                y2_ref, hn2v, xbuf, ybuf, wgb, wub, wdb, idx_ref, lsem, ssem, isem, *, n_tok):
    R = R_BLK
    slab = R * ROW_TILES
    e = pl.program_id(0)
    g0 = bstart_ref[e]
    nb_e = nblk_ref[e]

    def idx_word(sl, field, j):
        return idx_ref[sl, 2 * field + j // LANES, j % LANES]

    def idx_copy(g, sl):
        return pltpu.make_async_copy(pk_ref.at[g], idx_ref.at[sl], isem.at[sl])

    def gather_rows(sl, lo, hi):
        for j in range(lo, hi):
            row = pl.multiple_of(idx_word(sl, 0, j), ROW_TILES)
            sh = idx_word(sl, 1, j).astype(jnp.uint32)
            w = hn2v[pl.ds(row, ROW_TILES), :]
            w = jnp.left_shift(jnp.right_shift(w, sh), jnp.uint32(16))
            xbuf[pl.ds(j * ROW_TILES, ROW_TILES), :] = pltpu.bitcast(w, F32)

    yslab = R * Y_TILES

    def y_copy(sl, j, dst_row):
        return pltpu.make_async_copy(
            ybuf.at[pl.ds(sl * yslab + j * Y_TILES, Y_TILES), :],
            y2_ref.at[pl.ds(pl.multiple_of(dst_row, Y_TILES), Y_TILES), :],
            ssem.at[sl])

    def scatter_start(sl, lo, hi):
        for j in range(lo, hi):
            y_copy(sl, j, idx_word(sl, 2, j)).start(priority=j % 2)

    def slab_wait(sl):
        view = ybuf.at[pl.ds(sl * yslab, yslab), :]
        pltpu.make_async_copy(view, view, ssem.at[sl]).wait()

    n_idx_blocks = pk_ref.shape[0]

    @pl.when(e == 0)
    def _():
        load = pltpu.make_async_copy(hn2p_ref, hn2v, lsem.at[0])
        load.start()
        idx_copy(0, 0).start()
        prev = idx_copy(n_idx_blocks - 1, N_SLOTS - 1)
        prev.start()
        ybuf[...] = jnp.zeros(ybuf.shape, ybuf.dtype)
        for sl, base in ((0, 2 * n_tok), (1, 2 * n_tok + 2 * R)):
            for j in range(R):
                y_copy(sl, j, (base + j) * Y_TILES).start(priority=j % 2)
        prev.wait()
        load.wait()

    @pl.when(nb_e > 0)
    def _():
        wgb[...] = wg_ref[0].astype(BF16)
        wub[...] = wu_ref[0].astype(BF16)
        wdb[...] = wd_ref[0].astype(BF16)

    c1, c2 = R // 3, 2 * (R // 3)

    def run_block(g, sl):
        prv = (sl + N_SLOTS - 1) % N_SLOTS
        nxt = (sl + 1) % N_SLOTS
        idx_copy(g, sl).wait()
        idx_copy(g + 1, nxt).start()
        slab_wait(sl)
        gather_rows(sl, 0, R)
        x = jnp.concatenate(
            [xbuf[pl.ds(s, R, stride=ROW_TILES), :] for s in range(ROW_TILES)], axis=1).astype(BF16)
        scatter_start(prv, 0, c1)
        gg = _dot(x, wgb[...])
        scatter_start(prv, c1, c2)
        uu = _dot(x, wub[...])
        act = (_silu(gg) * uu).astype(BF16)
        scatter_start(prv, c2, R)
        yy = _dot(act, wdb[...])
        for s in range(Y_TILES):
            lo_w = pltpu.bitcast(yy[:, 2 * LANES * s:2 * LANES * s + LANES].astype(BF16).astype(F32), jnp.uint32)
            hi_w = pltpu.bitcast(yy[:, 2 * LANES * s + LANES:2 * LANES * (s + 1)].astype(BF16).astype(F32),
                                 jnp.uint32)
            ybuf[pl.ds(sl * yslab + s, R, stride=Y_TILES), :] = jnp.bitwise_or(
                jnp.right_shift(lo_w, jnp.uint32(16)), hi_w)

    def block(i, carry):
        g = g0 + i
        sl = lax.rem(g, N_SLOTS)
        for k in range(N_SLOTS):
            @pl.when(sl == k)
            def _(k=k):
                run_block(g, k)
        return carry

    lax.fori_loop(0, nb_e, block, 0)

    @pl.when(e == pl.num_programs(0) - 1)
    def _():
        g_end = g0 + nb_e
        sl_end = lax.rem(g_end, N_SLOTS)
        for k in range(N_SLOTS):
            @pl.when(sl_end == k)
            def _(k=k):
                scatter_start((k + N_SLOTS - 1) % N_SLOTS, 0, R)
                idx_copy(g_end, k).wait()
        for k in range(N_SLOTS):
            slab_wait(k)


def _moe_call(bstart, nblk, pk, hn2p, wg, wu, wd, n_tok):
    R = R_BLK
    slab = R * ROW_TILES
    grid_spec = pltpu.PrefetchScalarGridSpec(
        num_scalar_prefetch=2,
        grid=(N_EXPERTS,),
        in_specs=[
            pl.BlockSpec(memory_space=pl.ANY),
            pl.BlockSpec(memory_space=pl.ANY),
            pl.BlockSpec((1, D_MODEL, D_FF), lambda e, bs, nb: (e, 0, 0)),
            pl.BlockSpec((1, D_MODEL, D_FF), lambda e, bs, nb: (e, 0, 0)),
            pl.BlockSpec((1, D_FF, D_MODEL), lambda e, bs, nb: (e, 0, 0)),
        ],
        out_specs=pl.BlockSpec(memory_space=pl.ANY),
        scratch_shapes=[
            pltpu.VMEM(hn2p.shape, jnp.uint32),
            pltpu.VMEM((slab, LANES), F32),
            pltpu.VMEM((N_SLOTS * R * Y_TILES, LANES), jnp.uint32),
            pltpu.VMEM((D_MODEL, D_FF), BF16),
            pltpu.VMEM((D_MODEL, D_FF), BF16),
            pltpu.VMEM((D_FF, D_MODEL), BF16),
            pltpu.SMEM((N_SLOTS, SUBLANES, LANES), I32),
            pltpu.SemaphoreType.DMA((1,)),
            pltpu.SemaphoreType.DMA((N_SLOTS,)),
            pltpu.SemaphoreType.DMA((N_SLOTS,)),
        ],
    )
    return pl.pallas_call(
        functools.partial(_moe_kernel, n_tok=n_tok),
        grid_spec=grid_spec,
        out_shape=jax.ShapeDtypeStruct(((2 * n_tok + N_SLOTS * R) * Y_TILES, LANES), jnp.uint32),
        compiler_params=pltpu.CompilerParams(
            dimension_semantics=("arbitrary",),
            vmem_limit_bytes=VMEM_LIMIT),
        name="moe",
    )(bstart, nblk, pk, hn2p, wg, wu, wd)


def _unpack_rows(y_ref, rows):
    parts = []
    for s in range(Y_TILES):
        w = y_ref[pl.ds(s, rows, stride=Y_TILES), :]
        parts.append(pltpu.bitcast(jnp.left_shift(w, jnp.uint32(16)), F32))
        parts.append(pltpu.bitcast(jnp.bitwise_and(w, jnp.uint32(0xFFFF0000)), F32))
    return jnp.concatenate(parts, axis=1)


def _combine_kernel(h_ref, y0_ref, y1_ref, g_ref, fn_ref, o_ref):
    T = T_CMB
    h = h_ref[...]
    y0 = _unpack_rows(y0_ref, T)
    y1 = _unpack_rows(y1_ref, T)
    g = g_ref[...]
    v = h + (y0 * g[:, 0:1] + y1 * g[:, 1:2])
    ms = jnp.mean(v * v, axis=-1, keepdims=True)
    o_ref[...] = v * lax.rsqrt(ms + EPS) * fn_ref[...]


def _combine_call(h2d, y2, gates, fnorm):
    n_tok, D = h2d.shape
    T = T_CMB
    nt = n_tok // T
    return pl.pallas_call(
        _combine_kernel,
        grid=(nt,),
        in_specs=[
            pl.BlockSpec((T, D), lambda i: (i, 0)),
            pl.BlockSpec((T * Y_TILES, LANES), lambda i: (i, 0)),
            pl.BlockSpec((T * Y_TILES, LANES), lambda i: (nt + i, 0)),
            pl.BlockSpec((T, LANES), lambda i: (i, 0)),
            pl.BlockSpec((1, D), lambda i: (0, 0)),
        ],
        out_specs=pl.BlockSpec((T, D), lambda i: (i, 0)),
        out_shape=jax.ShapeDtypeStruct((n_tok, D), F32),
        compiler_params=pltpu.CompilerParams(dimension_semantics=("arbitrary",)),
        name="combine",
    )(h2d, y2, y2, gates, fnorm)


def _plan_kernel(e_ref, ux_ref, ones_ref, lx_ref, dest_ref, bstart_ref, nblk_ref):
    rows = e_ref.shape[0]
    ev = e_ref[...]
    lane8 = lax.broadcasted_iota(I32, (SUBLANES, LANES), 1)
    dest = jnp.zeros((rows, LANES), F32)
    bstart = jnp.zeros((SUBLANES, LANES), I32)
    nblk = jnp.zeros((SUBLANES, LANES), I32)
    pstart = jnp.zeros((1, LANES), F32)
    for e in range(N_EXPERTS):
        ohb = ev == e
        oh = jnp.where(ohb, 1.0, 0.0).astype(BF16)
        within = _dot(oh, ux_ref[...])
        rtot = _dot(oh, ones_ref[...])
        rpre = _dot(lx_ref[...], rtot.astype(BF16))
        cnt = rpre[rows - 1:rows, :] + rtot[rows - 1:rows, :]
        dest = dest + jnp.where(ohb, within + rpre + pstart, 0.0)
        cnt_i = cnt.astype(I32)
        nb_e = lax.shift_right_logical(cnt_i + (R_BLK - 1), R_SHIFT)
        bstart = jnp.where(lane8 == e, lax.shift_right_logical(pstart.astype(I32), R_SHIFT), bstart)
        nblk = jnp.where(lane8 == e, nb_e, nblk)
        pstart = pstart + lax.shift_left(nb_e, R_SHIFT).astype(F32)
    dest_ref[...] = dest.astype(I32)
    bstart_ref[...] = bstart
    nblk_ref[...] = nblk


def _invert_kernel(dest_ref, init_ref, inv_ref):
    n_asg = dest_ref.shape[0]
    pltpu.sync_copy(init_ref, inv_ref)

    def put(a, c):
        inv_ref[dest_ref[a]] = a
        return c

    lax.fori_loop(0, n_asg, put, 0, unroll=16)


def _index_tiles_kernel(inv_ref, pk_ref, *, n_tok):
    nb = pk_ref.shape[0] // SUBLANES
    per_blk = R_BLK // LANES
    assert per_blk == 2
    pk_ref[...] = jnp.zeros(pk_ref.shape, I32)
    for half in range(per_blk):
        a = inv_ref[pl.ds(half, nb, stride=per_blk), :]
        t = jnp.bitwise_and(a, n_tok - 1)
        pk_ref[pl.ds(half, nb, stride=SUBLANES), :] = lax.shift_left(lax.shift_right_logical(t, 1), 3)
        pk_ref[pl.ds(2 + half, nb, stride=SUBLANES), :] = lax.shift_left(jnp.bitwise_and(a, 1), 4)
        pk_ref[pl.ds(4 + half, nb, stride=SUBLANES), :] = a * Y_TILES


def _plan(e_rows, n_tok, nb):
    R = R_BLK
    n_asg = 2 * n_tok
    rows = n_asg // LANES
    emat = e_rows[0:2].reshape(rows, LANES)
    li = jnp.arange(LANES)
    ux = (li[:, None] < li[None, :]).astype(BF16)
    ones = jnp.ones((LANES, LANES), BF16)
    ri = jnp.arange(rows)
    lx = (ri[:, None] > ri[None, :]).astype(BF16)
    dest, bstart, nblk = pl.pallas_call(
        _plan_kernel,
        out_shape=[jax.ShapeDtypeStruct((rows, LANES), I32),
                   jax.ShapeDtypeStruct((SUBLANES, LANES), I32),
                   jax.ShapeDtypeStruct((SUBLANES, LANES), I32)],
        name="plan",
    )(emat, ux, ones, lx)
    pos = jnp.arange(nb * R, dtype=I32)
    init = n_asg + (pos & (R - 1)) + jnp.where(pos >= (nb - 1) * R, R, 0)
    inv = pl.pallas_call(
        _invert_kernel,
        in_specs=[pl.BlockSpec(memory_space=pltpu.SMEM), pl.BlockSpec(memory_space=pl.ANY)],
        out_specs=pl.BlockSpec(memory_space=pltpu.SMEM),
        out_shape=jax.ShapeDtypeStruct((nb * R,), I32),
        name="invert",
    )(dest.reshape(n_asg), init)
    pk = pl.pallas_call(
        functools.partial(_index_tiles_kernel, n_tok=n_tok),
        out_shape=jax.ShapeDtypeStruct((nb * SUBLANES, LANES), I32),
        name="index_tiles",
    )(inv.reshape(nb * R // LANES, LANES))
    return bstart[0, 0:N_EXPERTS], nblk[0, 0:N_EXPERTS], pk.reshape(nb, SUBLANES, LANES)


def kernel(x, norm_mix, w_in, ssd_conv_w, ssd_conv_b, dt_bias, a_log, d_skip, ssd_norm, sc_conv_w,
           sc_norm, w_out, norm_ffn, w_router_group, w_router_expert, w_gate, w_up, w_down, final_norm):
    B, L, D = x.shape
    n_tok = B * L
    depth = norm_mix.shape[0]
    assert depth == 1 and D == D_MODEL and (n_tok & (n_tok - 1)) == 0
    nb = -(-((2 * n_tok) // R_BLK + N_EXPERTS + 1) // SUBLANES) * SUBLANES

    o1 = SSD_WIDTH
    o2 = o1 + XBC
    o3 = o2 + N_HEADS
    wi = w_in.reshape(D, -1)
    wa = wi[:, 0:o2].astype(BF16)
    wb = wi[:, o3:].astype(BF16)
    wdt = jnp.pad(wi[:, o2:o3], ((0, 0), (0, LANES - N_HEADS))).astype(BF16)
    pad_h = (0, LANES - N_HEADS)
    dtb = jnp.pad(dt_bias[0], pad_h).reshape(1, LANES)
    alog = jnp.pad(a_log[0], pad_h).reshape(1, LANES)
    dskip = jnp.repeat(d_skip[0], HEAD_DIM).reshape(1, SSD_WIDTH)

    wre = jnp.transpose(w_router_expert[0], (1, 0, 2)).reshape(D, N_EXPERTS)
    wrt = jnp.pad(jnp.concatenate([w_router_group[0], wre], axis=1).T,
                  ((0, ROUTER_ROWS - N_GROUPS - N_EXPERTS), (0, 0)))
    wrt_hi = wrt.astype(BF16)
    wr = jnp.concatenate([wrt_hi, (wrt - wrt_hi.astype(F32)).astype(BF16)], axis=0)

    ri = jnp.arange(T_MIX)
    tri = (ri[:, None] >= ri[None, :]).astype(BF16)
    er = jnp.arange(LANES)
    ec = jnp.arange(SSD_WIDTH)
    e3 = ((er[:, None] < 48) & ((er[:, None] % 16) == (ec[None, :] // HEAD_DIM))).astype(BF16)
    gsum = ((ec[:, None] // HEAD_DIM) == er[None, :]).astype(BF16)

    h, hn2, e_rows, gates = _mixer_call(
        x, norm_mix[0].reshape(1, D), wa, wb, wdt, ssd_conv_w[0], ssd_conv_b[0].reshape(1, XBC), dtb, alog,
        dskip, ssd_norm[0].reshape(1, SSD_WIDTH), sc_conv_w[0], sc_norm[0].reshape(1, SC_WIDTH),
        w_out.reshape(-1, D).astype(BF16), norm_ffn[0].reshape(1, D), wr, tri, e3, gsum)

    bstart, nblk, pk = _plan(e_rows, n_tok, nb)
    y2 = _moe_call(bstart, nblk, pk, hn2, w_gate.reshape(N_EXPERTS, D, D_FF),
                   w_up.reshape(N_EXPERTS, D, D_FF), w_down.reshape(N_EXPERTS, D_FF, D), n_tok)
    out = _combine_call(h.reshape(n_tok, D), y2, gates, final_norm.reshape(1, D))
    return out.reshape(B, L, D)
```

```python
import functools

import jax
import jax.numpy as jnp
from jax import lax
from jax.experimental import pallas as pl
from jax.experimental.pallas import tpu as pltpu

F32 = jnp.float32
BF16 = jnp.bfloat16
I32 = jnp.int32

EPS = 1e-6
D_MODEL = 1024
N_HEADS = 16
HEAD_DIM = 64
N_BC_GROUPS = 2
STATE = 128
SSD_WIDTH = 1024
XBC = SSD_WIDTH + 2 * N_BC_GROUPS * STATE
SC_WIDTH = 1024
SC_GROUPS = 16
N_GROUPS = 4
EPG = 8
N_EXPERTS = 32
D_FF = 512

LANES = 128
SUBLANES = 8
ROW_TILES = D_MODEL // LANES
Y_TILES = ROW_TILES // 2

T_MIX = 512
Q_SSD = 128
PAIR_ROWS = T_MIX // 2
R_BLK = 256
R_SHIFT = 8
ROUTER_ROWS = 48
N_SLOTS = 3
T_CMB = 1024

VMEM_LIMIT = 56 * 1024 * 1024


def _dot(a, b):
    return jnp.dot(a, b, preferred_element_type=F32)


def _split3(v):
    p1 = v.astype(BF16).astype(F32)
    r1 = v - p1
    p2 = r1.astype(BF16).astype(F32)
    p3 = (r1 - p2).astype(BF16).astype(F32)
    return p1, p2, p3


def _pack3(v):
    p1, p2, p3 = _split3(v)
    return (p1 + pltpu.roll(p2, 16, 1) + pltpu.roll(p3, 32, 1)).astype(BF16)


def _silu(v):
    return v * jax.nn.sigmoid(v)


def _mixer_kernel(x_ref, gmix_ref, wa_ref, wb_ref, wdt_ref, convw_ref, convb_ref, dtb_ref, alog_ref,
                  dskip_ref, ssdn_ref, scw_ref, scn_ref, wout_ref, gffn_ref, wr_ref,
                  tri_ref, e3_ref, gsum_ref,
                  h_ref, hn2_ref, e_ref, g_ref,
                  cbuf, sbuf, st_ref, pbuf):
    T = T_MIX
    t = pl.program_id(1)

    @pl.when(t == 0)
    def _():
        cbuf[0:8, :] = jnp.zeros((8, XBC), F32)
        sbuf[0:8, :] = jnp.zeros((8, SC_WIDTH), F32)
        st_ref[...] = jnp.zeros(st_ref.shape, F32)

    @pl.when(t > 0)
    def _():
        cbuf[0:8, :] = cbuf[T:T + 8, :]
        sbuf[0:8, :] = sbuf[T:T + 8, :]

    x = x_ref[0]
    ms = jnp.mean(x * x, axis=-1, keepdims=True)
    hn = (x * lax.rsqrt(ms + EPS) * gmix_ref[...]).astype(BF16)

    z = _dot(hn, wa_ref[:, 0:1024])
    xbc = _dot(hn, wa_ref[:, 1024:2560])
    scb = _dot(hn, wb_ref[:, 0:1024])
    scc = _dot(hn, wb_ref[:, 1024:2048])
    scv = _dot(hn, wb_ref[:, 2048:3072])
    dt_raw = _dot(hn, wdt_ref[...])

    cbuf[8:8 + T, :] = xbc
    cw = convw_ref[...]
    acc = convb_ref[...] + cw[3:4, :] * xbc
    for k in range(3):
        acc = acc + cw[k:k + 1, :] * cbuf[5 + k:5 + k + T, :]
    xact = _silu(acc)
    xs = xact[:, 0:SSD_WIDTH]

    lane = lax.broadcasted_iota(I32, (1, LANES), 1)
    hmask = lane < N_HEADS
    a = jnp.where(hmask, -jnp.exp(alog_ref[...]), 0.0)
    dtv = dt_raw + dtb_ref[...]
    dt = jnp.where(hmask, jnp.maximum(dtv, 0.0) + jnp.log1p(jnp.exp(-jnp.abs(dtv))), 0.0)
    adt = dt * a
    c3 = _dot(tri_ref[...], _pack3(adt))
    ac = jnp.where(hmask, c3 + pltpu.roll(c3, LANES - 16, 1) + pltpu.roll(c3, LANES - 32, 1), 0.0)
    Q = Q_SSD
    n_sub = T // Q
    ends = [ac[(c + 1) * Q - 1:(c + 1) * Q, :] for c in range(n_sub)]
    base = jnp.concatenate([jnp.broadcast_to(ends[c - 1] if c else jnp.zeros_like(ends[0]), (Q, LANES))
                            for c in range(n_sub)], axis=0)
    endv = jnp.concatenate([jnp.broadcast_to(ends[c], (Q, LANES)) for c in range(n_sub)], axis=0)
    eac = jnp.where(hmask, jnp.exp(ac - base), 0.0)
    wdt = dt * jnp.exp(endv - ac)

    stacked = jnp.concatenate([_pack3(dt), _pack3(wdt), _pack3(eac)], axis=0)
    ex = _dot(stacked, e3_ref[...])
    dt_e = ex[0:T]
    wdt_e = ex[T:2 * T]
    eac_e = ex[2 * T:3 * T]
    xdt = (xs * dt_e).astype(BF16)
    xdtw = (xs * wdt_e).astype(BF16)

    ac_rows = ac.T
    rr = lax.broadcasted_iota(I32, (Q, Q), 0)
    cc = lax.broadcasted_iota(I32, (Q, Q), 1)
    causal = rr >= cc
    cblk = lax.shift_right_logical(lax.broadcasted_iota(I32, (Q, 4 * HEAD_DIM), 1), 6)

    y_cols = []
    for g in range(N_BC_GROUPS):
        bg = xact[:, SSD_WIDTH + STATE * g:SSD_WIDTH + STATE * (g + 1)]
        cg = xact[:, SSD_WIDTH + 2 * STATE + STATE * g:SSD_WIDTH + 2 * STATE + STATE * (g + 1)]
        st = st_ref[g]
        y_rows = [[], []]
        for c in range(n_sub):
            r0, r1 = c * Q, (c + 1) * Q
            bb = bg[r0:r1].astype(BF16)
            cbf = cg[r0:r1].astype(BF16)
            cb = lax.dot_general(cbf, bb, (((1,), (1,)), ((), ())), preferred_element_type=F32)
            yoff = _dot(cbf, st.astype(BF16))
            dec = eac_e[r1 - 1:r1, 512 * g:512 * (g + 1)]
            bgt = bg[r0:r1].T.astype(BF16)
            st = st * dec + _dot(bgt, xdtw[r0:r1, 512 * g:512 * (g + 1)])
            for q in range(2):
                ms_list = []
                for r in range(4):
                    hh = 8 * g + 4 * q + r
                    seg = ac[r0:r1, hh:hh + 1] - ac_rows[hh:hh + 1, r0:r1]
                    lh = jnp.exp(jnp.where(causal, seg, -jnp.inf))
                    ms_list.append((cb * lh).astype(BF16))
                lhs = jnp.concatenate(ms_list, axis=1)
                lo = 512 * g + 256 * q
                x4 = xdt[r0:r1, lo:lo + 256]
                rhs = jnp.concatenate(
                    [jnp.where(cblk == r, x4, jnp.zeros_like(x4)) for r in range(4)], axis=0)
                yd = _dot(lhs, rhs)
                y_rows[q].append(yd + eac_e[r0:r1, lo:lo + 256] * yoff[:, 256 * q:256 * (q + 1)])
        st_ref[g] = st
        y_cols += [jnp.concatenate(y_rows[0], axis=0), jnp.concatenate(y_rows[1], axis=0)]
    y = jnp.concatenate(y_cols, axis=1) + dskip_ref[...] * xs

    v = y * _silu(z)
    outs = []
    for g in range(N_BC_GROUPS):
        vg = v[:, 512 * g:512 * (g + 1)]
        msg = jnp.mean(vg * vg, axis=-1, keepdims=True)
        outs.append(vg * lax.rsqrt(msg + EPS))
    ssd_out = (jnp.concatenate(outs, axis=1) * ssdn_ref[...]).astype(BF16)

    u = scc * scv
    sbuf[8:8 + T, :] = u
    sw = scw_ref[...]
    conv = sw[2:3, :] * u + sw[1:2, :] * sbuf[7:7 + T, :] + sw[0:1, :] * sbuf[6:6 + T, :]
    sc = scb * conv
    gs = _dot((sc * sc).astype(BF16), gsum_ref[...])
    rstd = jnp.where(hmask, lax.rsqrt(gs * (1.0 / HEAD_DIM) + EPS), 0.0)
    rstd_e = _dot(_pack3(rstd), e3_ref[...])
    sc_out = (sc * rstd_e * scn_ref[...]).astype(BF16)

    mix = _dot(ssd_out, wout_ref[0:1024, :]) + _dot(sc_out, wout_ref[1024:2048, :])
    h = x + mix
    h_ref[0] = h

    ms2 = jnp.mean(h * h, axis=-1, keepdims=True)
    hn2 = h * lax.rsqrt(ms2 + EPS) * gffn_ref[...]
    hi = hn2.astype(BF16)
    bits = pltpu.bitcast(hi.astype(F32), jnp.uint32)
    for s in range(ROW_TILES):
        pbuf[pl.ds(s * T, T), :] = bits[:, LANES * s:LANES * (s + 1)]
    for s in range(ROW_TILES):
        even = pbuf[pl.ds(s * T, PAIR_ROWS, stride=2), :]
        odd = pbuf[pl.ds(s * T + 1, PAIR_ROWS, stride=2), :]
        hn2_ref[pl.ds(s, PAIR_ROWS, stride=ROW_TILES), :] = jnp.bitwise_or(
            jnp.right_shift(even, jnp.uint32(16)), odd)
    lo_ = (hn2 - hi.astype(F32)).astype(BF16)
    RT = ROUTER_ROWS
    nt_dims = (((1,), (1,)), ((), ()))
    both = lax.dot_general(wr_ref[...], hi, nt_dims, preferred_element_type=F32)
    low = lax.dot_general(wr_ref[0:RT, :], lo_, nt_dims, preferred_element_type=F32)
    logits = both[0:RT] + (low + both[RT:2 * RT])

    ri = lax.broadcasted_iota(I32, (RT, T), 0)
    ri_f = ri.astype(F32)
    neg = -jnp.inf
    big = 1e9
    gl = jnp.where(ri < N_GROUPS, logits, neg)
    gmax = jnp.max(gl, axis=0, keepdims=True)
    gidx = jnp.min(jnp.where(gl == gmax, ri_f, big), axis=0, keepdims=True)
    gsum = jnp.sum(jnp.where(ri < N_GROUPS, jnp.exp(logits - gmax), 0.0), axis=0, keepdims=True)
    gw = 1.0 / gsum
    egrp = lax.shift_right_logical(ri - N_GROUPS, 3).astype(F32)
    in_grp = (ri >= N_GROUPS) & (ri < N_GROUPS + N_EXPERTS) & (egrp == gidx)
    el = jnp.where(in_grp, logits, neg)
    v1 = jnp.max(el, axis=0, keepdims=True)
    i1 = jnp.min(jnp.where(el == v1, ri_f, big), axis=0, keepdims=True)
    el2 = jnp.where(ri_f == i1, neg, el)
    v2 = jnp.max(el2, axis=0, keepdims=True)
    i2 = jnp.min(jnp.where(el2 == v2, ri_f, big), axis=0, keepdims=True)
    p = jnp.exp(v2 - v1)
    s1 = 1.0 / (1.0 + p)
    gate1 = gw * s1
    gate2 = gw * (p * s1)
    r8 = lax.broadcasted_iota(I32, (SUBLANES, T), 0)
    e_ref[...] = jnp.where(r8 == 0, i1 - N_GROUPS, jnp.where(r8 == 1, i2 - N_GROUPS, 0.0)).astype(I32)
    rl = lax.broadcasted_iota(I32, (LANES, T), 0)
    g_ref[...] = jnp.where(rl == 0, gate1, jnp.where(rl == 1, gate2, 0.0)).T


def _mixer_call(x, gmix, wa, wb, wdt, convw, convb, dtb, alog, dskip, ssdn, scw, scn, wout, gffn,
                wr, tri, e3, gsum):
    B, L, D = x.shape
    T = T_MIX
    nt = L // T
    n_tok = B * L

    def const(shape):
        return pl.BlockSpec(shape, lambda b, t: (0,) * len(shape), pipeline_mode=pl.Buffered(1))

    in_specs = [
        pl.BlockSpec((1, T, D), lambda b, t: (b, t, 0)),
        const((1, D)),
        const(wa.shape), const(wb.shape), const(wdt.shape), const(convw.shape), const(convb.shape),
        const(dtb.shape), const(alog.shape), const(dskip.shape), const(ssdn.shape),
        const(scw.shape), const(scn.shape), const(wout.shape), const(gffn.shape),
        const(wr.shape), const(tri.shape), const(e3.shape), const(gsum.shape),
    ]
    out_shape = [
        jax.ShapeDtypeStruct((B, L, D), F32),
        jax.ShapeDtypeStruct((n_tok // 2 * ROW_TILES, LANES), jnp.uint32),
        jax.ShapeDtypeStruct((8, n_tok), I32),
        jax.ShapeDtypeStruct((n_tok, LANES), F32),
    ]
    out_specs = [
        pl.BlockSpec((1, T, D), lambda b, t: (b, t, 0)),
        pl.BlockSpec((PAIR_ROWS * ROW_TILES, LANES), lambda b, t: (b * nt + t, 0)),
        pl.BlockSpec((8, T), lambda b, t: (0, b * nt + t)),
        pl.BlockSpec((T, LANES), lambda b, t: (b * nt + t, 0)),
    ]
    return pl.pallas_call(
        _mixer_kernel,
        grid=(B, nt),
        in_specs=in_specs,
        out_specs=out_specs,
        out_shape=out_shape,
        scratch_shapes=[
            pltpu.VMEM((T + 8, XBC), F32),
            pltpu.VMEM((T + 8, SC_WIDTH), F32),
            pltpu.VMEM((N_BC_GROUPS, STATE, 512), F32),
            pltpu.VMEM((ROW_TILES * T, LANES), jnp.uint32),
        ],
        compiler_params=pltpu.CompilerParams(
            dimension_semantics=("arbitrary", "arbitrary"),
            vmem_limit_bytes=VMEM_LIMIT),
        name="mixer",
    )(x, gmix, wa, wb, wdt, convw, convb, dtb, alog, dskip, ssdn, scw, scn, wout, gffn, wr,
      tri, e3, gsum)


def _moe_kernel(bstart_ref, nblk_ref, pk_ref, hn2p_ref, wg_ref, wu_ref, wd_ref,
                y2_ref, hn2v, xbuf, ybuf, wgb, wub, wdb, idx_ref, lsem, ssem, isem, *, n_tok):
    R = R_BLK
    slab = R * ROW_TILES
    e = pl.program_id(0)
    g0 = bstart_ref[e]
    nb_e = nblk_ref[e]

    def idx_word(sl, field, j):
        return idx_ref[sl, 2 * field + j // LANES, j % LANES]

    def idx_copy(g, sl):
        return pltpu.make_async_copy(pk_ref.at[g], idx_ref.at[sl], isem.at[sl])

    def gather_rows(sl, lo, hi):
        for j in range(lo, hi):
            row = pl.multiple_of(idx_word(sl, 0, j), ROW_TILES)
            sh = idx_word(sl, 1, j).astype(jnp.uint32)
            w = hn2v[pl.ds(row, ROW_TILES), :]
            w = jnp.left_shift(jnp.right_shift(w, sh), jnp.uint32(16))
            xbuf[pl.ds(j * ROW_TILES, ROW_TILES), :] = pltpu.bitcast(w, F32)

    yslab = R * Y_TILES

    def y_copy(sl, j, dst_row):
        return pltpu.make_async_copy(
            ybuf.at[pl.ds(sl * yslab + j * Y_TILES, Y_TILES), :],
            y2_ref.at[pl.ds(pl.multiple_of(dst_row, Y_TILES), Y_TILES), :],
            ssem.at[sl])

    def scatter_start(sl, lo, hi):
        for j in range(lo, hi):
            y_copy(sl, j, idx_word(sl, 2, j)).start(priority=j % 2)

    def slab_wait(sl):
        view = ybuf.at[pl.ds(sl * yslab, yslab), :]
        pltpu.make_async_copy(view, view, ssem.at[sl]).wait()

    n_idx_blocks = pk_ref.shape[0]

    @pl.when(e == 0)
    def _():
        load = pltpu.make_async_copy(hn2p_ref, hn2v, lsem.at[0])
        load.start()
        idx_copy(0, 0).start()
        prev = idx_copy(n_idx_blocks - 1, N_SLOTS - 1)
        prev.start()
        ybuf[...] = jnp.zeros(ybuf.shape, ybuf.dtype)
        for sl, base in ((0, 2 * n_tok), (1, 2 * n_tok + 2 * R)):
            for j in range(R):
                y_copy(sl, j, (base + j) * Y_TILES).start(priority=j % 2)
        prev.wait()
        load.wait()

    @pl.when(nb_e > 0)
    def _():
        wgb[...] = wg_ref[0].astype(BF16)
        wub[...] = wu_ref[0].astype(BF16)
        wdb[...] = wd_ref[0].astype(BF16)

    c1, c2 = R // 3, 2 * (R // 3)

    def run_block(g, sl):
        prv = (sl + N_SLOTS - 1) % N_SLOTS
        nxt = (sl + 1) % N_SLOTS
        idx_copy(g, sl).wait()
        idx_copy(g + 1, nxt).start()
        slab_wait(sl)
        gather_rows(sl, 0, R)
        x = jnp.concatenate(
            [xbuf[pl.ds(s, R, stride=ROW_TILES), :] for s in range(ROW_TILES)], axis=1).astype(BF16)
        scatter_start(prv, 0, c1)
        gg = _dot(x, wgb[...])
        scatter_start(prv, c1, c2)
        uu = _dot(x, wub[...])
        act = (_silu(gg) * uu).astype(BF16)
        scatter_start(prv, c2, R)
        yy = _dot(act, wdb[...])
        for s in range(Y_TILES):
            lo_w = pltpu.bitcast(yy[:, 2 * LANES * s:2 * LANES * s + LANES].astype(BF16).astype(F32), jnp.uint32)
            hi_w = pltpu.bitcast(yy[:, 2 * LANES * s + LANES:2 * LANES * (s + 1)].astype(BF16).astype(F32),
                                 jnp.uint32)
            ybuf[pl.ds(sl * yslab + s, R, stride=Y_TILES), :] = jnp.bitwise_or(
                jnp.right_shift(lo_w, jnp.uint32(16)), hi_w)

    def block(i, carry):
        g = g0 + i
        sl = lax.rem(g, N_SLOTS)
        for k in range(N_SLOTS):
            @pl.when(sl == k)
            def _(k=k):
                run_block(g, k)
        return carry

    lax.fori_loop(0, nb_e, block, 0)

    @pl.when(e == pl.num_programs(0) - 1)
    def _():
        g_end = g0 + nb_e
        sl_end = lax.rem(g_end, N_SLOTS)
        for k in range(N_SLOTS):
            @pl.when(sl_end == k)
            def _(k=k):
                scatter_start((k + N_SLOTS - 1) % N_SLOTS, 0, R)
                idx_copy(g_end, k).wait()
        for k in range(N_SLOTS):
            slab_wait(k)


def _moe_call(bstart, nblk, pk, hn2p, wg, wu, wd, n_tok):
    R = R_BLK
    slab = R * ROW_TILES
    grid_spec = pltpu.PrefetchScalarGridSpec(
        num_scalar_prefetch=2,
        grid=(N_EXPERTS,),
        in_specs=[
            pl.BlockSpec(memory_space=pl.ANY),
            pl.BlockSpec(memory_space=pl.ANY),
            pl.BlockSpec((1, D_MODEL, D_FF), lambda e, bs, nb: (e, 0, 0)),
            pl.BlockSpec((1, D_MODEL, D_FF), lambda e, bs, nb: (e, 0, 0)),
            pl.BlockSpec((1, D_FF, D_MODEL), lambda e, bs, nb: (e, 0, 0)),
        ],
        out_specs=pl.BlockSpec(memory_space=pl.ANY),
        scratch_shapes=[
            pltpu.VMEM(hn2p.shape, jnp.uint32),
            pltpu.VMEM((slab, LANES), F32),
            pltpu.VMEM((N_SLOTS * R * Y_TILES, LANES), jnp.uint32),
            pltpu.VMEM((D_MODEL, D_FF), BF16),
            pltpu.VMEM((D_MODEL, D_FF), BF16),
            pltpu.VMEM((D_FF, D_MODEL), BF16),
            pltpu.SMEM((N_SLOTS, SUBLANES, LANES), I32),
            pltpu.SemaphoreType.DMA((1,)),
            pltpu.SemaphoreType.DMA((N_SLOTS,)),
            pltpu.SemaphoreType.DMA((N_SLOTS,)),
        ],
    )
    return pl.pallas_call(
        functools.partial(_moe_kernel, n_tok=n_tok),
        grid_spec=grid_spec,
        out_shape=jax.ShapeDtypeStruct(((2 * n_tok + N_SLOTS * R) * Y_TILES, LANES), jnp.uint32),
        compiler_params=pltpu.CompilerParams(
            dimension_semantics=("arbitrary",),
            vmem_limit_bytes=VMEM_LIMIT),
        name="moe",
    )(bstart, nblk, pk, hn2p, wg, wu, wd)


def _unpack_rows(y_ref, rows):
    parts = []
    for s in range(Y_TILES):
        w = y_ref[pl.ds(s, rows, stride=Y_TILES), :]
        parts.append(pltpu.bitcast(jnp.left_shift(w, jnp.uint32(16)), F32))
        parts.append(pltpu.bitcast(jnp.bitwise_and(w, jnp.uint32(0xFFFF0000)), F32))
    return jnp.concatenate(parts, axis=1)


def _combine_kernel(h_ref, y0_ref, y1_ref, g_ref, fn_ref, o_ref):
    T = T_CMB
    h = h_ref[...]
    y0 = _unpack_rows(y0_ref, T)
    y1 = _unpack_rows(y1_ref, T)
    g = g_ref[...]
    v = h + (y0 * g[:, 0:1] + y1 * g[:, 1:2])
    ms = jnp.mean(v * v, axis=-1, keepdims=True)
    o_ref[...] = v * lax.rsqrt(ms + EPS) * fn_ref[...]


def _combine_call(h2d, y2, gates, fnorm):
    n_tok, D = h2d.shape
    T = T_CMB
    nt = n_tok // T
    return pl.pallas_call(
        _combine_kernel,
        grid=(nt,),
        in_specs=[
            pl.BlockSpec((T, D), lambda i: (i, 0)),
            pl.BlockSpec((T * Y_TILES, LANES), lambda i: (i, 0)),
            pl.BlockSpec((T * Y_TILES, LANES), lambda i: (nt + i, 0)),
            pl.BlockSpec((T, LANES), lambda i: (i, 0)),
            pl.BlockSpec((1, D), lambda i: (0, 0)),
        ],
        out_specs=pl.BlockSpec((T, D), lambda i: (i, 0)),
        out_shape=jax.ShapeDtypeStruct((n_tok, D), F32),
        compiler_params=pltpu.CompilerParams(dimension_semantics=("arbitrary",)),
        name="combine",
    )(h2d, y2, y2, gates, fnorm)


def _plan_kernel(e_ref, ux_ref, ones_ref, lx_ref, dest_ref, bstart_ref, nblk_ref):
    rows = e_ref.shape[0]
    ev = e_ref[...]
    lane8 = lax.broadcasted_iota(I32, (SUBLANES, LANES), 1)
    dest = jnp.zeros((rows, LANES), F32)
    bstart = jnp.zeros((SUBLANES, LANES), I32)
    nblk = jnp.zeros((SUBLANES, LANES), I32)
    pstart = jnp.zeros((1, LANES), F32)
    for e in range(N_EXPERTS):
        ohb = ev == e
        oh = jnp.where(ohb, 1.0, 0.0).astype(BF16)
        within = _dot(oh, ux_ref[...])
        rtot = _dot(oh, ones_ref[...])
        rpre = _dot(lx_ref[...], rtot.astype(BF16))
        cnt = rpre[rows - 1:rows, :] + rtot[rows - 1:rows, :]
        dest = dest + jnp.where(ohb, within + rpre + pstart, 0.0)
        cnt_i = cnt.astype(I32)
        nb_e = lax.shift_right_logical(cnt_i + (R_BLK - 1), R_SHIFT)
        bstart = jnp.where(lane8 == e, lax.shift_right_logical(pstart.astype(I32), R_SHIFT), bstart)
        nblk = jnp.where(lane8 == e, nb_e, nblk)
        pstart = pstart + lax.shift_left(nb_e, R_SHIFT).astype(F32)
    dest_ref[...] = dest.astype(I32)
    bstart_ref[...] = bstart
    nblk_ref[...] = nblk


def _invert_kernel(dest_ref, init_ref, inv_ref):
    n_asg = dest_ref.shape[0]
    pltpu.sync_copy(init_ref, inv_ref)

    def put(a, c):
        inv_ref[dest_ref[a]] = a
        return c

    lax.fori_loop(0, n_asg, put, 0, unroll=16)


def _index_tiles_kernel(inv_ref, pk_ref, *, n_tok):
    nb = pk_ref.shape[0] // SUBLANES
    per_blk = R_BLK // LANES
    assert per_blk == 2
    pk_ref[...] = jnp.zeros(pk_ref.shape, I32)
    for half in range(per_blk):
        a = inv_ref[pl.ds(half, nb, stride=per_blk), :]
        t = jnp.bitwise_and(a, n_tok - 1)
        pk_ref[pl.ds(half, nb, stride=SUBLANES), :] = lax.shift_left(lax.shift_right_logical(t, 1), 3)
        pk_ref[pl.ds(2 + half, nb, stride=SUBLANES), :] = lax.shift_left(jnp.bitwise_and(a, 1), 4)
        pk_ref[pl.ds(4 + half, nb, stride=SUBLANES), :] = a * Y_TILES


def _plan(e_rows, n_tok, nb):
    R = R_BLK
    n_asg = 2 * n_tok
    rows = n_asg // LANES
    emat = e_rows[0:2].reshape(rows, LANES)
    li = jnp.arange(LANES)
    ux = (li[:, None] < li[None, :]).astype(BF16)
    ones = jnp.ones((LANES, LANES), BF16)
    ri = jnp.arange(rows)
    lx = (ri[:, None] > ri[None, :]).astype(BF16)
    dest, bstart, nblk = pl.pallas_call(
        _plan_kernel,
        out_shape=[jax.ShapeDtypeStruct((rows, LANES), I32),
                   jax.ShapeDtypeStruct((SUBLANES, LANES), I32),
                   jax.ShapeDtypeStruct((SUBLANES, LANES), I32)],
        name="plan",
    )(emat, ux, ones, lx)
    pos = jnp.arange(nb * R, dtype=I32)
    init = n_asg + (pos & (R - 1)) + jnp.where(pos >= (nb - 1) * R, R, 0)
    inv = pl.pallas_call(
        _invert_kernel,
        in_specs=[pl.BlockSpec(memory_space=pltpu.SMEM), pl.BlockSpec(memory_space=pl.ANY)],
        out_specs=pl.BlockSpec(memory_space=pltpu.SMEM),
        out_shape=jax.ShapeDtypeStruct((nb * R,), I32),
        name="invert",
    )(dest.reshape(n_asg), init)
    pk = pl.pallas_call(
        functools.partial(_index_tiles_kernel, n_tok=n_tok),
        out_shape=jax.ShapeDtypeStruct((nb * SUBLANES, LANES), I32),
        name="index_tiles",
    )(inv.reshape(nb * R // LANES, LANES))
    return bstart[0, 0:N_EXPERTS], nblk[0, 0:N_EXPERTS], pk.reshape(nb, SUBLANES, LANES)


def kernel(x, norm_mix, w_in, ssd_conv_w, ssd_conv_b, dt_bias, a_log, d_skip, ssd_norm, sc_conv_w,
           sc_norm, w_out, norm_ffn, w_router_group, w_router_expert, w_gate, w_up, w_down, final_norm):
    B, L, D = x.shape
    n_tok = B * L
    depth = norm_mix.shape[0]
    assert depth == 1 and D == D_MODEL and (n_tok & (n_tok - 1)) == 0
    nb = -(-((2 * n_tok) // R_BLK + N_EXPERTS + 1) // SUBLANES) * SUBLANES

    o1 = SSD_WIDTH
    o2 = o1 + XBC
    o3 = o2 + N_HEADS
    wi = w_in.reshape(D, -1)
    wa = wi[:, 0:o2].astype(BF16)
    wb = wi[:, o3:].astype(BF16)
    wdt = jnp.pad(wi[:, o2:o3], ((0, 0), (0, LANES - N_HEADS))).astype(BF16)
    pad_h = (0, LANES - N_HEADS)
    dtb = jnp.pad(dt_bias[0], pad_h).reshape(1, LANES)
    alog = jnp.pad(a_log[0], pad_h).reshape(1, LANES)
    dskip = jnp.repeat(d_skip[0], HEAD_DIM).reshape(1, SSD_WIDTH)

    wre = jnp.transpose(w_router_expert[0], (1, 0, 2)).reshape(D, N_EXPERTS)
    wrt = jnp.pad(jnp.concatenate([w_router_group[0], wre], axis=1).T,
                  ((0, ROUTER_ROWS - N_GROUPS - N_EXPERTS), (0, 0)))
    wrt_hi = wrt.astype(BF16)
    wr = jnp.concatenate([wrt_hi, (wrt - wrt_hi.astype(F32)).astype(BF16)], axis=0)

    ri = jnp.arange(T_MIX)
    tri = (ri[:, None] >= ri[None, :]).astype(BF16)
    er = jnp.arange(LANES)
    ec = jnp.arange(SSD_WIDTH)
    e3 = ((er[:, None] < 48) & ((er[:, None] % 16) == (ec[None, :] // HEAD_DIM))).astype(BF16)
    gsum = ((ec[:, None] // HEAD_DIM) == er[None, :]).astype(BF16)

    h, hn2, e_rows, gates = _mixer_call(
        x, norm_mix[0].reshape(1, D), wa, wb, wdt, ssd_conv_w[0], ssd_conv_b[0].reshape(1, XBC), dtb, alog,
        dskip, ssd_norm[0].reshape(1, SSD_WIDTH), sc_conv_w[0], sc_norm[0].reshape(1, SC_WIDTH),
        w_out.reshape(-1, D).astype(BF16), norm_ffn[0].reshape(1, D), wr, tri, e3, gsum)

    bstart, nblk, pk = _plan(e_rows, n_tok, nb)
    y2 = _moe_call(bstart, nblk, pk, hn2, w_gate.reshape(N_EXPERTS, D, D_FF),
                   w_up.reshape(N_EXPERTS, D, D_FF), w_down.reshape(N_EXPERTS, D_FF, D), n_tok)
    out = _combine_call(h.reshape(n_tok, D), y2, gates, final_norm.reshape(1, D))
    return out.reshape(B, L, D)
```

```python
import functools

import jax
import jax.numpy as jnp
from jax import lax
from jax.experimental import pallas as pl
from jax.experimental.pallas import tpu as pltpu

F32 = jnp.float32
BF16 = jnp.bfloat16
I32 = jnp.int32

EPS = 1e-6
D_MODEL = 1024
N_HEADS = 16
HEAD_DIM = 64
N_BC_GROUPS = 2
STATE = 128
SSD_WIDTH = 1024
XBC = SSD_WIDTH + 2 * N_BC_GROUPS * STATE
SC_WIDTH = 1024
SC_GROUPS = 16
N_GROUPS = 4
EPG = 8
N_EXPERTS = 32
D_FF = 512

LANES = 128
SUBLANES = 8
ROW_TILES = D_MODEL // LANES
Y_TILES = ROW_TILES // 2

T_MIX = 512
Q_SSD = 128
PAIR_ROWS = T_MIX // 2
R_BLK = 256
R_SHIFT = 8
ROUTER_ROWS = 48
N_SLOTS = 3
T_CMB = 1024

VMEM_LIMIT = 56 * 1024 * 1024


def _dot(a, b):
    return jnp.dot(a, b, preferred_element_type=F32)


def _split3(v):
    p1 = v.astype(BF16).astype(F32)
    r1 = v - p1
    p2 = r1.astype(BF16).astype(F32)
    p3 = (r1 - p2).astype(BF16).astype(F32)
    return p1, p2, p3


def _pack3(v):
    p1, p2, p3 = _split3(v)
    return (p1 + pltpu.roll(p2, 16, 1) + pltpu.roll(p3, 32, 1)).astype(BF16)


def _silu(v):
    return v * jax.nn.sigmoid(v)


def _mixer_kernel(x_ref, gmix_ref, wa_ref, wb_ref, wdt_ref, convw_ref, convb_ref, dtb_ref, alog_ref,
                  dskip_ref, ssdn_ref, scw_ref, scn_ref, wout_ref, gffn_ref, wr_ref,
                  tri_ref, e3_ref, gsum_ref,
                  h_ref, hn2_ref, e_ref, g_ref,
                  cbuf, sbuf, st_ref, pbuf):
    T = T_MIX
    t = pl.program_id(1)

    @pl.when(t == 0)
    def _():
        cbuf[0:8, :] = jnp.zeros((8, XBC), F32)
        sbuf[0:8, :] = jnp.zeros((8, SC_WIDTH), F32)
        st_ref[...] = jnp.zeros(st_ref.shape, F32)

    @pl.when(t > 0)
    def _():
        cbuf[0:8, :] = cbuf[T:T + 8, :]
        sbuf[0:8, :] = sbuf[T:T + 8, :]

    x = x_ref[0]
    ms = jnp.mean(x * x, axis=-1, keepdims=True)
    hn = (x * lax.rsqrt(ms + EPS) * gmix_ref[...]).astype(BF16)

    z = _dot(hn, wa_ref[:, 0:1024])
    xbc = _dot(hn, wa_ref[:, 1024:2560])
    scb = _dot(hn, wb_ref[:, 0:1024])
    scc = _dot(hn, wb_ref[:, 1024:2048])
    scv = _dot(hn, wb_ref[:, 2048:3072])
    dt_raw = _dot(hn, wdt_ref[...])

    cbuf[8:8 + T, :] = xbc
    cw = convw_ref[...]
    acc = convb_ref[...] + cw[3:4, :] * xbc
    for k in range(3):
        acc = acc + cw[k:k + 1, :] * cbuf[5 + k:5 + k + T, :]
    xact = _silu(acc)
    xs = xact[:, 0:SSD_WIDTH]

    lane = lax.broadcasted_iota(I32, (1, LANES), 1)
    hmask = lane < N_HEADS
    a = jnp.where(hmask, -jnp.exp(alog_ref[...]), 0.0)
    dtv = dt_raw + dtb_ref[...]
    dt = jnp.where(hmask, jnp.maximum(dtv, 0.0) + jnp.log1p(jnp.exp(-jnp.abs(dtv))), 0.0)
    adt = dt * a
    c3 = _dot(tri_ref[...], _pack3(adt))
    ac = jnp.where(hmask, c3 + pltpu.roll(c3, LANES - 16, 1) + pltpu.roll(c3, LANES - 32, 1), 0.0)
    Q = Q_SSD
    n_sub = T // Q
    ends = [ac[(c + 1) * Q - 1:(c + 1) * Q, :] for c in range(n_sub)]
    base = jnp.concatenate([jnp.broadcast_to(ends[c - 1] if c else jnp.zeros_like(ends[0]), (Q, LANES))
                            for c in range(n_sub)], axis=0)
    endv = jnp.concatenate([jnp.broadcast_to(ends[c], (Q, LANES)) for c in range(n_sub)], axis=0)
    eac = jnp.where(hmask, jnp.exp(ac - base), 0.0)
    wdt = dt * jnp.exp(endv - ac)

    stacked = jnp.concatenate([_pack3(dt), _pack3(wdt), _pack3(eac)], axis=0)
    ex = _dot(stacked, e3_ref[...])
    dt_e = ex[0:T]
    wdt_e = ex[T:2 * T]
    eac_e = ex[2 * T:3 * T]
    xdt = (xs * dt_e).astype(BF16)
    xdtw = (xs * wdt_e).astype(BF16)

    ac_rows = ac.T
    rr = lax.broadcasted_iota(I32, (Q, Q), 0)
    cc = lax.broadcasted_iota(I32, (Q, Q), 1)
    causal = rr >= cc
    cblk = lax.shift_right_logical(lax.broadcasted_iota(I32, (Q, 4 * HEAD_DIM), 1), 6)

    y_cols = []
    for g in range(N_BC_GROUPS):
        bg = xact[:, SSD_WIDTH + STATE * g:SSD_WIDTH + STATE * (g + 1)]
        cg = xact[:, SSD_WIDTH + 2 * STATE + STATE * g:SSD_WIDTH + 2 * STATE + STATE * (g + 1)]
        st = st_ref[g]
        y_rows = [[], []]
        for c in range(n_sub):
            r0, r1 = c * Q, (c + 1) * Q
            bb = bg[r0:r1].astype(BF16)
            cbf = cg[r0:r1].astype(BF16)
            cb = lax.dot_general(cbf, bb, (((1,), (1,)), ((), ())), preferred_element_type=F32)
            yoff = _dot(cbf, st.astype(BF16))
            dec = eac_e[r1 - 1:r1, 512 * g:512 * (g + 1)]
            bgt = bg[r0:r1].T.astype(BF16)
            st = st * dec + _dot(bgt, xdtw[r0:r1, 512 * g:512 * (g + 1)])
            for q in range(2):
                ms_list = []
                for r in range(4):
                    hh = 8 * g + 4 * q + r
                    seg = ac[r0:r1, hh:hh + 1] - ac_rows[hh:hh + 1, r0:r1]
                    lh = jnp.exp(jnp.where(causal, seg, -jnp.inf))
                    ms_list.append((cb * lh).astype(BF16))
                lhs = jnp.concatenate(ms_list, axis=1)
                lo = 512 * g + 256 * q
                x4 = xdt[r0:r1, lo:lo + 256]
                rhs = jnp.concatenate(
                    [jnp.where(cblk == r, x4, jnp.zeros_like(x4)) for r in range(4)], axis=0)
                yd = _dot(lhs, rhs)
                y_rows[q].append(yd + eac_e[r0:r1, lo:lo + 256] * yoff[:, 256 * q:256 * (q + 1)])
        st_ref[g] = st
        y_cols += [jnp.concatenate(y_rows[0], axis=0), jnp.concatenate(y_rows[1], axis=0)]
    y = jnp.concatenate(y_cols, axis=1) + dskip_ref[...] * xs

    v = y * _silu(z)
    outs = []
    for g in range(N_BC_GROUPS):
        vg = v[:, 512 * g:512 * (g + 1)]
        msg = jnp.mean(vg * vg, axis=-1, keepdims=True)
        outs.append(vg * lax.rsqrt(msg + EPS))
    ssd_out = (jnp.concatenate(outs, axis=1) * ssdn_ref[...]).astype(BF16)

    u = scc * scv
    sbuf[8:8 + T, :] = u
    sw = scw_ref[...]
    conv = sw[2:3, :] * u + sw[1:2, :] * sbuf[7:7 + T, :] + sw[0:1, :] * sbuf[6:6 + T, :]
    sc = scb * conv
    gs = _dot((sc * sc).astype(BF16), gsum_ref[...])
    rstd = jnp.where(hmask, lax.rsqrt(gs * (1.0 / HEAD_DIM) + EPS), 0.0)
    rstd_e = _dot(_pack3(rstd), e3_ref[...])
    sc_out = (sc * rstd_e * scn_ref[...]).astype(BF16)

    mix = _dot(ssd_out, wout_ref[0:1024, :]) + _dot(sc_out, wout_ref[1024:2048, :])
    h = x + mix
    h_ref[0] = h

    ms2 = jnp.mean(h * h, axis=-1, keepdims=True)
    hn2 = h * lax.rsqrt(ms2 + EPS) * gffn_ref[...]
    hi = hn2.astype(BF16)
    bits = pltpu.bitcast(hi.astype(F32), jnp.uint32)
    for s in range(ROW_TILES):
        pbuf[pl.ds(s * T, T), :] = bits[:, LANES * s:LANES * (s + 1)]
    for s in range(ROW_TILES):
        even = pbuf[pl.ds(s * T, PAIR_ROWS, stride=2), :]
        odd = pbuf[pl.ds(s * T + 1, PAIR_ROWS, stride=2), :]
        hn2_ref[pl.ds(s, PAIR_ROWS, stride=ROW_TILES), :] = jnp.bitwise_or(
            jnp.right_shift(even, jnp.uint32(16)), odd)
    lo_ = (hn2 - hi.astype(F32)).astype(BF16)
    RT = ROUTER_ROWS
    nt_dims = (((1,), (1,)), ((), ()))
    both = lax.dot_general(wr_ref[...], hi, nt_dims, preferred_element_type=F32)
    low = lax.dot_general(wr_ref[0:RT, :], lo_, nt_dims, preferred_element_type=F32)
    logits = both[0:RT] + (low + both[RT:2 * RT])

    ri = lax.broadcasted_iota(I32, (RT, T), 0)
    ri_f = ri.astype(F32)
    neg = -jnp.inf
    big = 1e9
    gl = jnp.where(ri < N_GROUPS, logits, neg)
    gmax = jnp.max(gl, axis=0, keepdims=True)
    gidx = jnp.min(jnp.where(gl == gmax, ri_f, big), axis=0, keepdims=True)
    gsum = jnp.sum(jnp.where(ri < N_GROUPS, jnp.exp(logits - gmax), 0.0), axis=0, keepdims=True)
    gw = 1.0 / gsum
    egrp = lax.shift_right_logical(ri - N_GROUPS, 3).astype(F32)
    in_grp = (ri >= N_GROUPS) & (ri < N_GROUPS + N_EXPERTS) & (egrp == gidx)
    el = jnp.where(in_grp, logits, neg)
    v1 = jnp.max(el, axis=0, keepdims=True)
    i1 = jnp.min(jnp.where(el == v1, ri_f, big), axis=0, keepdims=True)
    el2 = jnp.where(ri_f == i1, neg, el)
    v2 = jnp.max(el2, axis=0, keepdims=True)
    i2 = jnp.min(jnp.where(el2 == v2, ri_f, big), axis=0, keepdims=True)
    p = jnp.exp(v2 - v1)
    s1 = 1.0 / (1.0 + p)
    gate1 = gw * s1
    gate2 = gw * (p * s1)
    r8 = lax.broadcasted_iota(I32, (SUBLANES, T), 0)
    e_ref[...] = jnp.where(r8 == 0, i1 - N_GROUPS, jnp.where(r8 == 1, i2 - N_GROUPS, 0.0)).astype(I32)
    rl = lax.broadcasted_iota(I32, (LANES, T), 0)
    g_ref[...] = jnp.where(rl == 0, gate1, jnp.where(rl == 1, gate2, 0.0)).T


def _mixer_call(x, gmix, wa, wb, wdt, convw, convb, dtb, alog, dskip, ssdn, scw, scn, wout, gffn,
                wr, tri, e3, gsum):
    B, L, D = x.shape
    T = T_MIX
    nt = L // T
    n_tok = B * L

    def const(shape):
        return pl.BlockSpec(shape, lambda b, t: (0,) * len(shape), pipeline_mode=pl.Buffered(1))

    in_specs = [
        pl.BlockSpec((1, T, D), lambda b, t: (b, t, 0)),
        const((1, D)),
        const(wa.shape), const(wb.shape), const(wdt.shape), const(convw.shape), const(convb.shape),
        const(dtb.shape), const(alog.shape), const(dskip.shape), const(ssdn.shape),
        const(scw.shape), const(scn.shape), const(wout.shape), const(gffn.shape),
        const(wr.shape), const(tri.shape), const(e3.shape), const(gsum.shape),
    ]
    out_shape = [
        jax.ShapeDtypeStruct((B, L, D), F32),
        jax.ShapeDtypeStruct((n_tok // 2 * ROW_TILES, LANES), jnp.uint32),
        jax.ShapeDtypeStruct((8, n_tok), I32),
        jax.ShapeDtypeStruct((n_tok, LANES), F32),
    ]
    out_specs = [
        pl.BlockSpec((1, T, D), lambda b, t: (b, t, 0)),
        pl.BlockSpec((PAIR_ROWS * ROW_TILES, LANES), lambda b, t: (b * nt + t, 0)),
        pl.BlockSpec((8, T), lambda b, t: (0, b * nt + t)),
        pl.BlockSpec((T, LANES), lambda b, t: (b * nt + t, 0)),
    ]
    return pl.pallas_call(
        _mixer_kernel,
        grid=(B, nt),
        in_specs=in_specs,
        out_specs=out_specs,
        out_shape=out_shape,
        scratch_shapes=[
            pltpu.VMEM((T + 8, XBC), F32),
            pltpu.VMEM((T + 8, SC_WIDTH), F32),
            pltpu.VMEM((N_BC_GROUPS, STATE, 512), F32),
            pltpu.VMEM((ROW_TILES * T, LANES), jnp.uint32),
        ],
        compiler_params=pltpu.CompilerParams(
            dimension_semantics=("arbitrary", "arbitrary"),
            vmem_limit_bytes=VMEM_LIMIT),
        name="mixer",
    )(x, gmix, wa, wb, wdt, convw, convb, dtb, alog, dskip, ssdn, scw, scn, wout, gffn, wr,
      tri, e3, gsum)


def _moe_kernel(bstart_ref, nblk_ref, pk_ref, hn2p_ref, wg_ref, wu_ref, wd_ref,
                y2_ref, hn2v, xbuf, ybuf, wgu, wdb, idx_ref, lsem, ssem, isem, *, n_tok):
    R = R_BLK
    slab = R * ROW_TILES
    e = pl.program_id(0)
    g0 = bstart_ref[e]
    nb_e = nblk_ref[e]

    def idx_word(sl, field, j):
        return idx_ref[sl, 2 * field + j // LANES, j % LANES]

    def idx_copy(g, sl):
        return pltpu.make_async_copy(pk_ref.at[g], idx_ref.at[sl], isem.at[sl])

    def gather_rows(sl, lo, hi):
        for j in range(lo, hi):
            row = pl.multiple_of(idx_word(sl, 0, j), ROW_TILES)
            sh = idx_word(sl, 1, j).astype(jnp.uint32)
            w = hn2v[pl.ds(row, ROW_TILES), :]
            w = jnp.left_shift(jnp.right_shift(w, sh), jnp.uint32(16))
            xbuf[pl.ds(j * ROW_TILES, ROW_TILES), :] = pltpu.bitcast(w, F32)

    yslab = R * Y_TILES
    FH = D_FF // 2

    def y_copy(sl, j, dst_row):
        return pltpu.make_async_copy(
            ybuf.at[pl.ds(sl * yslab + j * Y_TILES, Y_TILES), :],
            y2_ref.at[pl.ds(pl.multiple_of(dst_row, Y_TILES), Y_TILES), :],
            ssem.at[sl])

    def scatter_start(sl, lo, hi):
        for j in range(lo, hi):
            y_copy(sl, j, idx_word(sl, 2, j)).start(priority=j % 2)

    def slab_wait(sl):
        view = ybuf.at[pl.ds(sl * yslab, yslab), :]
        pltpu.make_async_copy(view, view, ssem.at[sl]).wait()

    n_idx_blocks = pk_ref.shape[0]

    @pl.when(e == 0)
    def _():
        load = pltpu.make_async_copy(hn2p_ref, hn2v, lsem.at[0])
        load.start()
        idx_copy(0, 0).start()
        prev = idx_copy(n_idx_blocks - 1, N_SLOTS - 1)
        prev.start()
        ybuf[...] = jnp.zeros(ybuf.shape, ybuf.dtype)
        for sl, base in ((0, 2 * n_tok), (1, 2 * n_tok + 2 * R)):
            for j in range(R):
                y_copy(sl, j, (base + j) * Y_TILES).start(priority=j % 2)
        prev.wait()
        load.wait()

    @pl.when(nb_e > 0)
    def _():
        for j in range(2):
            wgu[:, 2 * j * FH:(2 * j + 1) * FH] = wg_ref[0, :, j * FH:(j + 1) * FH].astype(BF16)
            wgu[:, (2 * j + 1) * FH:(2 * j + 2) * FH] = wu_ref[0, :, j * FH:(j + 1) * FH].astype(BF16)
        wdb[...] = wd_ref[0].astype(BF16)

    c1, c2 = R // 3, 2 * (R // 3)

    def run_block(g, sl):
        prv = (sl + N_SLOTS - 1) % N_SLOTS
        nxt = (sl + 1) % N_SLOTS
        idx_copy(g, sl).wait()
        idx_copy(g + 1, nxt).start()
        slab_wait(sl)
        gather_rows(sl, 0, R)
        x = jnp.concatenate(
            [xbuf[pl.ds(s, R, stride=ROW_TILES), :] for s in range(ROW_TILES)], axis=1).astype(BF16)
        scatter_start(prv, 0, c1)
        gu = _dot(x, wgu[...])
        scatter_start(prv, c1, c2)
        act = jnp.concatenate(
            [_silu(gu[:, 2 * j * FH:(2 * j + 1) * FH]) * gu[:, (2 * j + 1) * FH:(2 * j + 2) * FH]
             for j in range(2)], axis=1).astype(BF16)
        scatter_start(prv, c2, R)
        yy = _dot(act, wdb[...])
        for s in range(Y_TILES):
            lo_w = pltpu.bitcast(yy[:, 2 * LANES * s:2 * LANES * s + LANES].astype(BF16).astype(F32), jnp.uint32)
            hi_w = pltpu.bitcast(yy[:, 2 * LANES * s + LANES:2 * LANES * (s + 1)].astype(BF16).astype(F32),
                                 jnp.uint32)
            ybuf[pl.ds(sl * yslab + s, R, stride=Y_TILES), :] = jnp.bitwise_or(
                jnp.right_shift(lo_w, jnp.uint32(16)), hi_w)

    def block(i, carry):
        g = g0 + i
        sl = lax.rem(g, N_SLOTS)
        for k in range(N_SLOTS):
            @pl.when(sl == k)
            def _(k=k):
                run_block(g, k)
        return carry

    lax.fori_loop(0, nb_e, block, 0)

    @pl.when(e == pl.num_programs(0) - 1)
    def _():
        g_end = g0 + nb_e
        sl_end = lax.rem(g_end, N_SLOTS)
        for k in range(N_SLOTS):
            @pl.when(sl_end == k)
            def _(k=k):
                scatter_start((k + N_SLOTS - 1) % N_SLOTS, 0, R)
                idx_copy(g_end, k).wait()
        for k in range(N_SLOTS):
            slab_wait(k)


def _moe_call(bstart, nblk, pk, hn2p, wg, wu, wd, n_tok):
    R = R_BLK
    slab = R * ROW_TILES
    grid_spec = pltpu.PrefetchScalarGridSpec(
        num_scalar_prefetch=2,
        grid=(N_EXPERTS,),
        in_specs=[
            pl.BlockSpec(memory_space=pl.ANY),
            pl.BlockSpec(memory_space=pl.ANY),
            pl.BlockSpec((1, D_MODEL, D_FF), lambda e, bs, nb: (e, 0, 0)),
            pl.BlockSpec((1, D_MODEL, D_FF), lambda e, bs, nb: (e, 0, 0)),
            pl.BlockSpec((1, D_FF, D_MODEL), lambda e, bs, nb: (e, 0, 0)),
        ],
        out_specs=pl.BlockSpec(memory_space=pl.ANY),
        scratch_shapes=[
            pltpu.VMEM(hn2p.shape, jnp.uint32),
            pltpu.VMEM((slab, LANES), F32),
            pltpu.VMEM((N_SLOTS * R * Y_TILES, LANES), jnp.uint32),
            pltpu.VMEM((D_MODEL, 2 * D_FF), BF16),
            pltpu.VMEM((D_FF, D_MODEL), BF16),
            pltpu.SMEM((N_SLOTS, SUBLANES, LANES), I32),
            pltpu.SemaphoreType.DMA((1,)),
            pltpu.SemaphoreType.DMA((N_SLOTS,)),
            pltpu.SemaphoreType.DMA((N_SLOTS,)),
        ],
    )
    return pl.pallas_call(
        functools.partial(_moe_kernel, n_tok=n_tok),
        grid_spec=grid_spec,
        out_shape=jax.ShapeDtypeStruct(((2 * n_tok + N_SLOTS * R) * Y_TILES, LANES), jnp.uint32),
        compiler_params=pltpu.CompilerParams(
            dimension_semantics=("arbitrary",),
            vmem_limit_bytes=VMEM_LIMIT),
        name="moe",
    )(bstart, nblk, pk, hn2p, wg, wu, wd)


def _unpack_rows(y_ref, rows):
    parts = []
    for s in range(Y_TILES):
        w = y_ref[pl.ds(s, rows, stride=Y_TILES), :]
        parts.append(pltpu.bitcast(jnp.left_shift(w, jnp.uint32(16)), F32))
        parts.append(pltpu.bitcast(jnp.bitwise_and(w, jnp.uint32(0xFFFF0000)), F32))
    return jnp.concatenate(parts, axis=1)


def _combine_kernel(h_ref, y0_ref, y1_ref, g_ref, fn_ref, o_ref):
    T = T_CMB
    h = h_ref[...]
    y0 = _unpack_rows(y0_ref, T)
    y1 = _unpack_rows(y1_ref, T)
    g = g_ref[...]
    v = h + (y0 * g[:, 0:1] + y1 * g[:, 1:2])
    ms = jnp.mean(v * v, axis=-1, keepdims=True)
    o_ref[...] = v * lax.rsqrt(ms + EPS) * fn_ref[...]


def _combine_call(h2d, y2, gates, fnorm):
    n_tok, D = h2d.shape
    T = T_CMB
    nt = n_tok // T
    return pl.pallas_call(
        _combine_kernel,
        grid=(nt,),
        in_specs=[
            pl.BlockSpec((T, D), lambda i: (i, 0)),
            pl.BlockSpec((T * Y_TILES, LANES), lambda i: (i, 0)),
            pl.BlockSpec((T * Y_TILES, LANES), lambda i: (nt + i, 0)),
            pl.BlockSpec((T, LANES), lambda i: (i, 0)),
            pl.BlockSpec((1, D), lambda i: (0, 0)),
        ],
        out_specs=pl.BlockSpec((T, D), lambda i: (i, 0)),
        out_shape=jax.ShapeDtypeStruct((n_tok, D), F32),
        compiler_params=pltpu.CompilerParams(dimension_semantics=("arbitrary",)),
        name="combine",
    )(h2d, y2, y2, gates, fnorm)


def _plan_kernel(e_ref, ux_ref, ones_ref, lx_ref, dest_ref, bstart_ref, nblk_ref):
    rows = e_ref.shape[0]
    ev = e_ref[...]
    lane8 = lax.broadcasted_iota(I32, (SUBLANES, LANES), 1)
    dest = jnp.zeros((rows, LANES), F32)
    bstart = jnp.zeros((SUBLANES, LANES), I32)
    nblk = jnp.zeros((SUBLANES, LANES), I32)
    pstart = jnp.zeros((1, LANES), F32)
    for e in range(N_EXPERTS):
        ohb = ev == e
        oh = jnp.where(ohb, 1.0, 0.0).astype(BF16)
        within = _dot(oh, ux_ref[...])
        rtot = _dot(oh, ones_ref[...])
        rpre = _dot(lx_ref[...], rtot.astype(BF16))
        cnt = rpre[rows - 1:rows, :] + rtot[rows - 1:rows, :]
        dest = dest + jnp.where(ohb, within + rpre + pstart, 0.0)
        cnt_i = cnt.astype(I32)
        nb_e = lax.shift_right_logical(cnt_i + (R_BLK - 1), R_SHIFT)
        bstart = jnp.where(lane8 == e, lax.shift_right_logical(pstart.astype(I32), R_SHIFT), bstart)
        nblk = jnp.where(lane8 == e, nb_e, nblk)
        pstart = pstart + lax.shift_left(nb_e, R_SHIFT).astype(F32)
    dest_ref[...] = dest.astype(I32)
    bstart_ref[...] = bstart
    nblk_ref[...] = nblk


def _invert_kernel(dest_ref, init_ref, inv_ref):
    n_asg = dest_ref.shape[0]
    pltpu.sync_copy(init_ref, inv_ref)

    def put(a, c):
        inv_ref[dest_ref[a]] = a
        return c

    lax.fori_loop(0, n_asg, put, 0, unroll=16)


def _index_tiles_kernel(inv_ref, pk_ref, *, n_tok):
    nb = pk_ref.shape[0] // SUBLANES
    per_blk = R_BLK // LANES
    assert per_blk == 2
    pk_ref[...] = jnp.zeros(pk_ref.shape, I32)
    for half in range(per_blk):
        a = inv_ref[pl.ds(half, nb, stride=per_blk), :]
        t = jnp.bitwise_and(a, n_tok - 1)
        pk_ref[pl.ds(half, nb, stride=SUBLANES), :] = lax.shift_left(lax.shift_right_logical(t, 1), 3)
        pk_ref[pl.ds(2 + half, nb, stride=SUBLANES), :] = lax.shift_left(jnp.bitwise_and(a, 1), 4)
        pk_ref[pl.ds(4 + half, nb, stride=SUBLANES), :] = a * Y_TILES


def _plan(e_rows, n_tok, nb):
    R = R_BLK
    n_asg = 2 * n_tok
    rows = n_asg // LANES
    emat = e_rows[0:2].reshape(rows, LANES)
    li = jnp.arange(LANES)
    ux = (li[:, None] < li[None, :]).astype(BF16)
    ones = jnp.ones((LANES, LANES), BF16)
    ri = jnp.arange(rows)
    lx = (ri[:, None] > ri[None, :]).astype(BF16)
    dest, bstart, nblk = pl.pallas_call(
        _plan_kernel,
        out_shape=[jax.ShapeDtypeStruct((rows, LANES), I32),
                   jax.ShapeDtypeStruct((SUBLANES, LANES), I32),
                   jax.ShapeDtypeStruct((SUBLANES, LANES), I32)],
        name="plan",
    )(emat, ux, ones, lx)
    pos = jnp.arange(nb * R, dtype=I32)
    init = n_asg + (pos & (R - 1)) + jnp.where(pos >= (nb - 1) * R, R, 0)
    inv = pl.pallas_call(
        _invert_kernel,
        in_specs=[pl.BlockSpec(memory_space=pltpu.SMEM), pl.BlockSpec(memory_space=pl.ANY)],
        out_specs=pl.BlockSpec(memory_space=pltpu.SMEM),
        out_shape=jax.ShapeDtypeStruct((nb * R,), I32),
        name="invert",
    )(dest.reshape(n_asg), init)
    pk = pl.pallas_call(
        functools.partial(_index_tiles_kernel, n_tok=n_tok),
        out_shape=jax.ShapeDtypeStruct((nb * SUBLANES, LANES), I32),
        name="index_tiles",
    )(inv.reshape(nb * R // LANES, LANES))
    return bstart[0, 0:N_EXPERTS], nblk[0, 0:N_EXPERTS], pk.reshape(nb, SUBLANES, LANES)


def kernel(x, norm_mix, w_in, ssd_conv_w, ssd_conv_b, dt_bias, a_log, d_skip, ssd_norm, sc_conv_w,
           sc_norm, w_out, norm_ffn, w_router_group, w_router_expert, w_gate, w_up, w_down, final_norm):
    B, L, D = x.shape
    n_tok = B * L
    depth = norm_mix.shape[0]
    assert depth == 1 and D == D_MODEL and (n_tok & (n_tok - 1)) == 0
    nb = -(-((2 * n_tok) // R_BLK + N_EXPERTS + 1) // SUBLANES) * SUBLANES

    o1 = SSD_WIDTH
    o2 = o1 + XBC
    o3 = o2 + N_HEADS
    wi = w_in.reshape(D, -1)
    wa = wi[:, 0:o2].astype(BF16)
    wb = wi[:, o3:].astype(BF16)
    wdt = jnp.pad(wi[:, o2:o3], ((0, 0), (0, LANES - N_HEADS))).astype(BF16)
    pad_h = (0, LANES - N_HEADS)
    dtb = jnp.pad(dt_bias[0], pad_h).reshape(1, LANES)
    alog = jnp.pad(a_log[0], pad_h).reshape(1, LANES)
    dskip = jnp.repeat(d_skip[0], HEAD_DIM).reshape(1, SSD_WIDTH)

    wre = jnp.transpose(w_router_expert[0], (1, 0, 2)).reshape(D, N_EXPERTS)
    wrt = jnp.pad(jnp.concatenate([w_router_group[0], wre], axis=1).T,
                  ((0, ROUTER_ROWS - N_GROUPS - N_EXPERTS), (0, 0)))
    wrt_hi = wrt.astype(BF16)
    wr = jnp.concatenate([wrt_hi, (wrt - wrt_hi.astype(F32)).astype(BF16)], axis=0)

    ri = jnp.arange(T_MIX)
    tri = (ri[:, None] >= ri[None, :]).astype(BF16)
    er = jnp.arange(LANES)
    ec = jnp.arange(SSD_WIDTH)
    e3 = ((er[:, None] < 48) & ((er[:, None] % 16) == (ec[None, :] // HEAD_DIM))).astype(BF16)
    gsum = ((ec[:, None] // HEAD_DIM) == er[None, :]).astype(BF16)

    h, hn2, e_rows, gates = _mixer_call(
        x, norm_mix[0].reshape(1, D), wa, wb, wdt, ssd_conv_w[0], ssd_conv_b[0].reshape(1, XBC), dtb, alog,
        dskip, ssd_norm[0].reshape(1, SSD_WIDTH), sc_conv_w[0], sc_norm[0].reshape(1, SC_WIDTH),
        w_out.reshape(-1, D).astype(BF16), norm_ffn[0].reshape(1, D), wr, tri, e3, gsum)

    bstart, nblk, pk = _plan(e_rows, n_tok, nb)
    y2 = _moe_call(bstart, nblk, pk, hn2, w_gate.reshape(N_EXPERTS, D, D_FF),
                   w_up.reshape(N_EXPERTS, D, D_FF), w_down.reshape(N_EXPERTS, D_FF, D), n_tok)
    out = _combine_call(h.reshape(n_tok, D), y2, gates, final_norm.reshape(1, D))
    return out.reshape(B, L, D)
```

```python
import functools

import jax
import jax.numpy as jnp
from jax import lax
from jax.experimental import pallas as pl
from jax.experimental.pallas import tpu as pltpu

F32 = jnp.float32
BF16 = jnp.bfloat16
I32 = jnp.int32

EPS = 1e-6
D_MODEL = 1024
N_HEADS = 16
HEAD_DIM = 64
N_BC_GROUPS = 2
STATE = 128
SSD_WIDTH = 1024
XBC = SSD_WIDTH + 2 * N_BC_GROUPS * STATE
SC_WIDTH = 1024
SC_GROUPS = 16
N_GROUPS = 4
EPG = 8
N_EXPERTS = 32
D_FF = 512

LANES = 128
SUBLANES = 8
ROW_TILES = D_MODEL // LANES
Y_TILES = ROW_TILES // 2

T_MIX = 512
Q_SSD = 128
PAIR_ROWS = T_MIX // 2
R_BLK = 256
R_SHIFT = 8
ROUTER_ROWS = 48
N_SLOTS = 4
T_CMB = 1024

VMEM_LIMIT = 56 * 1024 * 1024


def _dot(a, b):
    return jnp.dot(a, b, preferred_element_type=F32)


def _split3(v):
    p1 = v.astype(BF16).astype(F32)
    r1 = v - p1
    p2 = r1.astype(BF16).astype(F32)
    p3 = (r1 - p2).astype(BF16).astype(F32)
    return p1, p2, p3


def _pack3(v):
    p1, p2, p3 = _split3(v)
    return (p1 + pltpu.roll(p2, 16, 1) + pltpu.roll(p3, 32, 1)).astype(BF16)


def _silu(v):
    return v * jax.nn.sigmoid(v)


def _mixer_kernel(x_ref, gmix_ref, wa_ref, wb_ref, wdt_ref, convw_ref, convb_ref, dtb_ref, alog_ref,
                  dskip_ref, ssdn_ref, scw_ref, scn_ref, wout_ref, gffn_ref, wr_ref,
                  tri_ref, e3_ref, gsum_ref,
                  h_ref, hn2_ref, e_ref, g_ref,
                  cbuf, sbuf, st_ref, pbuf):
    T = T_MIX
    t = pl.program_id(1)

    @pl.when(t == 0)
    def _():
        cbuf[0:8, :] = jnp.zeros((8, XBC), F32)
        sbuf[0:8, :] = jnp.zeros((8, SC_WIDTH), F32)
        st_ref[...] = jnp.zeros(st_ref.shape, F32)

    @pl.when(t > 0)
    def _():
        cbuf[0:8, :] = cbuf[T:T + 8, :]
        sbuf[0:8, :] = sbuf[T:T + 8, :]

    x = x_ref[0]
    ms = jnp.mean(x * x, axis=-1, keepdims=True)
    hn = (x * lax.rsqrt(ms + EPS) * gmix_ref[...]).astype(BF16)

    z = _dot(hn, wa_ref[:, 0:1024])
    xbc = _dot(hn, wa_ref[:, 1024:2560])
    scb = _dot(hn, wb_ref[:, 0:1024])
    scc = _dot(hn, wb_ref[:, 1024:2048])
    scv = _dot(hn, wb_ref[:, 2048:3072])
    dt_raw = _dot(hn, wdt_ref[...])

    cbuf[8:8 + T, :] = xbc
    cw = convw_ref[...]
    acc = convb_ref[...] + cw[3:4, :] * xbc
    for k in range(3):
        acc = acc + cw[k:k + 1, :] * cbuf[5 + k:5 + k + T, :]
    xact = _silu(acc)
    xs = xact[:, 0:SSD_WIDTH]

    lane = lax.broadcasted_iota(I32, (1, LANES), 1)
    hmask = lane < N_HEADS
    a = jnp.where(hmask, -jnp.exp(alog_ref[...]), 0.0)
    dtv = dt_raw + dtb_ref[...]
    dt = jnp.where(hmask, jnp.maximum(dtv, 0.0) + jnp.log1p(jnp.exp(-jnp.abs(dtv))), 0.0)
    adt = dt * a
    c3 = _dot(tri_ref[...], _pack3(adt))
    ac = jnp.where(hmask, c3 + pltpu.roll(c3, LANES - 16, 1) + pltpu.roll(c3, LANES - 32, 1), 0.0)
    Q = Q_SSD
    n_sub = T // Q
    ends = [ac[(c + 1) * Q - 1:(c + 1) * Q, :] for c in range(n_sub)]
    base = jnp.concatenate([jnp.broadcast_to(ends[c - 1] if c else jnp.zeros_like(ends[0]), (Q, LANES))
                            for c in range(n_sub)], axis=0)
    endv = jnp.concatenate([jnp.broadcast_to(ends[c], (Q, LANES)) for c in range(n_sub)], axis=0)
    eac = jnp.where(hmask, jnp.exp(ac - base), 0.0)
    wdt = dt * jnp.exp(endv - ac)

    stacked = jnp.concatenate([_pack3(dt), _pack3(wdt), _pack3(eac)], axis=0)
    ex = _dot(stacked, e3_ref[...])
    dt_e = ex[0:T]
    wdt_e = ex[T:2 * T]
    eac_e = ex[2 * T:3 * T]
    xdt = (xs * dt_e).astype(BF16)
    xdtw = (xs * wdt_e).astype(BF16)

    ac_rows = ac.T
    rr = lax.broadcasted_iota(I32, (Q, Q), 0)
    cc = lax.broadcasted_iota(I32, (Q, Q), 1)
    causal = rr >= cc
    cblk = lax.shift_right_logical(lax.broadcasted_iota(I32, (Q, 4 * HEAD_DIM), 1), 6)

    y_cols = []
    for g in range(N_BC_GROUPS):
        bg = xact[:, SSD_WIDTH + STATE * g:SSD_WIDTH + STATE * (g + 1)]
        cg = xact[:, SSD_WIDTH + 2 * STATE + STATE * g:SSD_WIDTH + 2 * STATE + STATE * (g + 1)]
        st = st_ref[g]
        y_rows = [[], []]
        for c in range(n_sub):
            r0, r1 = c * Q, (c + 1) * Q
            bb = bg[r0:r1].astype(BF16)
            cbf = cg[r0:r1].astype(BF16)
            cb = lax.dot_general(cbf, bb, (((1,), (1,)), ((), ())), preferred_element_type=F32)
            yoff = _dot(cbf, st.astype(BF16))
            dec = eac_e[r1 - 1:r1, 512 * g:512 * (g + 1)]
            bgt = bg[r0:r1].T.astype(BF16)
            st = st * dec + _dot(bgt, xdtw[r0:r1, 512 * g:512 * (g + 1)])
            for q in range(2):
                ms_list = []
                for r in range(4):
                    hh = 8 * g + 4 * q + r
                    seg = ac[r0:r1, hh:hh + 1] - ac_rows[hh:hh + 1, r0:r1]
                    lh = jnp.exp(jnp.where(causal, seg, -jnp.inf))
                    ms_list.append((cb * lh).astype(BF16))
                lhs = jnp.concatenate(ms_list, axis=1)
                lo = 512 * g + 256 * q
                x4 = xdt[r0:r1, lo:lo + 256]
                rhs = jnp.concatenate(
                    [jnp.where(cblk == r, x4, jnp.zeros_like(x4)) for r in range(4)], axis=0)
                yd = _dot(lhs, rhs)
                y_rows[q].append(yd + eac_e[r0:r1, lo:lo + 256] * yoff[:, 256 * q:256 * (q + 1)])
        st_ref[g] = st
        y_cols += [jnp.concatenate(y_rows[0], axis=0), jnp.concatenate(y_rows[1], axis=0)]
    y = jnp.concatenate(y_cols, axis=1) + dskip_ref[...] * xs

    v = y * _silu(z)
    outs = []
    for g in range(N_BC_GROUPS):
        vg = v[:, 512 * g:512 * (g + 1)]
        msg = jnp.mean(vg * vg, axis=-1, keepdims=True)
        outs.append(vg * lax.rsqrt(msg + EPS))
    ssd_out = (jnp.concatenate(outs, axis=1) * ssdn_ref[...]).astype(BF16)

    u = scc * scv
    sbuf[8:8 + T, :] = u
    sw = scw_ref[...]
    conv = sw[2:3, :] * u + sw[1:2, :] * sbuf[7:7 + T, :] + sw[0:1, :] * sbuf[6:6 + T, :]
    sc = scb * conv
    gs = _dot((sc * sc).astype(BF16), gsum_ref[...])
    rstd = jnp.where(hmask, lax.rsqrt(gs * (1.0 / HEAD_DIM) + EPS), 0.0)
    rstd_e = _dot(_pack3(rstd), e3_ref[...])
    sc_out = (sc * rstd_e * scn_ref[...]).astype(BF16)

    mix = _dot(ssd_out, wout_ref[0:1024, :]) + _dot(sc_out, wout_ref[1024:2048, :])
    h = x + mix
    h_ref[0] = h

    ms2 = jnp.mean(h * h, axis=-1, keepdims=True)
    hn2 = h * lax.rsqrt(ms2 + EPS) * gffn_ref[...]
    hi = hn2.astype(BF16)
    bits = pltpu.bitcast(hi.astype(F32), jnp.uint32)
    for s in range(ROW_TILES):
        pbuf[pl.ds(s * T, T), :] = bits[:, LANES * s:LANES * (s + 1)]
    for s in range(ROW_TILES):
        even = pbuf[pl.ds(s * T, PAIR_ROWS, stride=2), :]
        odd = pbuf[pl.ds(s * T + 1, PAIR_ROWS, stride=2), :]
        hn2_ref[pl.ds(s, PAIR_ROWS, stride=ROW_TILES), :] = jnp.bitwise_or(
            jnp.right_shift(even, jnp.uint32(16)), odd)
    lo_ = (hn2 - hi.astype(F32)).astype(BF16)
    RT = ROUTER_ROWS
    nt_dims = (((1,), (1,)), ((), ()))
    both = lax.dot_general(wr_ref[...], hi, nt_dims, preferred_element_type=F32)
    low = lax.dot_general(wr_ref[0:RT, :], lo_, nt_dims, preferred_element_type=F32)
    logits = both[0:RT] + (low + both[RT:2 * RT])

    ri = lax.broadcasted_iota(I32, (RT, T), 0)
    ri_f = ri.astype(F32)
    neg = -jnp.inf
    big = 1e9
    gl = jnp.where(ri < N_GROUPS, logits, neg)
    gmax = jnp.max(gl, axis=0, keepdims=True)
    gidx = jnp.min(jnp.where(gl == gmax, ri_f, big), axis=0, keepdims=True)
    gsum = jnp.sum(jnp.where(ri < N_GROUPS, jnp.exp(logits - gmax), 0.0), axis=0, keepdims=True)
    gw = 1.0 / gsum
    egrp = lax.shift_right_logical(ri - N_GROUPS, 3).astype(F32)
    in_grp = (ri >= N_GROUPS) & (ri < N_GROUPS + N_EXPERTS) & (egrp == gidx)
    el = jnp.where(in_grp, logits, neg)
    v1 = jnp.max(el, axis=0, keepdims=True)
    i1 = jnp.min(jnp.where(el == v1, ri_f, big), axis=0, keepdims=True)
    el2 = jnp.where(ri_f == i1, neg, el)
    v2 = jnp.max(el2, axis=0, keepdims=True)
    i2 = jnp.min(jnp.where(el2 == v2, ri_f, big), axis=0, keepdims=True)
    p = jnp.exp(v2 - v1)
    s1 = 1.0 / (1.0 + p)
    gate1 = gw * s1
    gate2 = gw * (p * s1)
    r8 = lax.broadcasted_iota(I32, (SUBLANES, T), 0)
    e_ref[...] = jnp.where(r8 == 0, i1 - N_GROUPS, jnp.where(r8 == 1, i2 - N_GROUPS, 0.0)).astype(I32)
    rl = lax.broadcasted_iota(I32, (LANES, T), 0)
    g_ref[...] = jnp.where(rl == 0, gate1, jnp.where(rl == 1, gate2, 0.0)).T


def _mixer_call(x, gmix, wa, wb, wdt, convw, convb, dtb, alog, dskip, ssdn, scw, scn, wout, gffn,
                wr, tri, e3, gsum):
    B, L, D = x.shape
    T = T_MIX
    nt = L // T
    n_tok = B * L

    def const(shape):
        return pl.BlockSpec(shape, lambda b, t: (0,) * len(shape), pipeline_mode=pl.Buffered(1))

    in_specs = [
        pl.BlockSpec((1, T, D), lambda b, t: (b, t, 0)),
        const((1, D)),
        const(wa.shape), const(wb.shape), const(wdt.shape), const(convw.shape), const(convb.shape),
        const(dtb.shape), const(alog.shape), const(dskip.shape), const(ssdn.shape),
        const(scw.shape), const(scn.shape), const(wout.shape), const(gffn.shape),
        const(wr.shape), const(tri.shape), const(e3.shape), const(gsum.shape),
    ]
    out_shape = [
        jax.ShapeDtypeStruct((B, L, D), F32),
        jax.ShapeDtypeStruct((n_tok // 2 * ROW_TILES, LANES), jnp.uint32),
        jax.ShapeDtypeStruct((8, n_tok), I32),
        jax.ShapeDtypeStruct((n_tok, LANES), F32),
    ]
    out_specs = [
        pl.BlockSpec((1, T, D), lambda b, t: (b, t, 0)),
        pl.BlockSpec((PAIR_ROWS * ROW_TILES, LANES), lambda b, t: (b * nt + t, 0)),
        pl.BlockSpec((8, T), lambda b, t: (0, b * nt + t)),
        pl.BlockSpec((T, LANES), lambda b, t: (b * nt + t, 0)),
    ]
    return pl.pallas_call(
        _mixer_kernel,
        grid=(B, nt),
        in_specs=in_specs,
        out_specs=out_specs,
        out_shape=out_shape,
        scratch_shapes=[
            pltpu.VMEM((T + 8, XBC), F32),
            pltpu.VMEM((T + 8, SC_WIDTH), F32),
            pltpu.VMEM((N_BC_GROUPS, STATE, 512), F32),
            pltpu.VMEM((ROW_TILES * T, LANES), jnp.uint32),
        ],
        compiler_params=pltpu.CompilerParams(
            dimension_semantics=("arbitrary", "arbitrary"),
            vmem_limit_bytes=VMEM_LIMIT),
        name="mixer",
    )(x, gmix, wa, wb, wdt, convw, convb, dtb, alog, dskip, ssdn, scw, scn, wout, gffn, wr,
      tri, e3, gsum)


def _moe_kernel(bstart_ref, nblk_ref, pk_ref, hn2p_ref, wg_ref, wu_ref, wd_ref,
                y2_ref, hn2v, xbuf, ybuf, wgu, wdb, idx_ref, lsem, ssem, isem, *, n_tok):
    R = R_BLK
    slab = R * ROW_TILES
    e = pl.program_id(0)
    g0 = bstart_ref[e]
    nb_e = nblk_ref[e]

    def idx_word(sl, field, j):
        return idx_ref[sl, 2 * field + j // LANES, j % LANES]

    def idx_copy(g, sl):
        return pltpu.make_async_copy(pk_ref.at[g], idx_ref.at[sl], isem.at[sl])

    def gather_rows(sl, xs):
        for j in range(R):
            row = pl.multiple_of(idx_word(sl, 0, j), ROW_TILES)
            sh = idx_word(sl, 1, j).astype(jnp.uint32)
            w = hn2v[pl.ds(row, ROW_TILES), :]
            w = jnp.left_shift(jnp.right_shift(w, sh), jnp.uint32(16))
            xbuf[pl.ds(xs * slab + j * ROW_TILES, ROW_TILES), :] = pltpu.bitcast(w, F32)

    yslab = R * Y_TILES
    FH = D_FF // 2

    def y_copy(sl, j, dst_row):
        return pltpu.make_async_copy(
            ybuf.at[pl.ds(sl * yslab + j * Y_TILES, Y_TILES), :],
            y2_ref.at[pl.ds(pl.multiple_of(dst_row, Y_TILES), Y_TILES), :],
            ssem.at[sl])

    def scatter_start(sl, lo, hi):
        for j in range(lo, hi):
            y_copy(sl, j, idx_word(sl, 2, j)).start(priority=j % 2)

    def slab_wait(sl):
        view = ybuf.at[pl.ds(sl * yslab, yslab), :]
        pltpu.make_async_copy(view, view, ssem.at[sl]).wait()

    n_idx_blocks = pk_ref.shape[0]

    @pl.when(e == 0)
    def _():
        load = pltpu.make_async_copy(hn2p_ref, hn2v, lsem.at[0])
        load.start()
        first = idx_copy(0, 0)
        first.start()
        idx_copy(1, 1).start()
        prev = idx_copy(n_idx_blocks - 1, N_SLOTS - 1)
        prev.start()
        ybuf[...] = jnp.zeros(ybuf.shape, ybuf.dtype)
        for sl in range(N_SLOTS - 1):
            base = 2 * n_tok + (sl + 1 if sl else 0) * R
            for j in range(R):
                y_copy(sl, j, (base + j) * Y_TILES).start(priority=j % 2)
        prev.wait()
        first.wait()
        load.wait()
        gather_rows(0, 0)

    @pl.when(nb_e > 0)
    def _():
        for j in range(2):
            wgu[:, 2 * j * FH:(2 * j + 1) * FH] = wg_ref[0, :, j * FH:(j + 1) * FH].astype(BF16)
            wgu[:, (2 * j + 1) * FH:(2 * j + 2) * FH] = wu_ref[0, :, j * FH:(j + 1) * FH].astype(BF16)
        wdb[...] = wd_ref[0].astype(BF16)

    c1, c2 = R // 3, 2 * (R // 3)

    def run_block(g, sl):
        prv = (sl + N_SLOTS - 1) % N_SLOTS
        nx1 = (sl + 1) % N_SLOTS
        nx2 = (sl + 2) % N_SLOTS
        idx_copy(g + 1, nx1).wait()
        idx_copy(g + 2, nx2).start()
        slab_wait(sl)
        x = jnp.concatenate(
            [xbuf[pl.ds((sl % 2) * slab + s, R, stride=ROW_TILES), :] for s in range(ROW_TILES)],
            axis=1).astype(BF16)
        scatter_start(prv, 0, c1)
        gu = _dot(x, wgu[...])
        gather_rows(nx1, (sl + 1) % 2)
        scatter_start(prv, c1, c2)
        act = jnp.concatenate(
            [_silu(gu[:, 2 * j * FH:(2 * j + 1) * FH]) * gu[:, (2 * j + 1) * FH:(2 * j + 2) * FH]
             for j in range(2)], axis=1).astype(BF16)
        scatter_start(prv, c2, R)
        yy = _dot(act, wdb[...])
        for s in range(Y_TILES):
            lo_w = pltpu.bitcast(yy[:, 2 * LANES * s:2 * LANES * s + LANES].astype(BF16).astype(F32), jnp.uint32)
            hi_w = pltpu.bitcast(yy[:, 2 * LANES * s + LANES:2 * LANES * (s + 1)].astype(BF16).astype(F32),
                                 jnp.uint32)
            ybuf[pl.ds(sl * yslab + s, R, stride=Y_TILES), :] = jnp.bitwise_or(
                jnp.right_shift(lo_w, jnp.uint32(16)), hi_w)

    def block(i, carry):
        g = g0 + i
        sl = lax.rem(g, N_SLOTS)
        for k in range(N_SLOTS):
            @pl.when(sl == k)
            def _(k=k):
                run_block(g, k)
        return carry

    lax.fori_loop(0, nb_e, block, 0)

    @pl.when(e == pl.num_programs(0) - 1)
    def _():
        g_end = g0 + nb_e
        sl_end = lax.rem(g_end, N_SLOTS)
        for k in range(N_SLOTS):
            @pl.when(sl_end == k)
            def _(k=k):
                scatter_start((k + N_SLOTS - 1) % N_SLOTS, 0, R)
                idx_copy(g_end + 1, (k + 1) % N_SLOTS).wait()
        for k in range(N_SLOTS):
            slab_wait(k)


def _moe_call(bstart, nblk, pk, hn2p, wg, wu, wd, n_tok):
    R = R_BLK
    slab = R * ROW_TILES
    grid_spec = pltpu.PrefetchScalarGridSpec(
        num_scalar_prefetch=2,
        grid=(N_EXPERTS,),
        in_specs=[
            pl.BlockSpec(memory_space=pl.ANY),
            pl.BlockSpec(memory_space=pl.ANY),
            pl.BlockSpec((1, D_MODEL, D_FF), lambda e, bs, nb: (e, 0, 0)),
            pl.BlockSpec((1, D_MODEL, D_FF), lambda e, bs, nb: (e, 0, 0)),
            pl.BlockSpec((1, D_FF, D_MODEL), lambda e, bs, nb: (e, 0, 0)),
        ],
        out_specs=pl.BlockSpec(memory_space=pl.ANY),
        scratch_shapes=[
            pltpu.VMEM(hn2p.shape, jnp.uint32),
            pltpu.VMEM((2 * slab, LANES), F32),
            pltpu.VMEM((N_SLOTS * R * Y_TILES, LANES), jnp.uint32),
            pltpu.VMEM((D_MODEL, 2 * D_FF), BF16),
            pltpu.VMEM((D_FF, D_MODEL), BF16),
            pltpu.SMEM((N_SLOTS, SUBLANES, LANES), I32),
            pltpu.SemaphoreType.DMA((1,)),
            pltpu.SemaphoreType.DMA((N_SLOTS,)),
            pltpu.SemaphoreType.DMA((N_SLOTS,)),
        ],
    )
    return pl.pallas_call(
        functools.partial(_moe_kernel, n_tok=n_tok),
        grid_spec=grid_spec,
        out_shape=jax.ShapeDtypeStruct(((2 * n_tok + N_SLOTS * R) * Y_TILES, LANES), jnp.uint32),
        compiler_params=pltpu.CompilerParams(
            dimension_semantics=("arbitrary",),
            vmem_limit_bytes=VMEM_LIMIT),
        name="moe",
    )(bstart, nblk, pk, hn2p, wg, wu, wd)


def _unpack_rows(y_ref, rows):
    parts = []
    for s in range(Y_TILES):
        w = y_ref[pl.ds(s, rows, stride=Y_TILES), :]
        parts.append(pltpu.bitcast(jnp.left_shift(w, jnp.uint32(16)), F32))
        parts.append(pltpu.bitcast(jnp.bitwise_and(w, jnp.uint32(0xFFFF0000)), F32))
    return jnp.concatenate(parts, axis=1)


def _combine_kernel(h_ref, y0_ref, y1_ref, g_ref, fn_ref, o_ref):
    T = T_CMB
    h = h_ref[...]
    y0 = _unpack_rows(y0_ref, T)
    y1 = _unpack_rows(y1_ref, T)
    g = g_ref[...]
    v = h + (y0 * g[:, 0:1] + y1 * g[:, 1:2])
    ms = jnp.mean(v * v, axis=-1, keepdims=True)
    o_ref[...] = v * lax.rsqrt(ms + EPS) * fn_ref[...]


def _combine_call(h2d, y2, gates, fnorm):
    n_tok, D = h2d.shape
    T = T_CMB
    nt = n_tok // T
    return pl.pallas_call(
        _combine_kernel,
        grid=(nt,),
        in_specs=[
            pl.BlockSpec((T, D), lambda i: (i, 0)),
            pl.BlockSpec((T * Y_TILES, LANES), lambda i: (i, 0)),
            pl.BlockSpec((T * Y_TILES, LANES), lambda i: (nt + i, 0)),
            pl.BlockSpec((T, LANES), lambda i: (i, 0)),
            pl.BlockSpec((1, D), lambda i: (0, 0)),
        ],
        out_specs=pl.BlockSpec((T, D), lambda i: (i, 0)),
        out_shape=jax.ShapeDtypeStruct((n_tok, D), F32),
        compiler_params=pltpu.CompilerParams(dimension_semantics=("arbitrary",)),
        name="combine",
    )(h2d, y2, y2, gates, fnorm)


def _plan_kernel(e_ref, ux_ref, ones_ref, lx_ref, dest_ref, bstart_ref, nblk_ref):
    rows = e_ref.shape[0]
    ev = e_ref[...]
    lane8 = lax.broadcasted_iota(I32, (SUBLANES, LANES), 1)
    dest = jnp.zeros((rows, LANES), F32)
    bstart = jnp.zeros((SUBLANES, LANES), I32)
    nblk = jnp.zeros((SUBLANES, LANES), I32)
    pstart = jnp.zeros((1, LANES), F32)
    for e in range(N_EXPERTS):
        ohb = ev == e
        oh = jnp.where(ohb, 1.0, 0.0).astype(BF16)
        within = _dot(oh, ux_ref[...])
        rtot = _dot(oh, ones_ref[...])
        rpre = _dot(lx_ref[...], rtot.astype(BF16))
        cnt = rpre[rows - 1:rows, :] + rtot[rows - 1:rows, :]
        dest = dest + jnp.where(ohb, within + rpre + pstart, 0.0)
        cnt_i = cnt.astype(I32)
        nb_e = lax.shift_right_logical(cnt_i + (R_BLK - 1), R_SHIFT)
        bstart = jnp.where(lane8 == e, lax.shift_right_logical(pstart.astype(I32), R_SHIFT), bstart)
        nblk = jnp.where(lane8 == e, nb_e, nblk)
        pstart = pstart + lax.shift_left(nb_e, R_SHIFT).astype(F32)
    dest_ref[...] = dest.astype(I32)
    bstart_ref[...] = bstart
    nblk_ref[...] = nblk


def _invert_kernel(dest_ref, init_ref, inv_ref):
    n_asg = dest_ref.shape[0]
    pltpu.sync_copy(init_ref, inv_ref)

    def put(a, c):
        inv_ref[dest_ref[a]] = a
        return c

    lax.fori_loop(0, n_asg, put, 0, unroll=16)


def _index_tiles_kernel(inv_ref, pk_ref, *, n_tok):
    nb = pk_ref.shape[0] // SUBLANES
    per_blk = R_BLK // LANES
    assert per_blk == 2
    pk_ref[...] = jnp.zeros(pk_ref.shape, I32)
    for half in range(per_blk):
        a = inv_ref[pl.ds(half, nb, stride=per_blk), :]
        t = jnp.bitwise_and(a, n_tok - 1)
        pk_ref[pl.ds(half, nb, stride=SUBLANES), :] = lax.shift_left(lax.shift_right_logical(t, 1), 3)
        pk_ref[pl.ds(2 + half, nb, stride=SUBLANES), :] = lax.shift_left(jnp.bitwise_and(a, 1), 4)
        pk_ref[pl.ds(4 + half, nb, stride=SUBLANES), :] = a * Y_TILES


def _plan(e_rows, n_tok, nb):
    R = R_BLK
    n_asg = 2 * n_tok
    rows = n_asg // LANES
    emat = e_rows[0:2].reshape(rows, LANES)
    li = jnp.arange(LANES)
    ux = (li[:, None] < li[None, :]).astype(BF16)
    ones = jnp.ones((LANES, LANES), BF16)
    ri = jnp.arange(rows)
    lx = (ri[:, None] > ri[None, :]).astype(BF16)
    dest, bstart, nblk = pl.pallas_call(
        _plan_kernel,
        out_shape=[jax.ShapeDtypeStruct((rows, LANES), I32),
                   jax.ShapeDtypeStruct((SUBLANES, LANES), I32),
                   jax.ShapeDtypeStruct((SUBLANES, LANES), I32)],
        name="plan",
    )(emat, ux, ones, lx)
    pos = jnp.arange(nb * R, dtype=I32)
    init = n_asg + (pos & (R - 1)) + jnp.where(pos >= (nb - 1) * R, R, 0)
    inv = pl.pallas_call(
        _invert_kernel,
        in_specs=[pl.BlockSpec(memory_space=pltpu.SMEM), pl.BlockSpec(memory_space=pl.ANY)],
        out_specs=pl.BlockSpec(memory_space=pltpu.SMEM),
        out_shape=jax.ShapeDtypeStruct((nb * R,), I32),
        name="invert",
    )(dest.reshape(n_asg), init)
    pk = pl.pallas_call(
        functools.partial(_index_tiles_kernel, n_tok=n_tok),
        out_shape=jax.ShapeDtypeStruct((nb * SUBLANES, LANES), I32),
        name="index_tiles",
    )(inv.reshape(nb * R // LANES, LANES))
    return bstart[0, 0:N_EXPERTS], nblk[0, 0:N_EXPERTS], pk.reshape(nb, SUBLANES, LANES)


def kernel(x, norm_mix, w_in, ssd_conv_w, ssd_conv_b, dt_bias, a_log, d_skip, ssd_norm, sc_conv_w,
           sc_norm, w_out, norm_ffn, w_router_group, w_router_expert, w_gate, w_up, w_down, final_norm):
    B, L, D = x.shape
    n_tok = B * L
    depth = norm_mix.shape[0]
    assert depth == 1 and D == D_MODEL and (n_tok & (n_tok - 1)) == 0
    nb = -(-((2 * n_tok) // R_BLK + N_EXPERTS + 1) // SUBLANES) * SUBLANES

    o1 = SSD_WIDTH
    o2 = o1 + XBC
    o3 = o2 + N_HEADS
    wi = w_in.reshape(D, -1)
    wa = wi[:, 0:o2].astype(BF16)
    wb = wi[:, o3:].astype(BF16)
    wdt = jnp.pad(wi[:, o2:o3], ((0, 0), (0, LANES - N_HEADS))).astype(BF16)
    pad_h = (0, LANES - N_HEADS)
    dtb = jnp.pad(dt_bias[0], pad_h).reshape(1, LANES)
    alog = jnp.pad(a_log[0], pad_h).reshape(1, LANES)
    dskip = jnp.repeat(d_skip[0], HEAD_DIM).reshape(1, SSD_WIDTH)

    wre = jnp.transpose(w_router_expert[0], (1, 0, 2)).reshape(D, N_EXPERTS)
    wrt = jnp.pad(jnp.concatenate([w_router_group[0], wre], axis=1).T,
                  ((0, ROUTER_ROWS - N_GROUPS - N_EXPERTS), (0, 0)))
    wrt_hi = wrt.astype(BF16)
    wr = jnp.concatenate([wrt_hi, (wrt - wrt_hi.astype(F32)).astype(BF16)], axis=0)

    ri = jnp.arange(T_MIX)
    tri = (ri[:, None] >= ri[None, :]).astype(BF16)
    er = jnp.arange(LANES)
    ec = jnp.arange(SSD_WIDTH)
    e3 = ((er[:, None] < 48) & ((er[:, None] % 16) == (ec[None, :] // HEAD_DIM))).astype(BF16)
    gsum = ((ec[:, None] // HEAD_DIM) == er[None, :]).astype(BF16)

    h, hn2, e_rows, gates = _mixer_call(
        x, norm_mix[0].reshape(1, D), wa, wb, wdt, ssd_conv_w[0], ssd_conv_b[0].reshape(1, XBC), dtb, alog,
        dskip, ssd_norm[0].reshape(1, SSD_WIDTH), sc_conv_w[0], sc_norm[0].reshape(1, SC_WIDTH),
        w_out.reshape(-1, D).astype(BF16), norm_ffn[0].reshape(1, D), wr, tri, e3, gsum)

    bstart, nblk, pk = _plan(e_rows, n_tok, nb)
    y2 = _moe_call(bstart, nblk, pk, hn2, w_gate.reshape(N_EXPERTS, D, D_FF),
                   w_up.reshape(N_EXPERTS, D, D_FF), w_down.reshape(N_EXPERTS, D_FF, D), n_tok)
    out = _combine_call(h.reshape(n_tok, D), y2, gates, final_norm.reshape(1, D))
    return out.reshape(B, L, D)
```

```python
import functools

import jax
import jax.numpy as jnp
from jax import lax
from jax.experimental import pallas as pl
from jax.experimental.pallas import tpu as pltpu

F32 = jnp.float32
BF16 = jnp.bfloat16
I32 = jnp.int32

EPS = 1e-6
D_MODEL = 1024
N_HEADS = 16
HEAD_DIM = 64
N_BC_GROUPS = 2
STATE = 128
SSD_WIDTH = 1024
XBC = SSD_WIDTH + 2 * N_BC_GROUPS * STATE
SC_WIDTH = 1024
SC_GROUPS = 16
N_GROUPS = 4
EPG = 8
N_EXPERTS = 32
D_FF = 512

LANES = 128
SUBLANES = 8
ROW_TILES = D_MODEL // LANES
Y_TILES = ROW_TILES // 2

T_MIX = 512
Q_SSD = 128
COL_CHUNK = 512
PAIR_ROWS = T_MIX // 2
R_BLK = 256
R_SHIFT = 8
ROUTER_ROWS = 48
N_SLOTS = 4
T_CMB = 1024

VMEM_LIMIT = 56 * 1024 * 1024


def _dot(a, b):
    return jnp.dot(a, b, preferred_element_type=F32)


def _split3(v):
    p1 = v.astype(BF16).astype(F32)
    r1 = v - p1
    p2 = r1.astype(BF16).astype(F32)
    p3 = (r1 - p2).astype(BF16).astype(F32)
    return p1, p2, p3


def _pack3(v):
    p1, p2, p3 = _split3(v)
    return (p1 + pltpu.roll(p2, 16, 1) + pltpu.roll(p3, 32, 1)).astype(BF16)


def _silu(v):
    return v * jax.nn.sigmoid(v)


def _mixer_kernel(x_ref, gmix_ref, wa_ref, wb_ref, wdt_ref, convw_ref, convb_ref, dtb_ref, alog_ref,
                  dskip_ref, ssdn_ref, scw_ref, scn_ref, wout_ref, gffn_ref, wr_ref,
                  tri_ref, e3_ref, gsum_ref,
                  h_ref, hn2_ref, e_ref, g_ref,
                  cbuf, sbuf, st_ref, pbuf):
    T = T_MIX
    t = pl.program_id(1)

    @pl.when(t == 0)
    def _():
        cbuf[0:8, :] = jnp.zeros((8, XBC), F32)
        sbuf[0:8, :] = jnp.zeros((8, SC_WIDTH), F32)
        st_ref[...] = jnp.zeros(st_ref.shape, F32)

    @pl.when(t > 0)
    def _():
        cbuf[0:8, :] = cbuf[T:T + 8, :]
        sbuf[0:8, :] = sbuf[T:T + 8, :]

    x = x_ref[0]
    ms = jnp.mean(x * x, axis=-1, keepdims=True)
    hn = (x * lax.rsqrt(ms + EPS) * gmix_ref[...]).astype(BF16)

    dt_raw = _dot(hn, wdt_ref[...])

    cw = convw_ref[...]
    xact_parts = []
    for c0 in range(0, XBC, COL_CHUNK):
        c1 = c0 + COL_CHUNK
        xbc = _dot(hn, wa_ref[:, 1024 + c0:1024 + c1])
        cbuf[8:8 + T, c0:c1] = xbc
        acc = convb_ref[:, c0:c1] + cw[3:4, c0:c1] * xbc
        for k in range(3):
            acc = acc + cw[k:k + 1, c0:c1] * cbuf[5 + k:5 + k + T, c0:c1]
        xact_parts.append(_silu(acc))
    xact = jnp.concatenate(xact_parts, axis=1)
    xs = xact[:, 0:SSD_WIDTH]

    lane = lax.broadcasted_iota(I32, (1, LANES), 1)
    hmask = lane < N_HEADS
    a = jnp.where(hmask, -jnp.exp(alog_ref[...]), 0.0)
    dtv = dt_raw + dtb_ref[...]
    dt = jnp.where(hmask, jnp.maximum(dtv, 0.0) + jnp.log1p(jnp.exp(-jnp.abs(dtv))), 0.0)
    adt = dt * a

    sw = scw_ref[...]
    sc_parts = []
    for c0 in range(0, SC_WIDTH, COL_CHUNK):
        c1 = c0 + COL_CHUNK
        u = _dot(hn, wb_ref[:, 1024 + c0:1024 + c1]) * _dot(hn, wb_ref[:, 2048 + c0:2048 + c1])
        sbuf[8:8 + T, c0:c1] = u
        conv = (sw[2:3, c0:c1] * u + sw[1:2, c0:c1] * sbuf[7:7 + T, c0:c1]
                + sw[0:1, c0:c1] * sbuf[6:6 + T, c0:c1])
        sc_parts.append(_dot(hn, wb_ref[:, c0:c1]) * conv)
    sc = jnp.concatenate(sc_parts, axis=1)
    z = _dot(hn, wa_ref[:, 0:1024])

    c3 = _dot(tri_ref[...], _pack3(adt))
    ac = jnp.where(hmask, c3 + pltpu.roll(c3, LANES - 16, 1) + pltpu.roll(c3, LANES - 32, 1), 0.0)
    Q = Q_SSD
    n_sub = T // Q
    ends = [ac[(c + 1) * Q - 1:(c + 1) * Q, :] for c in range(n_sub)]
    base = jnp.concatenate([jnp.broadcast_to(ends[c - 1] if c else jnp.zeros_like(ends[0]), (Q, LANES))
                            for c in range(n_sub)], axis=0)
    endv = jnp.concatenate([jnp.broadcast_to(ends[c], (Q, LANES)) for c in range(n_sub)], axis=0)
    eac = jnp.where(hmask, jnp.exp(ac - base), 0.0)
    wdt = dt * jnp.exp(endv - ac)

    stacked = jnp.concatenate([_pack3(dt), _pack3(wdt), _pack3(eac)], axis=0)
    ex = _dot(stacked, e3_ref[...])
    dt_e = ex[0:T]
    wdt_e = ex[T:2 * T]
    eac_e = ex[2 * T:3 * T]
    xdt = (xs * dt_e).astype(BF16)
    xdtw = (xs * wdt_e).astype(BF16)

    ac_rows = ac.T
    rr = lax.broadcasted_iota(I32, (Q, Q), 0)
    cc = lax.broadcasted_iota(I32, (Q, Q), 1)
    causal = rr >= cc
    cblk = lax.shift_right_logical(lax.broadcasted_iota(I32, (Q, 4 * HEAD_DIM), 1), 6)

    y_cols = []
    for g in range(N_BC_GROUPS):
        bg = xact[:, SSD_WIDTH + STATE * g:SSD_WIDTH + STATE * (g + 1)]
        cg = xact[:, SSD_WIDTH + 2 * STATE + STATE * g:SSD_WIDTH + 2 * STATE + STATE * (g + 1)]
        st = st_ref[g]
        y_rows = [[], []]
        for c in range(n_sub):
            r0, r1 = c * Q, (c + 1) * Q
            bb = bg[r0:r1].astype(BF16)
            cbf = cg[r0:r1].astype(BF16)
            cb = lax.dot_general(cbf, bb, (((1,), (1,)), ((), ())), preferred_element_type=F32)
            yoff = _dot(cbf, st.astype(BF16))
            dec = eac_e[r1 - 1:r1, 512 * g:512 * (g + 1)]
            bgt = bg[r0:r1].T.astype(BF16)
            st = st * dec + _dot(bgt, xdtw[r0:r1, 512 * g:512 * (g + 1)])
            for q in range(2):
                ms_list = []
                for r in range(4):
                    hh = 8 * g + 4 * q + r
                    seg = ac[r0:r1, hh:hh + 1] - ac_rows[hh:hh + 1, r0:r1]
                    lh = jnp.exp(jnp.where(causal, seg, -jnp.inf))
                    ms_list.append((cb * lh).astype(BF16))
                lhs = jnp.concatenate(ms_list, axis=1)
                lo = 512 * g + 256 * q
                x4 = xdt[r0:r1, lo:lo + 256]
                rhs = jnp.concatenate(
                    [jnp.where(cblk == r, x4, jnp.zeros_like(x4)) for r in range(4)], axis=0)
                yd = _dot(lhs, rhs)
                y_rows[q].append(yd + eac_e[r0:r1, lo:lo + 256] * yoff[:, 256 * q:256 * (q + 1)])
        st_ref[g] = st
        y_cols += [jnp.concatenate(y_rows[0], axis=0), jnp.concatenate(y_rows[1], axis=0)]
    y = jnp.concatenate(y_cols, axis=1) + dskip_ref[...] * xs

    v = y * _silu(z)
    outs = []
    for g in range(N_BC_GROUPS):
        vg = v[:, 512 * g:512 * (g + 1)]
        msg = jnp.mean(vg * vg, axis=-1, keepdims=True)
        outs.append(vg * lax.rsqrt(msg + EPS))
    ssd_out = (jnp.concatenate(outs, axis=1) * ssdn_ref[...]).astype(BF16)

    gs = _dot((sc * sc).astype(BF16), gsum_ref[...])
    rstd = jnp.where(hmask, lax.rsqrt(gs * (1.0 / HEAD_DIM) + EPS), 0.0)
    rstd_e = _dot(_pack3(rstd), e3_ref[...])
    sc_out = (sc * rstd_e * scn_ref[...]).astype(BF16)

    mix = _dot(ssd_out, wout_ref[0:1024, :]) + _dot(sc_out, wout_ref[1024:2048, :])
    h = x + mix
    h_ref[0] = h

    ms2 = jnp.mean(h * h, axis=-1, keepdims=True)
    hn2 = h * lax.rsqrt(ms2 + EPS) * gffn_ref[...]
    hi = hn2.astype(BF16)
    bits = pltpu.bitcast(hi.astype(F32), jnp.uint32)
    for s in range(ROW_TILES):
        pbuf[pl.ds(s * T, T), :] = bits[:, LANES * s:LANES * (s + 1)]
    for s in range(ROW_TILES):
        even = pbuf[pl.ds(s * T, PAIR_ROWS, stride=2), :]
        odd = pbuf[pl.ds(s * T + 1, PAIR_ROWS, stride=2), :]
        hn2_ref[pl.ds(s, PAIR_ROWS, stride=ROW_TILES), :] = jnp.bitwise_or(
            jnp.right_shift(even, jnp.uint32(16)), odd)
    lo_ = (hn2 - hi.astype(F32)).astype(BF16)
    RT = ROUTER_ROWS
    nt_dims = (((1,), (1,)), ((), ()))
    both = lax.dot_general(wr_ref[...], hi, nt_dims, preferred_element_type=F32)
    low = lax.dot_general(wr_ref[0:RT, :], lo_, nt_dims, preferred_element_type=F32)
    logits = both[0:RT] + (low + both[RT:2 * RT])

    ri = lax.broadcasted_iota(I32, (RT, T), 0)
    ri_f = ri.astype(F32)
    neg = -jnp.inf
    big = 1e9
    gl = jnp.where(ri < N_GROUPS, logits, neg)
    gmax = jnp.max(gl, axis=0, keepdims=True)
    gidx = jnp.min(jnp.where(gl == gmax, ri_f, big), axis=0, keepdims=True)
    gsum = jnp.sum(jnp.where(ri < N_GROUPS, jnp.exp(logits - gmax), 0.0), axis=0, keepdims=True)
    gw = 1.0 / gsum
    egrp = lax.shift_right_logical(ri - N_GROUPS, 3).astype(F32)
    in_grp = (ri >= N_GROUPS) & (ri < N_GROUPS + N_EXPERTS) & (egrp == gidx)
    el = jnp.where(in_grp, logits, neg)
    v1 = jnp.max(el, axis=0, keepdims=True)
    i1 = jnp.min(jnp.where(el == v1, ri_f, big), axis=0, keepdims=True)
    el2 = jnp.where(ri_f == i1, neg, el)
    v2 = jnp.max(el2, axis=0, keepdims=True)
    i2 = jnp.min(jnp.where(el2 == v2, ri_f, big), axis=0, keepdims=True)
    p = jnp.exp(v2 - v1)
    s1 = 1.0 / (1.0 + p)
    gate1 = gw * s1
    gate2 = gw * (p * s1)
    r8 = lax.broadcasted_iota(I32, (SUBLANES, T), 0)
    e_ref[...] = jnp.where(r8 == 0, i1 - N_GROUPS, jnp.where(r8 == 1, i2 - N_GROUPS, 0.0)).astype(I32)
    rl = lax.broadcasted_iota(I32, (LANES, T), 0)
    g_ref[...] = jnp.where(rl == 0, gate1, jnp.where(rl == 1, gate2, 0.0)).T


def _mixer_call(x, gmix, wa, wb, wdt, convw, convb, dtb, alog, dskip, ssdn, scw, scn, wout, gffn,
                wr, tri, e3, gsum):
    B, L, D = x.shape
    T = T_MIX
    nt = L // T
    n_tok = B * L

    def const(shape):
        return pl.BlockSpec(shape, lambda b, t: (0,) * len(shape), pipeline_mode=pl.Buffered(1))

    in_specs = [
        pl.BlockSpec((1, T, D), lambda b, t: (b, t, 0)),
        const((1, D)),
        const(wa.shape), const(wb.shape), const(wdt.shape), const(convw.shape), const(convb.shape),
        const(dtb.shape), const(alog.shape), const(dskip.shape), const(ssdn.shape),
        const(scw.shape), const(scn.shape), const(wout.shape), const(gffn.shape),
        const(wr.shape), const(tri.shape), const(e3.shape), const(gsum.shape),
    ]
    out_shape = [
        jax.ShapeDtypeStruct((B, L, D), F32),
        jax.ShapeDtypeStruct((n_tok // 2 * ROW_TILES, LANES), jnp.uint32),
        jax.ShapeDtypeStruct((8, n_tok), I32),
        jax.ShapeDtypeStruct((n_tok, LANES), F32),
    ]
    out_specs = [
        pl.BlockSpec((1, T, D), lambda b, t: (b, t, 0)),
        pl.BlockSpec((PAIR_ROWS * ROW_TILES, LANES), lambda b, t: (b * nt + t, 0)),
        pl.BlockSpec((8, T), lambda b, t: (0, b * nt + t)),
        pl.BlockSpec((T, LANES), lambda b, t: (b * nt + t, 0)),
    ]
    return pl.pallas_call(
        _mixer_kernel,
        grid=(B, nt),
        in_specs=in_specs,
        out_specs=out_specs,
        out_shape=out_shape,
        scratch_shapes=[
            pltpu.VMEM((T + 8, XBC), F32),
            pltpu.VMEM((T + 8, SC_WIDTH), F32),
            pltpu.VMEM((N_BC_GROUPS, STATE, 512), F32),
            pltpu.VMEM((ROW_TILES * T, LANES), jnp.uint32),
        ],
        compiler_params=pltpu.CompilerParams(
            dimension_semantics=("arbitrary", "arbitrary"),
            vmem_limit_bytes=VMEM_LIMIT),
        name="mixer",
    )(x, gmix, wa, wb, wdt, convw, convb, dtb, alog, dskip, ssdn, scw, scn, wout, gffn, wr,
      tri, e3, gsum)


def _moe_kernel(bstart_ref, nblk_ref, pk_ref, hn2p_ref, wg_ref, wu_ref, wd_ref,
                y2_ref, hn2v, xbuf, ybuf, wgu, wdb, idx_ref, lsem, ssem, isem, *, n_tok):
    R = R_BLK
    slab = R * ROW_TILES
    e = pl.program_id(0)
    g0 = bstart_ref[e]
    nb_e = nblk_ref[e]

    def idx_word(sl, field, j):
        return idx_ref[sl, 2 * field + j // LANES, j % LANES]

    def idx_copy(g, sl):
        return pltpu.make_async_copy(pk_ref.at[g], idx_ref.at[sl], isem.at[sl])

    def gather_rows(sl, xs):
        for j in range(R):
            row = pl.multiple_of(idx_word(sl, 0, j), ROW_TILES)
            sh = idx_word(sl, 1, j).astype(jnp.uint32)
            w = hn2v[pl.ds(row, ROW_TILES), :]
            w = jnp.left_shift(jnp.right_shift(w, sh), jnp.uint32(16))
            xbuf[pl.ds(xs * slab + j * ROW_TILES, ROW_TILES), :] = pltpu.bitcast(w, F32)

    yslab = R * Y_TILES
    FH = D_FF // 2

    def y_copy(sl, j, dst_row):
        return pltpu.make_async_copy(
            ybuf.at[pl.ds(sl * yslab + j * Y_TILES, Y_TILES), :],
            y2_ref.at[pl.ds(pl.multiple_of(dst_row, Y_TILES), Y_TILES), :],
            ssem.at[sl])

    def scatter_start(sl, lo, hi):
        for j in range(lo, hi):
            y_copy(sl, j, idx_word(sl, 2, j)).start(priority=j % 2)

    def slab_wait(sl):
        view = ybuf.at[pl.ds(sl * yslab, yslab), :]
        pltpu.make_async_copy(view, view, ssem.at[sl]).wait()

    n_idx_blocks = pk_ref.shape[0]

    @pl.when(e == 0)
    def _():
        load = pltpu.make_async_copy(hn2p_ref, hn2v, lsem.at[0])
        load.start()
        first = idx_copy(0, 0)
        first.start()
        idx_copy(1, 1).start()
        prev = idx_copy(n_idx_blocks - 1, N_SLOTS - 1)
        prev.start()
        ybuf[...] = jnp.zeros(ybuf.shape, ybuf.dtype)
        for sl in range(N_SLOTS - 1):
            base = 2 * n_tok + (sl + 1 if sl else 0) * R
            for j in range(R):
                y_copy(sl, j, (base + j) * Y_TILES).start(priority=j % 2)
        prev.wait()
        first.wait()
        load.wait()
        gather_rows(0, 0)

    @pl.when(nb_e > 0)
    def _():
        for j in range(2):
            wgu[:, 2 * j * FH:(2 * j + 1) * FH] = wg_ref[0, :, j * FH:(j + 1) * FH].astype(BF16)
            wgu[:, (2 * j + 1) * FH:(2 * j + 2) * FH] = wu_ref[0, :, j * FH:(j + 1) * FH].astype(BF16)
        wdb[...] = wd_ref[0].astype(BF16)

    c1, c2 = R // 3, 2 * (R // 3)

    def run_block(g, sl):
        prv = (sl + N_SLOTS - 1) % N_SLOTS
        nx1 = (sl + 1) % N_SLOTS
        nx2 = (sl + 2) % N_SLOTS
        idx_copy(g + 1, nx1).wait()
        idx_copy(g + 2, nx2).start()
        slab_wait(sl)
        x = jnp.concatenate(
            [xbuf[pl.ds((sl % 2) * slab + s, R, stride=ROW_TILES), :] for s in range(ROW_TILES)],
            axis=1).astype(BF16)
        scatter_start(prv, 0, c1)
        gu = _dot(x, wgu[...])
        gather_rows(nx1, (sl + 1) % 2)
        scatter_start(prv, c1, c2)
        act = jnp.concatenate(
            [_silu(gu[:, 2 * j * FH:(2 * j + 1) * FH]) * gu[:, (2 * j + 1) * FH:(2 * j + 2) * FH]
             for j in range(2)], axis=1).astype(BF16)
        scatter_start(prv, c2, R)
        yy = _dot(act, wdb[...])
        for s in range(Y_TILES):
            lo_w = pltpu.bitcast(yy[:, 2 * LANES * s:2 * LANES * s + LANES].astype(BF16).astype(F32), jnp.uint32)
            hi_w = pltpu.bitcast(yy[:, 2 * LANES * s + LANES:2 * LANES * (s + 1)].astype(BF16).astype(F32),
                                 jnp.uint32)
            ybuf[pl.ds(sl * yslab + s, R, stride=Y_TILES), :] = jnp.bitwise_or(
                jnp.right_shift(lo_w, jnp.uint32(16)), hi_w)

    def block(i, carry):
        g = g0 + i
        sl = lax.rem(g, N_SLOTS)
        for k in range(N_SLOTS):
            @pl.when(sl == k)
            def _(k=k):
                run_block(g, k)
        return carry

    lax.fori_loop(0, nb_e, block, 0)

    @pl.when(e == pl.num_programs(0) - 1)
    def _():
        g_end = g0 + nb_e
        sl_end = lax.rem(g_end, N_SLOTS)
        for k in range(N_SLOTS):
            @pl.when(sl_end == k)
            def _(k=k):
                scatter_start((k + N_SLOTS - 1) % N_SLOTS, 0, R)
                idx_copy(g_end + 1, (k + 1) % N_SLOTS).wait()
        for k in range(N_SLOTS):
            slab_wait(k)


def _moe_call(bstart, nblk, pk, hn2p, wg, wu, wd, n_tok):
    R = R_BLK
    slab = R * ROW_TILES
    grid_spec = pltpu.PrefetchScalarGridSpec(
        num_scalar_prefetch=2,
        grid=(N_EXPERTS,),
        in_specs=[
            pl.BlockSpec(memory_space=pl.ANY),
            pl.BlockSpec(memory_space=pl.ANY),
            pl.BlockSpec((1, D_MODEL, D_FF), lambda e, bs, nb: (e, 0, 0)),
            pl.BlockSpec((1, D_MODEL, D_FF), lambda e, bs, nb: (e, 0, 0)),
            pl.BlockSpec((1, D_FF, D_MODEL), lambda e, bs, nb: (e, 0, 0)),
        ],
        out_specs=pl.BlockSpec(memory_space=pl.ANY),
        scratch_shapes=[
            pltpu.VMEM(hn2p.shape, jnp.uint32),
            pltpu.VMEM((2 * slab, LANES), F32),
            pltpu.VMEM((N_SLOTS * R * Y_TILES, LANES), jnp.uint32),
            pltpu.VMEM((D_MODEL, 2 * D_FF), BF16),
            pltpu.VMEM((D_FF, D_MODEL), BF16),
            pltpu.SMEM((N_SLOTS, SUBLANES, LANES), I32),
            pltpu.SemaphoreType.DMA((1,)),
            pltpu.SemaphoreType.DMA((N_SLOTS,)),
            pltpu.SemaphoreType.DMA((N_SLOTS,)),
        ],
    )
    return pl.pallas_call(
        functools.partial(_moe_kernel, n_tok=n_tok),
        grid_spec=grid_spec,
        out_shape=jax.ShapeDtypeStruct(((2 * n_tok + N_SLOTS * R) * Y_TILES, LANES), jnp.uint32),
        compiler_params=pltpu.CompilerParams(
            dimension_semantics=("arbitrary",),
            vmem_limit_bytes=VMEM_LIMIT),
        name="moe",
    )(bstart, nblk, pk, hn2p, wg, wu, wd)


def _unpack_rows(y_ref, rows):
    parts = []
    for s in range(Y_TILES):
        w = y_ref[pl.ds(s, rows, stride=Y_TILES), :]
        parts.append(pltpu.bitcast(jnp.left_shift(w, jnp.uint32(16)), F32))
        parts.append(pltpu.bitcast(jnp.bitwise_and(w, jnp.uint32(0xFFFF0000)), F32))
    return jnp.concatenate(parts, axis=1)


def _combine_kernel(h_ref, y0_ref, y1_ref, g_ref, fn_ref, o_ref):
    T = T_CMB
    h = h_ref[...]
    y0 = _unpack_rows(y0_ref, T)
    y1 = _unpack_rows(y1_ref, T)
    g = g_ref[...]
    v = h + (y0 * g[:, 0:1] + y1 * g[:, 1:2])
    ms = jnp.mean(v * v, axis=-1, keepdims=True)
    o_ref[...] = v * lax.rsqrt(ms + EPS) * fn_ref[...]


def _combine_call(h2d, y2, gates, fnorm):
    n_tok, D = h2d.shape
    T = T_CMB
    nt = n_tok // T
    return pl.pallas_call(
        _combine_kernel,
        grid=(nt,),
        in_specs=[
            pl.BlockSpec((T, D), lambda i: (i, 0)),
            pl.BlockSpec((T * Y_TILES, LANES), lambda i: (i, 0)),
            pl.BlockSpec((T * Y_TILES, LANES), lambda i: (nt + i, 0)),
            pl.BlockSpec((T, LANES), lambda i: (i, 0)),
            pl.BlockSpec((1, D), lambda i: (0, 0)),
        ],
        out_specs=pl.BlockSpec((T, D), lambda i: (i, 0)),
        out_shape=jax.ShapeDtypeStruct((n_tok, D), F32),
        compiler_params=pltpu.CompilerParams(dimension_semantics=("arbitrary",)),
        name="combine",
    )(h2d, y2, y2, gates, fnorm)


def _plan_kernel(e_ref, ux_ref, ones_ref, lx_ref, dest_ref, bstart_ref, nblk_ref):
    rows = e_ref.shape[0]
    ev = e_ref[...]
    lane8 = lax.broadcasted_iota(I32, (SUBLANES, LANES), 1)
    dest = jnp.zeros((rows, LANES), F32)
    bstart = jnp.zeros((SUBLANES, LANES), I32)
    nblk = jnp.zeros((SUBLANES, LANES), I32)
    pstart = jnp.zeros((1, LANES), F32)
    for e in range(N_EXPERTS):
        ohb = ev == e
        oh = jnp.where(ohb, 1.0, 0.0).astype(BF16)
        within = _dot(oh, ux_ref[...])
        rtot = _dot(oh, ones_ref[...])
        rpre = _dot(lx_ref[...], rtot.astype(BF16))
        cnt = rpre[rows - 1:rows, :] + rtot[rows - 1:rows, :]
        dest = dest + jnp.where(ohb, within + rpre + pstart, 0.0)
        cnt_i = cnt.astype(I32)
        nb_e = lax.shift_right_logical(cnt_i + (R_BLK - 1), R_SHIFT)
        bstart = jnp.where(lane8 == e, lax.shift_right_logical(pstart.astype(I32), R_SHIFT), bstart)
        nblk = jnp.where(lane8 == e, nb_e, nblk)
        pstart = pstart + lax.shift_left(nb_e, R_SHIFT).astype(F32)
    dest_ref[...] = dest.astype(I32)
    bstart_ref[...] = bstart
    nblk_ref[...] = nblk


def _invert_kernel(dest_ref, init_ref, inv_ref):
    n_asg = dest_ref.shape[0]
    pltpu.sync_copy(init_ref, inv_ref)

    def put(a, c):
        inv_ref[dest_ref[a]] = a
        return c

    lax.fori_loop(0, n_asg, put, 0, unroll=16)


def _index_tiles_kernel(inv_ref, pk_ref, *, n_tok):
    nb = pk_ref.shape[0] // SUBLANES
    per_blk = R_BLK // LANES
    assert per_blk == 2
    pk_ref[...] = jnp.zeros(pk_ref.shape, I32)
    for half in range(per_blk):
        a = inv_ref[pl.ds(half, nb, stride=per_blk), :]
        t = jnp.bitwise_and(a, n_tok - 1)
        pk_ref[pl.ds(half, nb, stride=SUBLANES), :] = lax.shift_left(lax.shift_right_logical(t, 1), 3)
        pk_ref[pl.ds(2 + half, nb, stride=SUBLANES), :] = lax.shift_left(jnp.bitwise_and(a, 1), 4)
        pk_ref[pl.ds(4 + half, nb, stride=SUBLANES), :] = a * Y_TILES


def _plan(e_rows, n_tok, nb):
    R = R_BLK
    n_asg = 2 * n_tok
    rows = n_asg // LANES
    emat = e_rows[0:2].reshape(rows, LANES)
    li = jnp.arange(LANES)
    ux = (li[:, None] < li[None, :]).astype(BF16)
    ones = jnp.ones((LANES, LANES), BF16)
    ri = jnp.arange(rows)
    lx = (ri[:, None] > ri[None, :]).astype(BF16)
    dest, bstart, nblk = pl.pallas_call(
        _plan_kernel,
        out_shape=[jax.ShapeDtypeStruct((rows, LANES), I32),
                   jax.ShapeDtypeStruct((SUBLANES, LANES), I32),
                   jax.ShapeDtypeStruct((SUBLANES, LANES), I32)],
        name="plan",
    )(emat, ux, ones, lx)
    pos = jnp.arange(nb * R, dtype=I32)
    init = n_asg + (pos & (R - 1)) + jnp.where(pos >= (nb - 1) * R, R, 0)
    inv = pl.pallas_call(
        _invert_kernel,
        in_specs=[pl.BlockSpec(memory_space=pltpu.SMEM), pl.BlockSpec(memory_space=pl.ANY)],
        out_specs=pl.BlockSpec(memory_space=pltpu.SMEM),
        out_shape=jax.ShapeDtypeStruct((nb * R,), I32),
        name="invert",
    )(dest.reshape(n_asg), init)
    pk = pl.pallas_call(
        functools.partial(_index_tiles_kernel, n_tok=n_tok),
        out_shape=jax.ShapeDtypeStruct((nb * SUBLANES, LANES), I32),
        name="index_tiles",
    )(inv.reshape(nb * R // LANES, LANES))
    return bstart[0, 0:N_EXPERTS], nblk[0, 0:N_EXPERTS], pk.reshape(nb, SUBLANES, LANES)


def kernel(x, norm_mix, w_in, ssd_conv_w, ssd_conv_b, dt_bias, a_log, d_skip, ssd_norm, sc_conv_w,
           sc_norm, w_out, norm_ffn, w_router_group, w_router_expert, w_gate, w_up, w_down, final_norm):
    B, L, D = x.shape
    n_tok = B * L
    depth = norm_mix.shape[0]
    assert depth == 1 and D == D_MODEL and (n_tok & (n_tok - 1)) == 0
    nb = -(-((2 * n_tok) // R_BLK + N_EXPERTS + 1) // SUBLANES) * SUBLANES

    o1 = SSD_WIDTH
    o2 = o1 + XBC
    o3 = o2 + N_HEADS
    wi = w_in.reshape(D, -1)
    wa = wi[:, 0:o2].astype(BF16)
    wb = wi[:, o3:].astype(BF16)
    wdt = jnp.pad(wi[:, o2:o3], ((0, 0), (0, LANES - N_HEADS))).astype(BF16)
    pad_h = (0, LANES - N_HEADS)
    dtb = jnp.pad(dt_bias[0], pad_h).reshape(1, LANES)
    alog = jnp.pad(a_log[0], pad_h).reshape(1, LANES)
    dskip = jnp.repeat(d_skip[0], HEAD_DIM).reshape(1, SSD_WIDTH)

    wre = jnp.transpose(w_router_expert[0], (1, 0, 2)).reshape(D, N_EXPERTS)
    wrt = jnp.pad(jnp.concatenate([w_router_group[0], wre], axis=1).T,
                  ((0, ROUTER_ROWS - N_GROUPS - N_EXPERTS), (0, 0)))
    wrt_hi = wrt.astype(BF16)
    wr = jnp.concatenate([wrt_hi, (wrt - wrt_hi.astype(F32)).astype(BF16)], axis=0)

    ri = jnp.arange(T_MIX)
    tri = (ri[:, None] >= ri[None, :]).astype(BF16)
    er = jnp.arange(LANES)
    ec = jnp.arange(SSD_WIDTH)
    e3 = ((er[:, None] < 48) & ((er[:, None] % 16) == (ec[None, :] // HEAD_DIM))).astype(BF16)
    gsum = ((ec[:, None] // HEAD_DIM) == er[None, :]).astype(BF16)

    h, hn2, e_rows, gates = _mixer_call(
        x, norm_mix[0].reshape(1, D), wa, wb, wdt, ssd_conv_w[0], ssd_conv_b[0].reshape(1, XBC), dtb, alog,
        dskip, ssd_norm[0].reshape(1, SSD_WIDTH), sc_conv_w[0], sc_norm[0].reshape(1, SC_WIDTH),
        w_out.reshape(-1, D).astype(BF16), norm_ffn[0].reshape(1, D), wr, tri, e3, gsum)

    bstart, nblk, pk = _plan(e_rows, n_tok, nb)
    y2 = _moe_call(bstart, nblk, pk, hn2, w_gate.reshape(N_EXPERTS, D, D_FF),
                   w_up.reshape(N_EXPERTS, D, D_FF), w_down.reshape(N_EXPERTS, D_FF, D), n_tok)
    out = _combine_call(h.reshape(n_tok, D), y2, gates, final_norm.reshape(1, D))
    return out.reshape(B, L, D)
```

```python
import functools

import jax
import jax.numpy as jnp
from jax import lax
from jax.experimental import pallas as pl
from jax.experimental.pallas import tpu as pltpu

F32 = jnp.float32
BF16 = jnp.bfloat16
I32 = jnp.int32

EPS = 1e-6
D_MODEL = 1024
N_HEADS = 16
HEAD_DIM = 64
N_BC_GROUPS = 2
STATE = 128
SSD_WIDTH = 1024
XBC = SSD_WIDTH + 2 * N_BC_GROUPS * STATE
SC_WIDTH = 1024
SC_GROUPS = 16
N_GROUPS = 4
EPG = 8
N_EXPERTS = 32
D_FF = 512

LANES = 128
SUBLANES = 8
ROW_TILES = D_MODEL // LANES
Y_TILES = ROW_TILES // 2

T_MIX = 512
Q_SSD = 128
COL_CHUNK = 512
PAIR_ROWS = T_MIX // 2
R_BLK = 256
R_SHIFT = 8
ROUTER_ROWS = 48
N_SLOTS = 4
T_CMB = 1024

VMEM_LIMIT = 56 * 1024 * 1024


def _dot(a, b):
    return jnp.dot(a, b, preferred_element_type=F32)


def _split3(v):
    p1 = v.astype(BF16).astype(F32)
    r1 = v - p1
    p2 = r1.astype(BF16).astype(F32)
    p3 = (r1 - p2).astype(BF16).astype(F32)
    return p1, p2, p3


def _pack3(v):
    p1, p2, p3 = _split3(v)
    return (p1 + pltpu.roll(p2, 16, 1) + pltpu.roll(p3, 32, 1)).astype(BF16)


def _silu(v):
    return v * jax.nn.sigmoid(v)


def _mixer_kernel(x_ref, gmix_ref, wa_ref, wb_ref, wdt_ref, convw_ref, convb_ref, dtb_ref, alog_ref,
                  dskip_ref, ssdn_ref, scw_ref, scn_ref, wout_ref, gffn_ref, wr_ref,
                  tri_ref, e3_ref, gsum_ref,
                  h_ref, hn2_ref, e_ref, g_ref,
                  cbuf, sbuf, st_ref, pbuf):
    T = T_MIX
    t = pl.program_id(1)

    @pl.when(t == 0)
    def _():
        cbuf[0:8, :] = jnp.zeros((8, XBC), F32)
        sbuf[0:8, :] = jnp.zeros((8, SC_WIDTH), F32)
        st_ref[...] = jnp.zeros(st_ref.shape, F32)

    @pl.when(t > 0)
    def _():
        cbuf[0:8, :] = cbuf[T:T + 8, :]
        sbuf[0:8, :] = sbuf[T:T + 8, :]

    x = x_ref[0]
    ms = jnp.mean(x * x, axis=-1, keepdims=True)
    hn = (x * lax.rsqrt(ms + EPS) * gmix_ref[...]).astype(BF16)

    dt_raw = _dot(hn, wdt_ref[...])

    cw = convw_ref[...]
    xact_parts = []
    for c0 in range(0, XBC, COL_CHUNK):
        c1 = c0 + COL_CHUNK
        xbc = _dot(hn, wa_ref[:, 1024 + c0:1024 + c1])
        cbuf[8:8 + T, c0:c1] = xbc
        acc = convb_ref[:, c0:c1] + cw[3:4, c0:c1] * xbc
        for k in range(3):
            acc = acc + cw[k:k + 1, c0:c1] * cbuf[5 + k:5 + k + T, c0:c1]
        xact_parts.append(_silu(acc))
    xact = jnp.concatenate(xact_parts, axis=1)
    xs = xact[:, 0:SSD_WIDTH]

    lane = lax.broadcasted_iota(I32, (1, LANES), 1)
    hmask = lane < N_HEADS
    a = jnp.where(hmask, -jnp.exp(alog_ref[...]), 0.0)
    dtv = dt_raw + dtb_ref[...]
    dt = jnp.where(hmask, jnp.maximum(dtv, 0.0) + jnp.log1p(jnp.exp(-jnp.abs(dtv))), 0.0)
    adt = dt * a

    sw = scw_ref[...]
    sc_parts = []
    for c0 in range(0, SC_WIDTH, COL_CHUNK):
        c1 = c0 + COL_CHUNK
        u = _dot(hn, wb_ref[:, 1024 + c0:1024 + c1]) * _dot(hn, wb_ref[:, 2048 + c0:2048 + c1])
        sbuf[8:8 + T, c0:c1] = u
        conv = (sw[2:3, c0:c1] * u + sw[1:2, c0:c1] * sbuf[7:7 + T, c0:c1]
                + sw[0:1, c0:c1] * sbuf[6:6 + T, c0:c1])
        sc_parts.append(_dot(hn, wb_ref[:, c0:c1]) * conv)
    sc = jnp.concatenate(sc_parts, axis=1)
    z_lo = _dot(hn, wa_ref[:, 0:COL_CHUNK])
    gs = _dot((sc * sc).astype(BF16), gsum_ref[...])
    z_hi = _dot(hn, wa_ref[:, COL_CHUNK:2 * COL_CHUNK])
    rstd = jnp.where(lax.broadcasted_iota(I32, (1, LANES), 1) < SC_GROUPS,
                     lax.rsqrt(gs * (1.0 / HEAD_DIM) + EPS), 0.0)
    rstd_e = _dot(_pack3(rstd), e3_ref[...])
    sc_out = (sc * rstd_e * scn_ref[...]).astype(BF16)
    z = jnp.concatenate([z_lo, z_hi], axis=1)

    c3 = _dot(tri_ref[...], _pack3(adt))
    ac = jnp.where(hmask, c3 + pltpu.roll(c3, LANES - 16, 1) + pltpu.roll(c3, LANES - 32, 1), 0.0)
    Q = Q_SSD
    n_sub = T // Q
    ends = [ac[(c + 1) * Q - 1:(c + 1) * Q, :] for c in range(n_sub)]
    base = jnp.concatenate([jnp.broadcast_to(ends[c - 1] if c else jnp.zeros_like(ends[0]), (Q, LANES))
                            for c in range(n_sub)], axis=0)
    endv = jnp.concatenate([jnp.broadcast_to(ends[c], (Q, LANES)) for c in range(n_sub)], axis=0)
    eac = jnp.where(hmask, jnp.exp(ac - base), 0.0)
    wdt = dt * jnp.exp(endv - ac)

    stacked = jnp.concatenate([_pack3(dt), _pack3(wdt), _pack3(eac)], axis=0)
    ex = _dot(stacked, e3_ref[...])
    dt_e = ex[0:T]
    wdt_e = ex[T:2 * T]
    eac_e = ex[2 * T:3 * T]
    xdt = (xs * dt_e).astype(BF16)
    xdtw = (xs * wdt_e).astype(BF16)

    mix_sc = []
    mix_todo = list(range(0, D_MODEL, COL_CHUNK))

    def emit_mix():
        if mix_todo:
            c0 = mix_todo.pop(0)
            mix_sc.append(_dot(sc_out, wout_ref[SSD_WIDTH:SSD_WIDTH + SC_WIDTH, c0:c0 + COL_CHUNK]))

    ac_rows = ac.T
    rr = lax.broadcasted_iota(I32, (Q, Q), 0)
    cc = lax.broadcasted_iota(I32, (Q, Q), 1)
    causal = rr >= cc
    cblk = lax.shift_right_logical(lax.broadcasted_iota(I32, (Q, 4 * HEAD_DIM), 1), 6)

    y_cols = []
    for g in range(N_BC_GROUPS):
        bg = xact[:, SSD_WIDTH + STATE * g:SSD_WIDTH + STATE * (g + 1)]
        cg = xact[:, SSD_WIDTH + 2 * STATE + STATE * g:SSD_WIDTH + 2 * STATE + STATE * (g + 1)]
        st = st_ref[g]
        y_rows = [[], []]
        for c in range(n_sub):
            r0, r1 = c * Q, (c + 1) * Q
            bb = bg[r0:r1].astype(BF16)
            cbf = cg[r0:r1].astype(BF16)
            cb = lax.dot_general(cbf, bb, (((1,), (1,)), ((), ())), preferred_element_type=F32)
            yoff = _dot(cbf, st.astype(BF16))
            dec = eac_e[r1 - 1:r1, 512 * g:512 * (g + 1)]
            bgt = bg[r0:r1].T.astype(BF16)
            st = st * dec + _dot(bgt, xdtw[r0:r1, 512 * g:512 * (g + 1)])
            for q in range(2):
                ms_list = []
                for r in range(4):
                    hh = 8 * g + 4 * q + r
                    seg = ac[r0:r1, hh:hh + 1] - ac_rows[hh:hh + 1, r0:r1]
                    lh = jnp.exp(jnp.where(causal, seg, -jnp.inf))
                    ms_list.append((cb * lh).astype(BF16))
                lhs = jnp.concatenate(ms_list, axis=1)
                if c == 1 and q == 0:
                    emit_mix()
                lo = 512 * g + 256 * q
                x4 = xdt[r0:r1, lo:lo + 256]
                rhs = jnp.concatenate(
                    [jnp.where(cblk == r, x4, jnp.zeros_like(x4)) for r in range(4)], axis=0)
                yd = _dot(lhs, rhs)
                y_rows[q].append(yd + eac_e[r0:r1, lo:lo + 256] * yoff[:, 256 * q:256 * (q + 1)])
        st_ref[g] = st
        y_cols += [jnp.concatenate(y_rows[0], axis=0), jnp.concatenate(y_rows[1], axis=0)]
    y = jnp.concatenate(y_cols, axis=1) + dskip_ref[...] * xs

    while mix_todo:
        emit_mix()
    v = y * _silu(z)
    outs = []
    for g in range(N_BC_GROUPS):
        vg = v[:, 512 * g:512 * (g + 1)]
        msg = jnp.mean(vg * vg, axis=-1, keepdims=True)
        outs.append(vg * lax.rsqrt(msg + EPS))
    ssd_out = (jnp.concatenate(outs, axis=1) * ssdn_ref[...]).astype(BF16)

    mix = _dot(ssd_out, wout_ref[0:SSD_WIDTH, :]) + jnp.concatenate(mix_sc, axis=1)
    h = x + mix
    h_ref[0] = h

    ms2 = jnp.mean(h * h, axis=-1, keepdims=True)
    hn2 = h * lax.rsqrt(ms2 + EPS) * gffn_ref[...]
    hi = hn2.astype(BF16)
    bits = pltpu.bitcast(hi.astype(F32), jnp.uint32)
    for s in range(ROW_TILES):
        pbuf[pl.ds(s * T, T), :] = bits[:, LANES * s:LANES * (s + 1)]
    for s in range(ROW_TILES):
        even = pbuf[pl.ds(s * T, PAIR_ROWS, stride=2), :]
        odd = pbuf[pl.ds(s * T + 1, PAIR_ROWS, stride=2), :]
        hn2_ref[pl.ds(s, PAIR_ROWS, stride=ROW_TILES), :] = jnp.bitwise_or(
            jnp.right_shift(even, jnp.uint32(16)), odd)
    lo_ = (hn2 - hi.astype(F32)).astype(BF16)
    RT = ROUTER_ROWS
    nt_dims = (((1,), (1,)), ((), ()))
    both = lax.dot_general(wr_ref[...], hi, nt_dims, preferred_element_type=F32)
    low = lax.dot_general(wr_ref[0:RT, :], lo_, nt_dims, preferred_element_type=F32)
    logits = both[0:RT] + (low + both[RT:2 * RT])

    ri = lax.broadcasted_iota(I32, (RT, T), 0)
    ri_f = ri.astype(F32)
    neg = -jnp.inf
    big = 1e9
    gl = jnp.where(ri < N_GROUPS, logits, neg)
    gmax = jnp.max(gl, axis=0, keepdims=True)
    gidx = jnp.min(jnp.where(gl == gmax, ri_f, big), axis=0, keepdims=True)
    gsum = jnp.sum(jnp.where(ri < N_GROUPS, jnp.exp(logits - gmax), 0.0), axis=0, keepdims=True)
    gw = 1.0 / gsum
    egrp = lax.shift_right_logical(ri - N_GROUPS, 3).astype(F32)
    in_grp = (ri >= N_GROUPS) & (ri < N_GROUPS + N_EXPERTS) & (egrp == gidx)
    el = jnp.where(in_grp, logits, neg)
    v1 = jnp.max(el, axis=0, keepdims=True)
    i1 = jnp.min(jnp.where(el == v1, ri_f, big), axis=0, keepdims=True)
    el2 = jnp.where(ri_f == i1, neg, el)
    v2 = jnp.max(el2, axis=0, keepdims=True)
    i2 = jnp.min(jnp.where(el2 == v2, ri_f, big), axis=0, keepdims=True)
    p = jnp.exp(v2 - v1)
    s1 = 1.0 / (1.0 + p)
    gate1 = gw * s1
    gate2 = gw * (p * s1)
    r8 = lax.broadcasted_iota(I32, (SUBLANES, T), 0)
    e_ref[...] = jnp.where(r8 == 0, i1 - N_GROUPS, jnp.where(r8 == 1, i2 - N_GROUPS, 0.0)).astype(I32)
    rl = lax.broadcasted_iota(I32, (LANES, T), 0)
    g_ref[...] = jnp.where(rl == 0, gate1, jnp.where(rl == 1, gate2, 0.0)).T


def _mixer_call(x, gmix, wa, wb, wdt, convw, convb, dtb, alog, dskip, ssdn, scw, scn, wout, gffn,
                wr, tri, e3, gsum):
    B, L, D = x.shape
    T = T_MIX
    nt = L // T
    n_tok = B * L

    def const(shape):
        return pl.BlockSpec(shape, lambda b, t: (0,) * len(shape), pipeline_mode=pl.Buffered(1))

    in_specs = [
        pl.BlockSpec((1, T, D), lambda b, t: (b, t, 0)),
        const((1, D)),
        const(wa.shape), const(wb.shape), const(wdt.shape), const(convw.shape), const(convb.shape),
        const(dtb.shape), const(alog.shape), const(dskip.shape), const(ssdn.shape),
        const(scw.shape), const(scn.shape), const(wout.shape), const(gffn.shape),
        const(wr.shape), const(tri.shape), const(e3.shape), const(gsum.shape),
    ]
    out_shape = [
        jax.ShapeDtypeStruct((B, L, D), F32),
        jax.ShapeDtypeStruct((n_tok // 2 * ROW_TILES, LANES), jnp.uint32),
        jax.ShapeDtypeStruct((8, n_tok), I32),
        jax.ShapeDtypeStruct((n_tok, LANES), F32),
    ]
    out_specs = [
        pl.BlockSpec((1, T, D), lambda b, t: (b, t, 0)),
        pl.BlockSpec((PAIR_ROWS * ROW_TILES, LANES), lambda b, t: (b * nt + t, 0)),
        pl.BlockSpec((8, T), lambda b, t: (0, b * nt + t)),
        pl.BlockSpec((T, LANES), lambda b, t: (b * nt + t, 0)),
    ]
    return pl.pallas_call(
        _mixer_kernel,
        grid=(B, nt),
        in_specs=in_specs,
        out_specs=out_specs,
        out_shape=out_shape,
        scratch_shapes=[
            pltpu.VMEM((T + 8, XBC), F32),
            pltpu.VMEM((T + 8, SC_WIDTH), F32),
            pltpu.VMEM((N_BC_GROUPS, STATE, 512), F32),
            pltpu.VMEM((ROW_TILES * T, LANES), jnp.uint32),
        ],
        compiler_params=pltpu.CompilerParams(
            dimension_semantics=("arbitrary", "arbitrary"),
            vmem_limit_bytes=VMEM_LIMIT),
        name="mixer",
    )(x, gmix, wa, wb, wdt, convw, convb, dtb, alog, dskip, ssdn, scw, scn, wout, gffn, wr,
      tri, e3, gsum)


def _moe_kernel(bstart_ref, nblk_ref, pk_ref, hn2p_ref, wg_ref, wu_ref, wd_ref,
                y2_ref, hn2v, xbuf, ybuf, wgu, wdb, idx_ref, lsem, ssem, isem, *, n_tok):
    R = R_BLK
    slab = R * ROW_TILES
    e = pl.program_id(0)
    g0 = bstart_ref[e]
    nb_e = nblk_ref[e]

    def idx_word(sl, field, j):
        return idx_ref[sl, 2 * field + j // LANES, j % LANES]

    def idx_copy(g, sl):
        return pltpu.make_async_copy(pk_ref.at[g], idx_ref.at[sl], isem.at[sl])

    def gather_rows(sl, xs):
        for j in range(R):
            row = pl.multiple_of(idx_word(sl, 0, j), ROW_TILES)
            sh = idx_word(sl, 1, j).astype(jnp.uint32)
            w = hn2v[pl.ds(row, ROW_TILES), :]
            w = jnp.left_shift(jnp.right_shift(w, sh), jnp.uint32(16))
            xbuf[pl.ds(xs * slab + j * ROW_TILES, ROW_TILES), :] = pltpu.bitcast(w, F32)

    yslab = R * Y_TILES
    FH = D_FF // 2

    def y_copy(sl, j, dst_row):
        return pltpu.make_async_copy(
            ybuf.at[pl.ds(sl * yslab + j * Y_TILES, Y_TILES), :],
            y2_ref.at[pl.ds(pl.multiple_of(dst_row, Y_TILES), Y_TILES), :],
            ssem.at[sl])

    def scatter_start(sl, lo, hi):
        for j in range(lo, hi):
            y_copy(sl, j, idx_word(sl, 2, j)).start(priority=j % 2)

    def slab_wait(sl):
        view = ybuf.at[pl.ds(sl * yslab, yslab), :]
        pltpu.make_async_copy(view, view, ssem.at[sl]).wait()

    n_idx_blocks = pk_ref.shape[0]

    @pl.when(e == 0)
    def _():
        load = pltpu.make_async_copy(hn2p_ref, hn2v, lsem.at[0])
        load.start()
        first = idx_copy(0, 0)
        first.start()
        idx_copy(1, 1).start()
        prev = idx_copy(n_idx_blocks - 1, N_SLOTS - 1)
        prev.start()
        ybuf[...] = jnp.zeros(ybuf.shape, ybuf.dtype)
        for sl in range(N_SLOTS - 1):
            base = 2 * n_tok + (sl + 1 if sl else 0) * R
            for j in range(R):
                y_copy(sl, j, (base + j) * Y_TILES).start(priority=j % 2)
        prev.wait()
        first.wait()
        load.wait()
        gather_rows(0, 0)

    @pl.when(nb_e > 0)
    def _():
        for j in range(2):
            wgu[:, 2 * j * FH:(2 * j + 1) * FH] = wg_ref[0, :, j * FH:(j + 1) * FH].astype(BF16)
            wgu[:, (2 * j + 1) * FH:(2 * j + 2) * FH] = wu_ref[0, :, j * FH:(j + 1) * FH].astype(BF16)
        wdb[...] = wd_ref[0].astype(BF16)

    c1, c2 = R // 3, 2 * (R // 3)

    def run_block(g, sl):
        prv = (sl + N_SLOTS - 1) % N_SLOTS
        nx1 = (sl + 1) % N_SLOTS
        nx2 = (sl + 2) % N_SLOTS
        idx_copy(g + 1, nx1).wait()
        idx_copy(g + 2, nx2).start()
        slab_wait(sl)
        x = jnp.concatenate(
            [xbuf[pl.ds((sl % 2) * slab + s, R, stride=ROW_TILES), :] for s in range(ROW_TILES)],
            axis=1).astype(BF16)
        scatter_start(prv, 0, c1)
        gu = _dot(x, wgu[...])
        gather_rows(nx1, (sl + 1) % 2)
        scatter_start(prv, c1, c2)
        act = jnp.concatenate(
            [_silu(gu[:, 2 * j * FH:(2 * j + 1) * FH]) * gu[:, (2 * j + 1) * FH:(2 * j + 2) * FH]
             for j in range(2)], axis=1).astype(BF16)
        scatter_start(prv, c2, R)
        yy = _dot(act, wdb[...])
        for s in range(Y_TILES):
            lo_w = pltpu.bitcast(yy[:, 2 * LANES * s:2 * LANES * s + LANES].astype(BF16).astype(F32), jnp.uint32)
            hi_w = pltpu.bitcast(yy[:, 2 * LANES * s + LANES:2 * LANES * (s + 1)].astype(BF16).astype(F32),
                                 jnp.uint32)
            ybuf[pl.ds(sl * yslab + s, R, stride=Y_TILES), :] = jnp.bitwise_or(
                jnp.right_shift(lo_w, jnp.uint32(16)), hi_w)

    def block(i, carry):
        g = g0 + i
        sl = lax.rem(g, N_SLOTS)
        for k in range(N_SLOTS):
            @pl.when(sl == k)
            def _(k=k):
                run_block(g, k)
        return carry

    lax.fori_loop(0, nb_e, block, 0)

    @pl.when(e == pl.num_programs(0) - 1)
    def _():
        g_end = g0 + nb_e
        sl_end = lax.rem(g_end, N_SLOTS)
        for k in range(N_SLOTS):
            @pl.when(sl_end == k)
            def _(k=k):
                scatter_start((k + N_SLOTS - 1) % N_SLOTS, 0, R)
                idx_copy(g_end + 1, (k + 1) % N_SLOTS).wait()
        for k in range(N_SLOTS):
            slab_wait(k)


def _moe_call(bstart, nblk, pk, hn2p, wg, wu, wd, n_tok):
    R = R_BLK
    slab = R * ROW_TILES
    grid_spec = pltpu.PrefetchScalarGridSpec(
        num_scalar_prefetch=2,
        grid=(N_EXPERTS,),
        in_specs=[
            pl.BlockSpec(memory_space=pl.ANY),
            pl.BlockSpec(memory_space=pl.ANY),
            pl.BlockSpec((1, D_MODEL, D_FF), lambda e, bs, nb: (e, 0, 0)),
            pl.BlockSpec((1, D_MODEL, D_FF), lambda e, bs, nb: (e, 0, 0)),
            pl.BlockSpec((1, D_FF, D_MODEL), lambda e, bs, nb: (e, 0, 0)),
        ],
        out_specs=pl.BlockSpec(memory_space=pl.ANY),
        scratch_shapes=[
            pltpu.VMEM(hn2p.shape, jnp.uint32),
            pltpu.VMEM((2 * slab, LANES), F32),
            pltpu.VMEM((N_SLOTS * R * Y_TILES, LANES), jnp.uint32),
            pltpu.VMEM((D_MODEL, 2 * D_FF), BF16),
            pltpu.VMEM((D_FF, D_MODEL), BF16),
            pltpu.SMEM((N_SLOTS, SUBLANES, LANES), I32),
            pltpu.SemaphoreType.DMA((1,)),
            pltpu.SemaphoreType.DMA((N_SLOTS,)),
            pltpu.SemaphoreType.DMA((N_SLOTS,)),
        ],
    )
    return pl.pallas_call(
        functools.partial(_moe_kernel, n_tok=n_tok),
        grid_spec=grid_spec,
        out_shape=jax.ShapeDtypeStruct(((2 * n_tok + N_SLOTS * R) * Y_TILES, LANES), jnp.uint32),
        compiler_params=pltpu.CompilerParams(
            dimension_semantics=("arbitrary",),
            vmem_limit_bytes=VMEM_LIMIT),
        name="moe",
    )(bstart, nblk, pk, hn2p, wg, wu, wd)


def _unpack_rows(y_ref, rows):
    parts = []
    for s in range(Y_TILES):
        w = y_ref[pl.ds(s, rows, stride=Y_TILES), :]
        parts.append(pltpu.bitcast(jnp.left_shift(w, jnp.uint32(16)), F32))
        parts.append(pltpu.bitcast(jnp.bitwise_and(w, jnp.uint32(0xFFFF0000)), F32))
    return jnp.concatenate(parts, axis=1)


def _combine_kernel(h_ref, y0_ref, y1_ref, g_ref, fn_ref, o_ref):
    T = T_CMB
    h = h_ref[...]
    y0 = _unpack_rows(y0_ref, T)
    y1 = _unpack_rows(y1_ref, T)
    g = g_ref[...]
    v = h + (y0 * g[:, 0:1] + y1 * g[:, 1:2])
    ms = jnp.mean(v * v, axis=-1, keepdims=True)
    o_ref[...] = v * lax.rsqrt(ms + EPS) * fn_ref[...]


def _combine_call(h2d, y2, gates, fnorm):
    n_tok, D = h2d.shape
    T = T_CMB
    nt = n_tok // T
    return pl.pallas_call(
        _combine_kernel,
        grid=(nt,),
        in_specs=[
            pl.BlockSpec((T, D), lambda i: (i, 0)),
            pl.BlockSpec((T * Y_TILES, LANES), lambda i: (i, 0)),
            pl.BlockSpec((T * Y_TILES, LANES), lambda i: (nt + i, 0)),
            pl.BlockSpec((T, LANES), lambda i: (i, 0)),
            pl.BlockSpec((1, D), lambda i: (0, 0)),
        ],
        out_specs=pl.BlockSpec((T, D), lambda i: (i, 0)),
        out_shape=jax.ShapeDtypeStruct((n_tok, D), F32),
        compiler_params=pltpu.CompilerParams(dimension_semantics=("arbitrary",)),
        name="combine",
    )(h2d, y2, y2, gates, fnorm)


def _plan_kernel(e_ref, ux_ref, ones_ref, lx_ref, dest_ref, bstart_ref, nblk_ref):
    rows = e_ref.shape[0]
    ev = e_ref[...]
    lane8 = lax.broadcasted_iota(I32, (SUBLANES, LANES), 1)
    dest = jnp.zeros((rows, LANES), F32)
    bstart = jnp.zeros((SUBLANES, LANES), I32)
    nblk = jnp.zeros((SUBLANES, LANES), I32)
    pstart = jnp.zeros((1, LANES), F32)
    for e in range(N_EXPERTS):
        ohb = ev == e
        oh = jnp.where(ohb, 1.0, 0.0).astype(BF16)
        within = _dot(oh, ux_ref[...])
        rtot = _dot(oh, ones_ref[...])
        rpre = _dot(lx_ref[...], rtot.astype(BF16))
        cnt = rpre[rows - 1:rows, :] + rtot[rows - 1:rows, :]
        dest = dest + jnp.where(ohb, within + rpre + pstart, 0.0)
        cnt_i = cnt.astype(I32)
        nb_e = lax.shift_right_logical(cnt_i + (R_BLK - 1), R_SHIFT)
        bstart = jnp.where(lane8 == e, lax.shift_right_logical(pstart.astype(I32), R_SHIFT), bstart)
        nblk = jnp.where(lane8 == e, nb_e, nblk)
        pstart = pstart + lax.shift_left(nb_e, R_SHIFT).astype(F32)
    dest_ref[...] = dest.astype(I32)
    bstart_ref[...] = bstart
    nblk_ref[...] = nblk


def _invert_kernel(dest_ref, init_ref, inv_ref):
    n_asg = dest_ref.shape[0]
    pltpu.sync_copy(init_ref, inv_ref)

    def put(a, c):
        inv_ref[dest_ref[a]] = a
        return c

    lax.fori_loop(0, n_asg, put, 0, unroll=16)


def _index_tiles_kernel(inv_ref, pk_ref, *, n_tok):
    nb = pk_ref.shape[0] // SUBLANES
    per_blk = R_BLK // LANES
    assert per_blk == 2
    pk_ref[...] = jnp.zeros(pk_ref.shape, I32)
    for half in range(per_blk):
        a = inv_ref[pl.ds(half, nb, stride=per_blk), :]
        t = jnp.bitwise_and(a, n_tok - 1)
        pk_ref[pl.ds(half, nb, stride=SUBLANES), :] = lax.shift_left(lax.shift_right_logical(t, 1), 3)
        pk_ref[pl.ds(2 + half, nb, stride=SUBLANES), :] = lax.shift_left(jnp.bitwise_and(a, 1), 4)
        pk_ref[pl.ds(4 + half, nb, stride=SUBLANES), :] = a * Y_TILES


def _plan(e_rows, n_tok, nb):
    R = R_BLK
    n_asg = 2 * n_tok
    rows = n_asg // LANES
    emat = e_rows[0:2].reshape(rows, LANES)
    li = jnp.arange(LANES)
    ux = (li[:, None] < li[None, :]).astype(BF16)
    ones = jnp.ones((LANES, LANES), BF16)
    ri = jnp.arange(rows)
    lx = (ri[:, None] > ri[None, :]).astype(BF16)
    dest, bstart, nblk = pl.pallas_call(
        _plan_kernel,
        out_shape=[jax.ShapeDtypeStruct((rows, LANES), I32),
                   jax.ShapeDtypeStruct((SUBLANES, LANES), I32),
                   jax.ShapeDtypeStruct((SUBLANES, LANES), I32)],
        name="plan",
    )(emat, ux, ones, lx)
    pos = jnp.arange(nb * R, dtype=I32)
    init = n_asg + (pos & (R - 1)) + jnp.where(pos >= (nb - 1) * R, R, 0)
    inv = pl.pallas_call(
        _invert_kernel,
        in_specs=[pl.BlockSpec(memory_space=pltpu.SMEM), pl.BlockSpec(memory_space=pl.ANY)],
        out_specs=pl.BlockSpec(memory_space=pltpu.SMEM),
        out_shape=jax.ShapeDtypeStruct((nb * R,), I32),
        name="invert",
    )(dest.reshape(n_asg), init)
    pk = pl.pallas_call(
        functools.partial(_index_tiles_kernel, n_tok=n_tok),
        out_shape=jax.ShapeDtypeStruct((nb * SUBLANES, LANES), I32),
        name="index_tiles",
    )(inv.reshape(nb * R // LANES, LANES))
    return bstart[0, 0:N_EXPERTS], nblk[0, 0:N_EXPERTS], pk.reshape(nb, SUBLANES, LANES)


def kernel(x, norm_mix, w_in, ssd_conv_w, ssd_conv_b, dt_bias, a_log, d_skip, ssd_norm, sc_conv_w,
           sc_norm, w_out, norm_ffn, w_router_group, w_router_expert, w_gate, w_up, w_down, final_norm):
    B, L, D = x.shape
    n_tok = B * L
    depth = norm_mix.shape[0]
    assert depth == 1 and D == D_MODEL and (n_tok & (n_tok - 1)) == 0
    nb = -(-((2 * n_tok) // R_BLK + N_EXPERTS + 1) // SUBLANES) * SUBLANES

    o1 = SSD_WIDTH
    o2 = o1 + XBC
    o3 = o2 + N_HEADS
    wi = w_in.reshape(D, -1)
    wa = wi[:, 0:o2].astype(BF16)
    wb = wi[:, o3:].astype(BF16)
    wdt = jnp.pad(wi[:, o2:o3], ((0, 0), (0, LANES - N_HEADS))).astype(BF16)
    pad_h = (0, LANES - N_HEADS)
    dtb = jnp.pad(dt_bias[0], pad_h).reshape(1, LANES)
    alog = jnp.pad(a_log[0], pad_h).reshape(1, LANES)
    dskip = jnp.repeat(d_skip[0], HEAD_DIM).reshape(1, SSD_WIDTH)

    wre = jnp.transpose(w_router_expert[0], (1, 0, 2)).reshape(D, N_EXPERTS)
    wrt = jnp.pad(jnp.concatenate([w_router_group[0], wre], axis=1).T,
                  ((0, ROUTER_ROWS - N_GROUPS - N_EXPERTS), (0, 0)))
    wrt_hi = wrt.astype(BF16)
    wr = jnp.concatenate([wrt_hi, (wrt - wrt_hi.astype(F32)).astype(BF16)], axis=0)

    ri = jnp.arange(T_MIX)
    tri = (ri[:, None] >= ri[None, :]).astype(BF16)
    er = jnp.arange(LANES)
    ec = jnp.arange(SSD_WIDTH)
    e3 = ((er[:, None] < 48) & ((er[:, None] % 16) == (ec[None, :] // HEAD_DIM))).astype(BF16)
    gsum = ((ec[:, None] // HEAD_DIM) == er[None, :]).astype(BF16)

    h, hn2, e_rows, gates = _mixer_call(
        x, norm_mix[0].reshape(1, D), wa, wb, wdt, ssd_conv_w[0], ssd_conv_b[0].reshape(1, XBC), dtb, alog,
        dskip, ssd_norm[0].reshape(1, SSD_WIDTH), sc_conv_w[0], sc_norm[0].reshape(1, SC_WIDTH),
        w_out.reshape(-1, D).astype(BF16), norm_ffn[0].reshape(1, D), wr, tri, e3, gsum)

    bstart, nblk, pk = _plan(e_rows, n_tok, nb)
    y2 = _moe_call(bstart, nblk, pk, hn2, w_gate.reshape(N_EXPERTS, D, D_FF),
                   w_up.reshape(N_EXPERTS, D, D_FF), w_down.reshape(N_EXPERTS, D_FF, D), n_tok)
    out = _combine_call(h.reshape(n_tok, D), y2, gates, final_norm.reshape(1, D))
    return out.reshape(B, L, D)
```

```python
import functools

import jax
import jax.numpy as jnp
from jax import lax
from jax.experimental import pallas as pl
from jax.experimental.pallas import tpu as pltpu

F32 = jnp.float32
BF16 = jnp.bfloat16
I32 = jnp.int32

EPS = 1e-6
D_MODEL = 1024
N_HEADS = 16
HEAD_DIM = 64
N_BC_GROUPS = 2
STATE = 128
SSD_WIDTH = 1024
XBC = SSD_WIDTH + 2 * N_BC_GROUPS * STATE
SC_WIDTH = 1024
SC_GROUPS = 16
N_GROUPS = 4
EPG = 8
N_EXPERTS = 32
D_FF = 512

LANES = 128
SUBLANES = 8
ROW_TILES = D_MODEL // LANES
Y_TILES = ROW_TILES // 2

T_MIX = 512
Q_SSD = 128
COL_CHUNK = 512
PAIR_ROWS = T_MIX // 2
R_BLK = 256
R_SHIFT = 8
ROUTER_ROWS = 48
N_SLOTS = 4
T_CMB = 1024

VMEM_LIMIT = 56 * 1024 * 1024


def _dot(a, b):
    return jnp.dot(a, b, preferred_element_type=F32)


def _dot_t(a, b):
    return lax.dot_general(a, b, (((1,), (1,)), ((), ())), preferred_element_type=F32)


def _split3(v):
    p1 = v.astype(BF16).astype(F32)
    r1 = v - p1
    p2 = r1.astype(BF16).astype(F32)
    p3 = (r1 - p2).astype(BF16).astype(F32)
    return p1, p2, p3


def _pack3(v):
    p1, p2, p3 = _split3(v)
    return (p1 + pltpu.roll(p2, 16, 1) + pltpu.roll(p3, 32, 1)).astype(BF16)


def _silu(v):
    return v * jax.nn.sigmoid(v)


def _mixer_kernel(x_ref, gmix_ref, wa_ref, wb_ref, wdt_ref, convw_ref, convb_ref, dtb_ref, alog_ref,
                  dskip_ref, ssdn_ref, scw_ref, scn_ref, wout_ref, gffn_ref, wr_ref,
                  tri_ref, e3_ref, gsum_ref,
                  h_ref, hn2_ref, e_ref, g_ref,
                  cbuf, sbuf, st_ref, pbuf):
    T = T_MIX
    t = pl.program_id(1)

    @pl.when(t == 0)
    def _():
        cbuf[0:8, :] = jnp.zeros((8, XBC), F32)
        sbuf[0:8, :] = jnp.zeros((8, SC_WIDTH), F32)
        st_ref[...] = jnp.zeros(st_ref.shape, F32)

    @pl.when(t > 0)
    def _():
        cbuf[0:8, :] = cbuf[T:T + 8, :]
        sbuf[0:8, :] = sbuf[T:T + 8, :]

    x = x_ref[0]
    ms = jnp.mean(x * x, axis=-1, keepdims=True)
    hn = (x * lax.rsqrt(ms + EPS) * gmix_ref[...]).astype(BF16)

    dt_raw = _dot_t(hn, wdt_ref[...])

    cw = convw_ref[...]
    xact_parts = []
    for c0 in range(0, XBC, COL_CHUNK):
        c1 = c0 + COL_CHUNK
        xbc = _dot_t(hn, wa_ref[1024 + c0:1024 + c1, :])
        cbuf[8:8 + T, c0:c1] = xbc
        acc = convb_ref[:, c0:c1] + cw[3:4, c0:c1] * xbc
        for k in range(3):
            acc = acc + cw[k:k + 1, c0:c1] * cbuf[5 + k:5 + k + T, c0:c1]
        xact_parts.append(_silu(acc))
    xact = jnp.concatenate(xact_parts, axis=1)
    xs = xact[:, 0:SSD_WIDTH]

    lane = lax.broadcasted_iota(I32, (1, LANES), 1)
    hmask = lane < N_HEADS
    a = jnp.where(hmask, -jnp.exp(alog_ref[...]), 0.0)
    dtv = dt_raw + dtb_ref[...]
    dt = jnp.where(hmask, jnp.maximum(dtv, 0.0) + jnp.log1p(jnp.exp(-jnp.abs(dtv))), 0.0)
    adt = dt * a

    sw = scw_ref[...]
    sc_parts = []
    for c0 in range(0, SC_WIDTH, COL_CHUNK):
        c1 = c0 + COL_CHUNK
        u = _dot_t(hn, wb_ref[1024 + c0:1024 + c1, :]) * _dot_t(hn, wb_ref[2048 + c0:2048 + c1, :])
        sbuf[8:8 + T, c0:c1] = u
        conv = (sw[2:3, c0:c1] * u + sw[1:2, c0:c1] * sbuf[7:7 + T, c0:c1]
                + sw[0:1, c0:c1] * sbuf[6:6 + T, c0:c1])
        sc_parts.append(_dot_t(hn, wb_ref[c0:c1, :]) * conv)
    sc = jnp.concatenate(sc_parts, axis=1)
    z_lo = _dot_t(hn, wa_ref[0:COL_CHUNK, :])
    gs = _dot((sc * sc).astype(BF16), gsum_ref[...])
    z_hi = _dot_t(hn, wa_ref[COL_CHUNK:2 * COL_CHUNK, :])
    rstd = jnp.where(lax.broadcasted_iota(I32, (1, LANES), 1) < SC_GROUPS,
                     lax.rsqrt(gs * (1.0 / HEAD_DIM) + EPS), 0.0)
    rstd_e = _dot(_pack3(rstd), e3_ref[...])
    sc_out = (sc * rstd_e * scn_ref[...]).astype(BF16)
    z = jnp.concatenate([z_lo, z_hi], axis=1)

    c3 = _dot(tri_ref[...], _pack3(adt))
    ac = jnp.where(hmask, c3 + pltpu.roll(c3, LANES - 16, 1) + pltpu.roll(c3, LANES - 32, 1), 0.0)
    Q = Q_SSD
    n_sub = T // Q
    ends = [ac[(c + 1) * Q - 1:(c + 1) * Q, :] for c in range(n_sub)]
    base = jnp.concatenate([jnp.broadcast_to(ends[c - 1] if c else jnp.zeros_like(ends[0]), (Q, LANES))
                            for c in range(n_sub)], axis=0)
    endv = jnp.concatenate([jnp.broadcast_to(ends[c], (Q, LANES)) for c in range(n_sub)], axis=0)
    eac = jnp.where(hmask, jnp.exp(ac - base), 0.0)
    wdt = dt * jnp.exp(endv - ac)

    stacked = jnp.concatenate([_pack3(dt), _pack3(wdt), _pack3(eac)], axis=0)
    ex = _dot(stacked, e3_ref[...])
    dt_e = ex[0:T]
    wdt_e = ex[T:2 * T]
    eac_e = ex[2 * T:3 * T]
    xdt = (xs * dt_e).astype(BF16)
    xdtw = (xs * wdt_e).astype(BF16)

    mix_sc = []
    mix_todo = list(range(0, D_MODEL, COL_CHUNK))

    def emit_mix():
        if mix_todo:
            c0 = mix_todo.pop(0)
            mix_sc.append(_dot(sc_out, wout_ref[SSD_WIDTH:SSD_WIDTH + SC_WIDTH, c0:c0 + COL_CHUNK]))

    ac_rows = ac.T
    rr = lax.broadcasted_iota(I32, (Q, Q), 0)
    cc = lax.broadcasted_iota(I32, (Q, Q), 1)
    causal = rr >= cc
    cblk = lax.shift_right_logical(lax.broadcasted_iota(I32, (Q, 4 * HEAD_DIM), 1), 6)

    y_cols = []
    for g in range(N_BC_GROUPS):
        bg = xact[:, SSD_WIDTH + STATE * g:SSD_WIDTH + STATE * (g + 1)]
        cg = xact[:, SSD_WIDTH + 2 * STATE + STATE * g:SSD_WIDTH + 2 * STATE + STATE * (g + 1)]
        st = st_ref[g]
        y_rows = [[], []]
        for c in range(n_sub):
            r0, r1 = c * Q, (c + 1) * Q
            bb = bg[r0:r1].astype(BF16)
            cbf = cg[r0:r1].astype(BF16)
            cb = lax.dot_general(cbf, bb, (((1,), (1,)), ((), ())), preferred_element_type=F32)
            yoff = _dot(cbf, st.astype(BF16))
            dec = eac_e[r1 - 1:r1, 512 * g:512 * (g + 1)]
            bgt = bg[r0:r1].T.astype(BF16)
            st = st * dec + _dot(bgt, xdtw[r0:r1, 512 * g:512 * (g + 1)])
            for q in range(2):
                ms_list = []
                for r in range(4):
                    hh = 8 * g + 4 * q + r
                    seg = ac[r0:r1, hh:hh + 1] - ac_rows[hh:hh + 1, r0:r1]
                    lh = jnp.exp(jnp.where(causal, seg, -jnp.inf))
                    ms_list.append((cb * lh).astype(BF16))
                lhs = jnp.concatenate(ms_list, axis=1)
                if c == 1 and q == 0:
                    emit_mix()
                lo = 512 * g + 256 * q
                x4 = xdt[r0:r1, lo:lo + 256]
                rhs = jnp.concatenate(
                    [jnp.where(cblk == r, x4, jnp.zeros_like(x4)) for r in range(4)], axis=0)
                yd = _dot(lhs, rhs)
                y_rows[q].append(yd + eac_e[r0:r1, lo:lo + 256] * yoff[:, 256 * q:256 * (q + 1)])
        st_ref[g] = st
        y_cols += [jnp.concatenate(y_rows[0], axis=0), jnp.concatenate(y_rows[1], axis=0)]
    y = jnp.concatenate(y_cols, axis=1) + dskip_ref[...] * xs

    while mix_todo:
        emit_mix()
    v = y * _silu(z)
    outs = []
    for g in range(N_BC_GROUPS):
        vg = v[:, 512 * g:512 * (g + 1)]
        msg = jnp.mean(vg * vg, axis=-1, keepdims=True)
        outs.append(vg * lax.rsqrt(msg + EPS))
    ssd_out = (jnp.concatenate(outs, axis=1) * ssdn_ref[...]).astype(BF16)

    mix = _dot(ssd_out, wout_ref[0:SSD_WIDTH, :]) + jnp.concatenate(mix_sc, axis=1)
    h = x + mix
    h_ref[0] = h

    ms2 = jnp.mean(h * h, axis=-1, keepdims=True)
    hn2 = h * lax.rsqrt(ms2 + EPS) * gffn_ref[...]
    hi = hn2.astype(BF16)
    bits = pltpu.bitcast(hi.astype(F32), jnp.uint32)
    for s in range(ROW_TILES):
        pbuf[pl.ds(s * T, T), :] = bits[:, LANES * s:LANES * (s + 1)]
    for s in range(ROW_TILES):
        even = pbuf[pl.ds(s * T, PAIR_ROWS, stride=2), :]
        odd = pbuf[pl.ds(s * T + 1, PAIR_ROWS, stride=2), :]
        hn2_ref[pl.ds(s, PAIR_ROWS, stride=ROW_TILES), :] = jnp.bitwise_or(
            jnp.right_shift(even, jnp.uint32(16)), odd)
    lo_ = (hn2 - hi.astype(F32)).astype(BF16)
    RT = ROUTER_ROWS
    nt_dims = (((1,), (1,)), ((), ()))
    both = lax.dot_general(wr_ref[...], hi, nt_dims, preferred_element_type=F32)
    low = lax.dot_general(wr_ref[0:RT, :], lo_, nt_dims, preferred_element_type=F32)
    logits = both[0:RT] + (low + both[RT:2 * RT])

    ri = lax.broadcasted_iota(I32, (RT, T), 0)
    ri_f = ri.astype(F32)
    neg = -jnp.inf
    big = 1e9
    gl = jnp.where(ri < N_GROUPS, logits, neg)
    gmax = jnp.max(gl, axis=0, keepdims=True)
    gidx = jnp.min(jnp.where(gl == gmax, ri_f, big), axis=0, keepdims=True)
    gsum = jnp.sum(jnp.where(ri < N_GROUPS, jnp.exp(logits - gmax), 0.0), axis=0, keepdims=True)
    gw = 1.0 / gsum
    egrp = lax.shift_right_logical(ri - N_GROUPS, 3).astype(F32)
    in_grp = (ri >= N_GROUPS) & (ri < N_GROUPS + N_EXPERTS) & (egrp == gidx)
    el = jnp.where(in_grp, logits, neg)
    v1 = jnp.max(el, axis=0, keepdims=True)
    i1 = jnp.min(jnp.where(el == v1, ri_f, big), axis=0, keepdims=True)
    el2 = jnp.where(ri_f == i1, neg, el)
    v2 = jnp.max(el2, axis=0, keepdims=True)
    i2 = jnp.min(jnp.where(el2 == v2, ri_f, big), axis=0, keepdims=True)
    p = jnp.exp(v2 - v1)
    s1 = 1.0 / (1.0 + p)
    gate1 = gw * s1
    gate2 = gw * (p * s1)
    r8 = lax.broadcasted_iota(I32, (SUBLANES, T), 0)
    e_ref[...] = jnp.where(r8 == 0, i1 - N_GROUPS, jnp.where(r8 == 1, i2 - N_GROUPS, 0.0)).astype(I32)
    rl = lax.broadcasted_iota(I32, (LANES, T), 0)
    g_ref[...] = jnp.where(rl == 0, gate1, jnp.where(rl == 1, gate2, 0.0)).T


def _mixer_call(x, gmix, wa, wb, wdt, convw, convb, dtb, alog, dskip, ssdn, scw, scn, wout, gffn,
                wr, tri, e3, gsum):
    B, L, D = x.shape
    T = T_MIX
    nt = L // T
    n_tok = B * L

    def const(shape):
        return pl.BlockSpec(shape, lambda b, t: (0,) * len(shape), pipeline_mode=pl.Buffered(1))

    in_specs = [
        pl.BlockSpec((1, T, D), lambda b, t: (b, t, 0)),
        const((1, D)),
        const(wa.shape), const(wb.shape), const(wdt.shape), const(convw.shape), const(convb.shape),
        const(dtb.shape), const(alog.shape), const(dskip.shape), const(ssdn.shape),
        const(scw.shape), const(scn.shape), const(wout.shape), const(gffn.shape),
        const(wr.shape), const(tri.shape), const(e3.shape), const(gsum.shape),
    ]
    out_shape = [
        jax.ShapeDtypeStruct((B, L, D), F32),
        jax.ShapeDtypeStruct((n_tok // 2 * ROW_TILES, LANES), jnp.uint32),
        jax.ShapeDtypeStruct((8, n_tok), I32),
        jax.ShapeDtypeStruct((n_tok, LANES), F32),
    ]
    out_specs = [
        pl.BlockSpec((1, T, D), lambda b, t: (b, t, 0)),
        pl.BlockSpec((PAIR_ROWS * ROW_TILES, LANES), lambda b, t: (b * nt + t, 0)),
        pl.BlockSpec((8, T), lambda b, t: (0, b * nt + t)),
        pl.BlockSpec((T, LANES), lambda b, t: (b * nt + t, 0)),
    ]
    return pl.pallas_call(
        _mixer_kernel,
        grid=(B, nt),
        in_specs=in_specs,
        out_specs=out_specs,
        out_shape=out_shape,
        scratch_shapes=[
            pltpu.VMEM((T + 8, XBC), F32),
            pltpu.VMEM((T + 8, SC_WIDTH), F32),
            pltpu.VMEM((N_BC_GROUPS, STATE, 512), F32),
            pltpu.VMEM((ROW_TILES * T, LANES), jnp.uint32),
        ],
        compiler_params=pltpu.CompilerParams(
            dimension_semantics=("arbitrary", "arbitrary"),
            vmem_limit_bytes=VMEM_LIMIT),
        name="mixer",
    )(x, gmix, wa, wb, wdt, convw, convb, dtb, alog, dskip, ssdn, scw, scn, wout, gffn, wr,
      tri, e3, gsum)


def _moe_kernel(bstart_ref, nblk_ref, pk_ref, hn2p_ref, wg_ref, wu_ref, wd_ref,
                y2_ref, hn2v, xbuf, ybuf, wgu, wdb, idx_ref, lsem, ssem, isem, *, n_tok):
    R = R_BLK
    slab = R * ROW_TILES
    e = pl.program_id(0)
    g0 = bstart_ref[e]
    nb_e = nblk_ref[e]

    def idx_word(sl, field, j):
        return idx_ref[sl, 2 * field + j // LANES, j % LANES]

    def idx_copy(g, sl):
        return pltpu.make_async_copy(pk_ref.at[g], idx_ref.at[sl], isem.at[sl])

    def gather_rows(sl, xs):
        for j in range(R):
            row = pl.multiple_of(idx_word(sl, 0, j), ROW_TILES)
            sh = idx_word(sl, 1, j).astype(jnp.uint32)
            w = hn2v[pl.ds(row, ROW_TILES), :]
            w = jnp.left_shift(jnp.right_shift(w, sh), jnp.uint32(16))
            xbuf[pl.ds(xs * slab + j * ROW_TILES, ROW_TILES), :] = pltpu.bitcast(w, F32)

    yslab = R * Y_TILES
    FH = D_FF // 2

    def y_copy(sl, j, dst_row):
        return pltpu.make_async_copy(
            ybuf.at[pl.ds(sl * yslab + j * Y_TILES, Y_TILES), :],
            y2_ref.at[pl.ds(pl.multiple_of(dst_row, Y_TILES), Y_TILES), :],
            ssem.at[sl])

    def scatter_start(sl, lo, hi):
        for j in range(lo, hi):
            y_copy(sl, j, idx_word(sl, 2, j)).start(priority=j % 2)

    def slab_wait(sl):
        view = ybuf.at[pl.ds(sl * yslab, yslab), :]
        pltpu.make_async_copy(view, view, ssem.at[sl]).wait()

    n_idx_blocks = pk_ref.shape[0]

    @pl.when(e == 0)
    def _():
        load = pltpu.make_async_copy(hn2p_ref, hn2v, lsem.at[0])
        load.start()
        first = idx_copy(0, 0)
        first.start()
        idx_copy(1, 1).start()
        prev = idx_copy(n_idx_blocks - 1, N_SLOTS - 1)
        prev.start()
        ybuf[...] = jnp.zeros(ybuf.shape, ybuf.dtype)
        for sl in range(N_SLOTS - 1):
            base = 2 * n_tok + (sl + 1 if sl else 0) * R
            for j in range(R):
                y_copy(sl, j, (base + j) * Y_TILES).start(priority=j % 2)
        prev.wait()
        first.wait()
        load.wait()
        gather_rows(0, 0)

    @pl.when(nb_e > 0)
    def _():
        for j in range(2):
            wgu[:, 2 * j * FH:(2 * j + 1) * FH] = wg_ref[0, :, j * FH:(j + 1) * FH].astype(BF16)
            wgu[:, (2 * j + 1) * FH:(2 * j + 2) * FH] = wu_ref[0, :, j * FH:(j + 1) * FH].astype(BF16)
        wdb[...] = wd_ref[0].astype(BF16)

    c1, c2 = R // 3, 2 * (R // 3)

    def run_block(g, sl):
        prv = (sl + N_SLOTS - 1) % N_SLOTS
        nx1 = (sl + 1) % N_SLOTS
        nx2 = (sl + 2) % N_SLOTS
        idx_copy(g + 1, nx1).wait()
        idx_copy(g + 2, nx2).start()
        slab_wait(sl)
        x = jnp.concatenate(
            [xbuf[pl.ds((sl % 2) * slab + s, R, stride=ROW_TILES), :] for s in range(ROW_TILES)],
            axis=1).astype(BF16)
        scatter_start(prv, 0, c1)
        gu = _dot(x, wgu[...])
        gather_rows(nx1, (sl + 1) % 2)
        scatter_start(prv, c1, c2)
        act = jnp.concatenate(
            [_silu(gu[:, 2 * j * FH:(2 * j + 1) * FH]) * gu[:, (2 * j + 1) * FH:(2 * j + 2) * FH]
             for j in range(2)], axis=1).astype(BF16)
        scatter_start(prv, c2, R)
        yy = _dot(act, wdb[...])
        for s in range(Y_TILES):
            lo_w = pltpu.bitcast(yy[:, 2 * LANES * s:2 * LANES * s + LANES].astype(BF16).astype(F32), jnp.uint32)
            hi_w = pltpu.bitcast(yy[:, 2 * LANES * s + LANES:2 * LANES * (s + 1)].astype(BF16).astype(F32),
                                 jnp.uint32)
            ybuf[pl.ds(sl * yslab + s, R, stride=Y_TILES), :] = jnp.bitwise_or(
                jnp.right_shift(lo_w, jnp.uint32(16)), hi_w)

    def block(i, carry):
        g = g0 + i
        sl = lax.rem(g, N_SLOTS)
        for k in range(N_SLOTS):
            @pl.when(sl == k)
            def _(k=k):
                run_block(g, k)
        return carry

    lax.fori_loop(0, nb_e, block, 0)

    @pl.when(e == pl.num_programs(0) - 1)
    def _():
        g_end = g0 + nb_e
        sl_end = lax.rem(g_end, N_SLOTS)
        for k in range(N_SLOTS):
            @pl.when(sl_end == k)
            def _(k=k):
                scatter_start((k + N_SLOTS - 1) % N_SLOTS, 0, R)
                idx_copy(g_end + 1, (k + 1) % N_SLOTS).wait()
        for k in range(N_SLOTS):
            slab_wait(k)


def _moe_call(bstart, nblk, pk, hn2p, wg, wu, wd, n_tok):
    R = R_BLK
    slab = R * ROW_TILES
    grid_spec = pltpu.PrefetchScalarGridSpec(
        num_scalar_prefetch=2,
        grid=(N_EXPERTS,),
        in_specs=[
            pl.BlockSpec(memory_space=pl.ANY),
            pl.BlockSpec(memory_space=pl.ANY),
            pl.BlockSpec((1, D_MODEL, D_FF), lambda e, bs, nb: (e, 0, 0)),
            pl.BlockSpec((1, D_MODEL, D_FF), lambda e, bs, nb: (e, 0, 0)),
            pl.BlockSpec((1, D_FF, D_MODEL), lambda e, bs, nb: (e, 0, 0)),
        ],
        out_specs=pl.BlockSpec(memory_space=pl.ANY),
        scratch_shapes=[
            pltpu.VMEM(hn2p.shape, jnp.uint32),
            pltpu.VMEM((2 * slab, LANES), F32),
            pltpu.VMEM((N_SLOTS * R * Y_TILES, LANES), jnp.uint32),
            pltpu.VMEM((D_MODEL, 2 * D_FF), BF16),
            pltpu.VMEM((D_FF, D_MODEL), BF16),
            pltpu.SMEM((N_SLOTS, SUBLANES, LANES), I32),
            pltpu.SemaphoreType.DMA((1,)),
            pltpu.SemaphoreType.DMA((N_SLOTS,)),
            pltpu.SemaphoreType.DMA((N_SLOTS,)),
        ],
    )
    return pl.pallas_call(
        functools.partial(_moe_kernel, n_tok=n_tok),
        grid_spec=grid_spec,
        out_shape=jax.ShapeDtypeStruct(((2 * n_tok + N_SLOTS * R) * Y_TILES, LANES), jnp.uint32),
        compiler_params=pltpu.CompilerParams(
            dimension_semantics=("arbitrary",),
            vmem_limit_bytes=VMEM_LIMIT),
        name="moe",
    )(bstart, nblk, pk, hn2p, wg, wu, wd)


def _unpack_rows(y_ref, rows):
    parts = []
    for s in range(Y_TILES):
        w = y_ref[pl.ds(s, rows, stride=Y_TILES), :]
        parts.append(pltpu.bitcast(jnp.left_shift(w, jnp.uint32(16)), F32))
        parts.append(pltpu.bitcast(jnp.bitwise_and(w, jnp.uint32(0xFFFF0000)), F32))
    return jnp.concatenate(parts, axis=1)


def _combine_kernel(h_ref, y0_ref, y1_ref, g_ref, fn_ref, o_ref):
    T = T_CMB
    h = h_ref[...]
    y0 = _unpack_rows(y0_ref, T)
    y1 = _unpack_rows(y1_ref, T)
    g = g_ref[...]
    v = h + (y0 * g[:, 0:1] + y1 * g[:, 1:2])
    ms = jnp.mean(v * v, axis=-1, keepdims=True)
    o_ref[...] = v * lax.rsqrt(ms + EPS) * fn_ref[...]


def _combine_call(h2d, y2, gates, fnorm):
    n_tok, D = h2d.shape
    T = T_CMB
    nt = n_tok // T
    return pl.pallas_call(
        _combine_kernel,
        grid=(nt,),
        in_specs=[
            pl.BlockSpec((T, D), lambda i: (i, 0)),
            pl.BlockSpec((T * Y_TILES, LANES), lambda i: (i, 0)),
            pl.BlockSpec((T * Y_TILES, LANES), lambda i: (nt + i, 0)),
            pl.BlockSpec((T, LANES), lambda i: (i, 0)),
            pl.BlockSpec((1, D), lambda i: (0, 0)),
        ],
        out_specs=pl.BlockSpec((T, D), lambda i: (i, 0)),
        out_shape=jax.ShapeDtypeStruct((n_tok, D), F32),
        compiler_params=pltpu.CompilerParams(dimension_semantics=("arbitrary",)),
        name="combine",
    )(h2d, y2, y2, gates, fnorm)


def _plan_kernel(e_ref, ux_ref, ones_ref, lx_ref, dest_ref, bstart_ref, nblk_ref):
    rows = e_ref.shape[0]
    ev = e_ref[...]
    lane8 = lax.broadcasted_iota(I32, (SUBLANES, LANES), 1)
    dest = jnp.zeros((rows, LANES), F32)
    bstart = jnp.zeros((SUBLANES, LANES), I32)
    nblk = jnp.zeros((SUBLANES, LANES), I32)
    pstart = jnp.zeros((1, LANES), F32)
    for e in range(N_EXPERTS):
        ohb = ev == e
        oh = jnp.where(ohb, 1.0, 0.0).astype(BF16)
        within = _dot(oh, ux_ref[...])
        rtot = _dot(oh, ones_ref[...])
        rpre = _dot(lx_ref[...], rtot.astype(BF16))
        cnt = rpre[rows - 1:rows, :] + rtot[rows - 1:rows, :]
        dest = dest + jnp.where(ohb, within + rpre + pstart, 0.0)
        cnt_i = cnt.astype(I32)
        nb_e = lax.shift_right_logical(cnt_i + (R_BLK - 1), R_SHIFT)
        bstart = jnp.where(lane8 == e, lax.shift_right_logical(pstart.astype(I32), R_SHIFT), bstart)
        nblk = jnp.where(lane8 == e, nb_e, nblk)
        pstart = pstart + lax.shift_left(nb_e, R_SHIFT).astype(F32)
    dest_ref[...] = dest.astype(I32)
    bstart_ref[...] = bstart
    nblk_ref[...] = nblk


def _invert_kernel(dest_ref, init_ref, inv_ref):
    n_asg = dest_ref.shape[0]
    pltpu.sync_copy(init_ref, inv_ref)

    def put(a, c):
        inv_ref[dest_ref[a]] = a
        return c

    lax.fori_loop(0, n_asg, put, 0, unroll=16)


def _index_tiles_kernel(inv_ref, pk_ref, *, n_tok):
    nb = pk_ref.shape[0] // SUBLANES
    per_blk = R_BLK // LANES
    assert per_blk == 2
    pk_ref[...] = jnp.zeros(pk_ref.shape, I32)
    for half in range(per_blk):
        a = inv_ref[pl.ds(half, nb, stride=per_blk), :]
        t = jnp.bitwise_and(a, n_tok - 1)
        pk_ref[pl.ds(half, nb, stride=SUBLANES), :] = lax.shift_left(lax.shift_right_logical(t, 1), 3)
        pk_ref[pl.ds(2 + half, nb, stride=SUBLANES), :] = lax.shift_left(jnp.bitwise_and(a, 1), 4)
        pk_ref[pl.ds(4 + half, nb, stride=SUBLANES), :] = a * Y_TILES


def _plan(e_rows, n_tok, nb):
    R = R_BLK
    n_asg = 2 * n_tok
    rows = n_asg // LANES
    emat = e_rows[0:2].reshape(rows, LANES)
    li = jnp.arange(LANES)
    ux = (li[:, None] < li[None, :]).astype(BF16)
    ones = jnp.ones((LANES, LANES), BF16)
    ri = jnp.arange(rows)
    lx = (ri[:, None] > ri[None, :]).astype(BF16)
    dest, bstart, nblk = pl.pallas_call(
        _plan_kernel,
        out_shape=[jax.ShapeDtypeStruct((rows, LANES), I32),
                   jax.ShapeDtypeStruct((SUBLANES, LANES), I32),
                   jax.ShapeDtypeStruct((SUBLANES, LANES), I32)],
        name="plan",
    )(emat, ux, ones, lx)
    pos = jnp.arange(nb * R, dtype=I32)
    init = n_asg + (pos & (R - 1)) + jnp.where(pos >= (nb - 1) * R, R, 0)
    inv = pl.pallas_call(
        _invert_kernel,
        in_specs=[pl.BlockSpec(memory_space=pltpu.SMEM), pl.BlockSpec(memory_space=pl.ANY)],
        out_specs=pl.BlockSpec(memory_space=pltpu.SMEM),
        out_shape=jax.ShapeDtypeStruct((nb * R,), I32),
        name="invert",
    )(dest.reshape(n_asg), init)
    pk = pl.pallas_call(
        functools.partial(_index_tiles_kernel, n_tok=n_tok),
        out_shape=jax.ShapeDtypeStruct((nb * SUBLANES, LANES), I32),
        name="index_tiles",
    )(inv.reshape(nb * R // LANES, LANES))
    return bstart[0, 0:N_EXPERTS], nblk[0, 0:N_EXPERTS], pk.reshape(nb, SUBLANES, LANES)


def kernel(x, norm_mix, w_in, ssd_conv_w, ssd_conv_b, dt_bias, a_log, d_skip, ssd_norm, sc_conv_w,
           sc_norm, w_out, norm_ffn, w_router_group, w_router_expert, w_gate, w_up, w_down, final_norm):
    B, L, D = x.shape
    n_tok = B * L
    depth = norm_mix.shape[0]
    assert depth == 1 and D == D_MODEL and (n_tok & (n_tok - 1)) == 0
    nb = -(-((2 * n_tok) // R_BLK + N_EXPERTS + 1) // SUBLANES) * SUBLANES

    o1 = SSD_WIDTH
    o2 = o1 + XBC
    o3 = o2 + N_HEADS
    wi = w_in.reshape(D, -1)
    wit = wi.T
    wa = wit[0:o2].astype(BF16)
    wb = wit[o3:].astype(BF16)
    wdt = jnp.pad(wit[o2:o3], ((0, LANES - N_HEADS), (0, 0))).astype(BF16)
    pad_h = (0, LANES - N_HEADS)
    dtb = jnp.pad(dt_bias[0], pad_h).reshape(1, LANES)
    alog = jnp.pad(a_log[0], pad_h).reshape(1, LANES)
    dskip = jnp.repeat(d_skip[0], HEAD_DIM).reshape(1, SSD_WIDTH)

    wre = jnp.transpose(w_router_expert[0], (1, 0, 2)).reshape(D, N_EXPERTS)
    wrt = jnp.pad(jnp.concatenate([w_router_group[0], wre], axis=1).T,
                  ((0, ROUTER_ROWS - N_GROUPS - N_EXPERTS), (0, 0)))
    wrt_hi = wrt.astype(BF16)
    wr = jnp.concatenate([wrt_hi, (wrt - wrt_hi.astype(F32)).astype(BF16)], axis=0)

    ri = jnp.arange(T_MIX)
    tri = (ri[:, None] >= ri[None, :]).astype(BF16)
    er = jnp.arange(LANES)
    ec = jnp.arange(SSD_WIDTH)
    e3 = ((er[:, None] < 48) & ((er[:, None] % 16) == (ec[None, :] // HEAD_DIM))).astype(BF16)
    gsum = ((ec[:, None] // HEAD_DIM) == er[None, :]).astype(BF16)

    h, hn2, e_rows, gates = _mixer_call(
        x, norm_mix[0].reshape(1, D), wa, wb, wdt, ssd_conv_w[0], ssd_conv_b[0].reshape(1, XBC), dtb, alog,
        dskip, ssd_norm[0].reshape(1, SSD_WIDTH), sc_conv_w[0], sc_norm[0].reshape(1, SC_WIDTH),
        w_out.reshape(-1, D).astype(BF16), norm_ffn[0].reshape(1, D), wr, tri, e3, gsum)

    bstart, nblk, pk = _plan(e_rows, n_tok, nb)
    y2 = _moe_call(bstart, nblk, pk, hn2, w_gate.reshape(N_EXPERTS, D, D_FF),
                   w_up.reshape(N_EXPERTS, D, D_FF), w_down.reshape(N_EXPERTS, D_FF, D), n_tok)
    out = _combine_call(h.reshape(n_tok, D), y2, gates, final_norm.reshape(1, D))
    return out.reshape(B, L, D)
```

```python
import functools

import jax
import jax.numpy as jnp
from jax import lax
from jax.experimental import pallas as pl
from jax.experimental.pallas import tpu as pltpu

F32 = jnp.float32
BF16 = jnp.bfloat16
I32 = jnp.int32

EPS = 1e-6
D_MODEL = 1024
N_HEADS = 16
HEAD_DIM = 64
N_BC_GROUPS = 2
STATE = 128
SSD_WIDTH = 1024
XBC = SSD_WIDTH + 2 * N_BC_GROUPS * STATE
SC_WIDTH = 1024
SC_GROUPS = 16
N_GROUPS = 4
EPG = 8
N_EXPERTS = 32
D_FF = 512

LANES = 128
SUBLANES = 8
ROW_TILES = D_MODEL // LANES
Y_TILES = ROW_TILES // 2

T_MIX = 512
Q_SSD = 128
COL_CHUNK = 512
PAIR_ROWS = T_MIX // 2
R_BLK = 256
R_SHIFT = 8
ROUTER_ROWS = 48
N_SLOTS = 4
T_CMB = 1024

VMEM_LIMIT = 56 * 1024 * 1024


def _dot(a, b):
    return jnp.dot(a, b, preferred_element_type=F32)


def _dot_t(a, b):
    return lax.dot_general(a, b, (((1,), (1,)), ((), ())), preferred_element_type=F32)


def _split3(v):
    p1 = v.astype(BF16).astype(F32)
    r1 = v - p1
    p2 = r1.astype(BF16).astype(F32)
    p3 = (r1 - p2).astype(BF16).astype(F32)
    return p1, p2, p3


def _pack3(v):
    p1, p2, p3 = _split3(v)
    return (p1 + pltpu.roll(p2, 16, 1) + pltpu.roll(p3, 32, 1)).astype(BF16)


def _silu(v):
    return v * jax.nn.sigmoid(v)


def _mixer_kernel(x_ref, gmix_ref, w_ref, wdt_ref, convw_ref, convb_ref, dtb_ref, alog_ref,
                  dskip_ref, ssdn_ref, scw_ref, scn_ref, wout_ref, gffn_ref, wr_ref,
                  tri_ref, e3_ref, gsum_ref,
                  h_ref, hn2_ref, e_ref, g_ref,
                  cbuf, sbuf, st_ref, pbuf):
    T = T_MIX
    t = pl.program_id(1)

    @pl.when(t == 0)
    def _():
        cbuf[0:8, :] = jnp.zeros((8, XBC), F32)
        sbuf[0:8, :] = jnp.zeros((8, SC_WIDTH), F32)
        st_ref[...] = jnp.zeros(st_ref.shape, F32)

    @pl.when(t > 0)
    def _():
        cbuf[0:8, :] = cbuf[T:T + 8, :]
        sbuf[0:8, :] = sbuf[T:T + 8, :]

    x = x_ref[0]
    ms = jnp.mean(x * x, axis=-1, keepdims=True)
    hn = (x * lax.rsqrt(ms + EPS) * gmix_ref[...]).astype(BF16)

    wb0 = SSD_WIDTH + XBC + N_HEADS
    dt_raw = _dot_t(hn, wdt_ref[...])

    cw = convw_ref[...]
    xact_parts = []
    for c0 in range(0, XBC, COL_CHUNK):
        c1 = c0 + COL_CHUNK
        xbc = _dot_t(hn, w_ref[1024 + c0:1024 + c1, :])
        cbuf[8:8 + T, c0:c1] = xbc
        acc = convb_ref[:, c0:c1] + cw[3:4, c0:c1] * xbc
        for k in range(3):
            acc = acc + cw[k:k + 1, c0:c1] * cbuf[5 + k:5 + k + T, c0:c1]
        xact_parts.append(_silu(acc))
    xact = jnp.concatenate(xact_parts, axis=1)
    xs = xact[:, 0:SSD_WIDTH]

    lane = lax.broadcasted_iota(I32, (1, LANES), 1)
    hmask = lane < N_HEADS
    a = jnp.where(hmask, -jnp.exp(alog_ref[...]), 0.0)
    dtv = dt_raw + dtb_ref[...]
    dt = jnp.where(hmask, jnp.maximum(dtv, 0.0) + jnp.log1p(jnp.exp(-jnp.abs(dtv))), 0.0)
    adt = dt * a

    sw = scw_ref[...]
    sc_parts = []
    for c0 in range(0, SC_WIDTH, COL_CHUNK):
        c1 = c0 + COL_CHUNK
        u = (_dot_t(hn, w_ref[wb0 + 1024 + c0:wb0 + 1024 + c1, :])
             * _dot_t(hn, w_ref[wb0 + 2048 + c0:wb0 + 2048 + c1, :]))
        sbuf[8:8 + T, c0:c1] = u
        conv = (sw[2:3, c0:c1] * u + sw[1:2, c0:c1] * sbuf[7:7 + T, c0:c1]
                + sw[0:1, c0:c1] * sbuf[6:6 + T, c0:c1])
        sc_parts.append(_dot_t(hn, w_ref[wb0 + c0:wb0 + c1, :]) * conv)
    sc = jnp.concatenate(sc_parts, axis=1)
    z_lo = _dot_t(hn, w_ref[0:COL_CHUNK, :])
    gs = _dot((sc * sc).astype(BF16), gsum_ref[...])
    z_hi = _dot_t(hn, w_ref[COL_CHUNK:2 * COL_CHUNK, :])
    rstd = jnp.where(lax.broadcasted_iota(I32, (1, LANES), 1) < SC_GROUPS,
                     lax.rsqrt(gs * (1.0 / HEAD_DIM) + EPS), 0.0)
    rstd_e = _dot(_pack3(rstd), e3_ref[...])
    sc_out = (sc * rstd_e * scn_ref[...]).astype(BF16)
    z = jnp.concatenate([z_lo, z_hi], axis=1)

    c3 = _dot(tri_ref[...], _pack3(adt))
    ac = jnp.where(hmask, c3 + pltpu.roll(c3, LANES - 16, 1) + pltpu.roll(c3, LANES - 32, 1), 0.0)
    Q = Q_SSD
    n_sub = T // Q
    ends = [ac[(c + 1) * Q - 1:(c + 1) * Q, :] for c in range(n_sub)]
    base = jnp.concatenate([jnp.broadcast_to(ends[c - 1] if c else jnp.zeros_like(ends[0]), (Q, LANES))
                            for c in range(n_sub)], axis=0)
    endv = jnp.concatenate([jnp.broadcast_to(ends[c], (Q, LANES)) for c in range(n_sub)], axis=0)
    eac = jnp.where(hmask, jnp.exp(ac - base), 0.0)
    wdt = dt * jnp.exp(endv - ac)

    stacked = jnp.concatenate([_pack3(dt), _pack3(wdt), _pack3(eac)], axis=0)
    ex = _dot(stacked, e3_ref[...])
    dt_e = ex[0:T]
    wdt_e = ex[T:2 * T]
    eac_e = ex[2 * T:3 * T]
    xdt = (xs * dt_e).astype(BF16)
    xdtw = (xs * wdt_e).astype(BF16)

    mix_sc = []
    mix_todo = list(range(0, D_MODEL, COL_CHUNK))

    def emit_mix():
        if mix_todo:
            c0 = mix_todo.pop(0)
            mix_sc.append(_dot(sc_out, wout_ref[SSD_WIDTH:SSD_WIDTH + SC_WIDTH, c0:c0 + COL_CHUNK]))

    ac_rows = ac.T
    rr = lax.broadcasted_iota(I32, (Q, Q), 0)
    cc = lax.broadcasted_iota(I32, (Q, Q), 1)
    causal = rr >= cc
    cblk = lax.shift_right_logical(lax.broadcasted_iota(I32, (Q, 4 * HEAD_DIM), 1), 6)

    y_cols = []
    for g in range(N_BC_GROUPS):
        bg = xact[:, SSD_WIDTH + STATE * g:SSD_WIDTH + STATE * (g + 1)]
        cg = xact[:, SSD_WIDTH + 2 * STATE + STATE * g:SSD_WIDTH + 2 * STATE + STATE * (g + 1)]
        st = st_ref[g]
        y_rows = [[], []]
        for c in range(n_sub):
            r0, r1 = c * Q, (c + 1) * Q
            bb = bg[r0:r1].astype(BF16)
            cbf = cg[r0:r1].astype(BF16)
            cb = lax.dot_general(cbf, bb, (((1,), (1,)), ((), ())), preferred_element_type=F32)
            yoff = _dot(cbf, st.astype(BF16))
            dec = eac_e[r1 - 1:r1, 512 * g:512 * (g + 1)]
            bgt = bg[r0:r1].T.astype(BF16)
            st = st * dec + _dot(bgt, xdtw[r0:r1, 512 * g:512 * (g + 1)])
            for q in range(2):
                ms_list = []
                for r in range(4):
                    hh = 8 * g + 4 * q + r
                    seg = ac[r0:r1, hh:hh + 1] - ac_rows[hh:hh + 1, r0:r1]
                    lh = jnp.exp(jnp.where(causal, seg, -jnp.inf))
                    ms_list.append((cb * lh).astype(BF16))
                lhs = jnp.concatenate(ms_list, axis=1)
                if c == 1 and q == 0:
                    emit_mix()
                lo = 512 * g + 256 * q
                x4 = xdt[r0:r1, lo:lo + 256]
                rhs = jnp.concatenate(
                    [jnp.where(cblk == r, x4, jnp.zeros_like(x4)) for r in range(4)], axis=0)
                yd = _dot(lhs, rhs)
                y_rows[q].append(yd + eac_e[r0:r1, lo:lo + 256] * yoff[:, 256 * q:256 * (q + 1)])
        st_ref[g] = st
        y_cols += [jnp.concatenate(y_rows[0], axis=0), jnp.concatenate(y_rows[1], axis=0)]
    y = jnp.concatenate(y_cols, axis=1) + dskip_ref[...] * xs

    while mix_todo:
        emit_mix()
    v = y * _silu(z)
    outs = []
    for g in range(N_BC_GROUPS):
        vg = v[:, 512 * g:512 * (g + 1)]
        msg = jnp.mean(vg * vg, axis=-1, keepdims=True)
        outs.append(vg * lax.rsqrt(msg + EPS))
    ssd_out = (jnp.concatenate(outs, axis=1) * ssdn_ref[...]).astype(BF16)

    mix = _dot(ssd_out, wout_ref[0:SSD_WIDTH, :]) + jnp.concatenate(mix_sc, axis=1)
    h = x + mix
    h_ref[0] = h

    ms2 = jnp.mean(h * h, axis=-1, keepdims=True)
    hn2 = h * lax.rsqrt(ms2 + EPS) * gffn_ref[...]
    hi = hn2.astype(BF16)
    bits = pltpu.bitcast(hi.astype(F32), jnp.uint32)
    for s in range(ROW_TILES):
        pbuf[pl.ds(s * T, T), :] = bits[:, LANES * s:LANES * (s + 1)]
    for s in range(ROW_TILES):
        even = pbuf[pl.ds(s * T, PAIR_ROWS, stride=2), :]
        odd = pbuf[pl.ds(s * T + 1, PAIR_ROWS, stride=2), :]
        hn2_ref[pl.ds(s, PAIR_ROWS, stride=ROW_TILES), :] = jnp.bitwise_or(
            jnp.right_shift(even, jnp.uint32(16)), odd)
    lo_ = (hn2 - hi.astype(F32)).astype(BF16)
    RT = ROUTER_ROWS
    nt_dims = (((1,), (1,)), ((), ()))
    both = lax.dot_general(wr_ref[...], hi, nt_dims, preferred_element_type=F32)
    low = lax.dot_general(wr_ref[0:RT, :], lo_, nt_dims, preferred_element_type=F32)
    logits = both[0:RT] + (low + both[RT:2 * RT])

    ri = lax.broadcasted_iota(I32, (RT, T), 0)
    ri_f = ri.astype(F32)
    neg = -jnp.inf
    big = 1e9
    gl = jnp.where(ri < N_GROUPS, logits, neg)
    gmax = jnp.max(gl, axis=0, keepdims=True)
    gidx = jnp.min(jnp.where(gl == gmax, ri_f, big), axis=0, keepdims=True)
    gsum = jnp.sum(jnp.where(ri < N_GROUPS, jnp.exp(logits - gmax), 0.0), axis=0, keepdims=True)
    gw = 1.0 / gsum
    egrp = lax.shift_right_logical(ri - N_GROUPS, 3).astype(F32)
    in_grp = (ri >= N_GROUPS) & (ri < N_GROUPS + N_EXPERTS) & (egrp == gidx)
    el = jnp.where(in_grp, logits, neg)
    v1 = jnp.max(el, axis=0, keepdims=True)
    i1 = jnp.min(jnp.where(el == v1, ri_f, big), axis=0, keepdims=True)
    el2 = jnp.where(ri_f == i1, neg, el)
    v2 = jnp.max(el2, axis=0, keepdims=True)
    i2 = jnp.min(jnp.where(el2 == v2, ri_f, big), axis=0, keepdims=True)
    p = jnp.exp(v2 - v1)
    s1 = 1.0 / (1.0 + p)
    gate1 = gw * s1
    gate2 = gw * (p * s1)
    r8 = lax.broadcasted_iota(I32, (SUBLANES, T), 0)
    e_ref[...] = jnp.where(r8 == 0, i1 - N_GROUPS, jnp.where(r8 == 1, i2 - N_GROUPS, 0.0)).astype(I32)
    rl = lax.broadcasted_iota(I32, (LANES, T), 0)
    g_ref[...] = jnp.where(rl == 0, gate1, jnp.where(rl == 1, gate2, 0.0)).T


def _mixer_call(x, gmix, w, wdt, convw, convb, dtb, alog, dskip, ssdn, scw, scn, wout, gffn,
                wr, tri, e3, gsum):
    B, L, D = x.shape
    T = T_MIX
    nt = L // T
    n_tok = B * L

    def const(shape):
        return pl.BlockSpec(shape, lambda b, t: (0,) * len(shape), pipeline_mode=pl.Buffered(1))

    in_specs = [
        pl.BlockSpec((1, T, D), lambda b, t: (b, t, 0)),
        const((1, D)),
        const(w.shape), const(wdt.shape), const(convw.shape), const(convb.shape),
        const(dtb.shape), const(alog.shape), const(dskip.shape), const(ssdn.shape),
        const(scw.shape), const(scn.shape), const(wout.shape), const(gffn.shape),
        const(wr.shape), const(tri.shape), const(e3.shape), const(gsum.shape),
    ]
    out_shape = [
        jax.ShapeDtypeStruct((B, L, D), F32),
        jax.ShapeDtypeStruct((n_tok // 2 * ROW_TILES, LANES), jnp.uint32),
        jax.ShapeDtypeStruct((8, n_tok), I32),
        jax.ShapeDtypeStruct((n_tok, LANES), F32),
    ]
    out_specs = [
        pl.BlockSpec((1, T, D), lambda b, t: (b, t, 0)),
        pl.BlockSpec((PAIR_ROWS * ROW_TILES, LANES), lambda b, t: (b * nt + t, 0)),
        pl.BlockSpec((8, T), lambda b, t: (0, b * nt + t)),
        pl.BlockSpec((T, LANES), lambda b, t: (b * nt + t, 0)),
    ]
    return pl.pallas_call(
        _mixer_kernel,
        grid=(B, nt),
        in_specs=in_specs,
        out_specs=out_specs,
        out_shape=out_shape,
        scratch_shapes=[
            pltpu.VMEM((T + 8, XBC), F32),
            pltpu.VMEM((T + 8, SC_WIDTH), F32),
            pltpu.VMEM((N_BC_GROUPS, STATE, 512), F32),
            pltpu.VMEM((ROW_TILES * T, LANES), jnp.uint32),
        ],
        compiler_params=pltpu.CompilerParams(
            dimension_semantics=("arbitrary", "arbitrary"),
            vmem_limit_bytes=VMEM_LIMIT),
        name="mixer",
    )(x, gmix, w, wdt, convw, convb, dtb, alog, dskip, ssdn, scw, scn, wout, gffn, wr,
      tri, e3, gsum)


def _moe_kernel(bstart_ref, nblk_ref, pk_ref, hn2p_ref, wg_ref, wu_ref, wd_ref,
                y2_ref, hn2v, xbuf, ybuf, wgu, wdb, idx_ref, lsem, ssem, isem, *, n_tok):
    R = R_BLK
    slab = R * ROW_TILES
    e = pl.program_id(0)
    g0 = bstart_ref[e]
    nb_e = nblk_ref[e]

    def idx_word(sl, field, j):
        return idx_ref[sl, 2 * field + j // LANES, j % LANES]

    def idx_copy(g, sl):
        return pltpu.make_async_copy(pk_ref.at[g], idx_ref.at[sl], isem.at[sl])

    def gather_rows(sl, xs):
        for j in range(R):
            row = pl.multiple_of(idx_word(sl, 0, j), ROW_TILES)
            sh = idx_word(sl, 1, j).astype(jnp.uint32)
            w = hn2v[pl.ds(row, ROW_TILES), :]
            w = jnp.left_shift(jnp.right_shift(w, sh), jnp.uint32(16))
            xbuf[pl.ds(xs * slab + j * ROW_TILES, ROW_TILES), :] = pltpu.bitcast(w, F32)

    yslab = R * Y_TILES
    FH = D_FF // 2

    def y_copy(sl, j, dst_row):
        return pltpu.make_async_copy(
            ybuf.at[pl.ds(sl * yslab + j * Y_TILES, Y_TILES), :],
            y2_ref.at[pl.ds(pl.multiple_of(dst_row, Y_TILES), Y_TILES), :],
            ssem.at[sl])

    def scatter_start(sl, lo, hi):
        for j in range(lo, hi):
            y_copy(sl, j, idx_word(sl, 2, j)).start(priority=j % 2)

    def slab_wait(sl):
        view = ybuf.at[pl.ds(sl * yslab, yslab), :]
        pltpu.make_async_copy(view, view, ssem.at[sl]).wait()

    n_idx_blocks = pk_ref.shape[0]

    @pl.when(e == 0)
    def _():
        load = pltpu.make_async_copy(hn2p_ref, hn2v, lsem.at[0])
        load.start()
        first = idx_copy(0, 0)
        first.start()
        idx_copy(1, 1).start()
        prev = idx_copy(n_idx_blocks - 1, N_SLOTS - 1)
        prev.start()
        ybuf[...] = jnp.zeros(ybuf.shape, ybuf.dtype)
        for sl in range(N_SLOTS - 1):
            base = 2 * n_tok + (sl + 1 if sl else 0) * R
            for j in range(R):
                y_copy(sl, j, (base + j) * Y_TILES).start(priority=j % 2)
        prev.wait()
        first.wait()
        load.wait()
        gather_rows(0, 0)

    @pl.when(nb_e > 0)
    def _():
        for j in range(2):
            wgu[:, 2 * j * FH:(2 * j + 1) * FH] = wg_ref[0, :, j * FH:(j + 1) * FH].astype(BF16)
            wgu[:, (2 * j + 1) * FH:(2 * j + 2) * FH] = wu_ref[0, :, j * FH:(j + 1) * FH].astype(BF16)
        wdb[...] = wd_ref[0].astype(BF16)

    c1, c2 = R // 3, 2 * (R // 3)

    def run_block(g, sl):
        prv = (sl + N_SLOTS - 1) % N_SLOTS
        nx1 = (sl + 1) % N_SLOTS
        nx2 = (sl + 2) % N_SLOTS
        idx_copy(g + 1, nx1).wait()
        idx_copy(g + 2, nx2).start()
        slab_wait(sl)
        x = jnp.concatenate(
            [xbuf[pl.ds((sl % 2) * slab + s, R, stride=ROW_TILES), :] for s in range(ROW_TILES)],
            axis=1).astype(BF16)
        scatter_start(prv, 0, c1)
        gu = _dot(x, wgu[...])
        gather_rows(nx1, (sl + 1) % 2)
        scatter_start(prv, c1, c2)
        act = jnp.concatenate(
            [_silu(gu[:, 2 * j * FH:(2 * j + 1) * FH]) * gu[:, (2 * j + 1) * FH:(2 * j + 2) * FH]
             for j in range(2)], axis=1).astype(BF16)
        scatter_start(prv, c2, R)
        yy = _dot(act, wdb[...])
        for s in range(Y_TILES):
            lo_w = pltpu.bitcast(yy[:, 2 * LANES * s:2 * LANES * s + LANES].astype(BF16).astype(F32), jnp.uint32)
            hi_w = pltpu.bitcast(yy[:, 2 * LANES * s + LANES:2 * LANES * (s + 1)].astype(BF16).astype(F32),
                                 jnp.uint32)
            ybuf[pl.ds(sl * yslab + s, R, stride=Y_TILES), :] = jnp.bitwise_or(
                jnp.right_shift(lo_w, jnp.uint32(16)), hi_w)

    def block(i, carry):
        g = g0 + i
        sl = lax.rem(g, N_SLOTS)
        for k in range(N_SLOTS):
            @pl.when(sl == k)
            def _(k=k):
                run_block(g, k)
        return carry

    lax.fori_loop(0, nb_e, block, 0)

    @pl.when(e == pl.num_programs(0) - 1)
    def _():
        g_end = g0 + nb_e
        sl_end = lax.rem(g_end, N_SLOTS)
        for k in range(N_SLOTS):
            @pl.when(sl_end == k)
            def _(k=k):
                scatter_start((k + N_SLOTS - 1) % N_SLOTS, 0, R)
                idx_copy(g_end + 1, (k + 1) % N_SLOTS).wait()
        for k in range(N_SLOTS):
            slab_wait(k)


def _moe_call(bstart, nblk, pk, hn2p, wg, wu, wd, n_tok):
    R = R_BLK
    slab = R * ROW_TILES
    grid_spec = pltpu.PrefetchScalarGridSpec(
        num_scalar_prefetch=2,
        grid=(N_EXPERTS,),
        in_specs=[
            pl.BlockSpec(memory_space=pl.ANY),
            pl.BlockSpec(memory_space=pl.ANY),
            pl.BlockSpec((1, D_MODEL, D_FF), lambda e, bs, nb: (e, 0, 0)),
            pl.BlockSpec((1, D_MODEL, D_FF), lambda e, bs, nb: (e, 0, 0)),
            pl.BlockSpec((1, D_FF, D_MODEL), lambda e, bs, nb: (e, 0, 0)),
        ],
        out_specs=pl.BlockSpec(memory_space=pl.ANY),
        scratch_shapes=[
            pltpu.VMEM(hn2p.shape, jnp.uint32),
            pltpu.VMEM((2 * slab, LANES), F32),
            pltpu.VMEM((N_SLOTS * R * Y_TILES, LANES), jnp.uint32),
            pltpu.VMEM((D_MODEL, 2 * D_FF), BF16),
            pltpu.VMEM((D_FF, D_MODEL), BF16),
            pltpu.SMEM((N_SLOTS, SUBLANES, LANES), I32),
            pltpu.SemaphoreType.DMA((1,)),
            pltpu.SemaphoreType.DMA((N_SLOTS,)),
            pltpu.SemaphoreType.DMA((N_SLOTS,)),
        ],
    )
    return pl.pallas_call(
        functools.partial(_moe_kernel, n_tok=n_tok),
        grid_spec=grid_spec,
        out_shape=jax.ShapeDtypeStruct(((2 * n_tok + N_SLOTS * R) * Y_TILES, LANES), jnp.uint32),
        compiler_params=pltpu.CompilerParams(
            dimension_semantics=("arbitrary",),
            vmem_limit_bytes=VMEM_LIMIT),
        name="moe",
    )(bstart, nblk, pk, hn2p, wg, wu, wd)


def _unpack_rows(y_ref, rows):
    parts = []
    for s in range(Y_TILES):
        w = y_ref[pl.ds(s, rows, stride=Y_TILES), :]
        parts.append(pltpu.bitcast(jnp.left_shift(w, jnp.uint32(16)), F32))
        parts.append(pltpu.bitcast(jnp.bitwise_and(w, jnp.uint32(0xFFFF0000)), F32))
    return jnp.concatenate(parts, axis=1)


def _combine_kernel(h_ref, y0_ref, y1_ref, g_ref, fn_ref, o_ref):
    T = T_CMB
    h = h_ref[...]
    y0 = _unpack_rows(y0_ref, T)
    y1 = _unpack_rows(y1_ref, T)
    g = g_ref[...]
    v = h + (y0 * g[:, 0:1] + y1 * g[:, 1:2])
    ms = jnp.mean(v * v, axis=-1, keepdims=True)
    o_ref[...] = v * lax.rsqrt(ms + EPS) * fn_ref[...]


def _combine_call(h2d, y2, gates, fnorm):
    n_tok, D = h2d.shape
    T = T_CMB
    nt = n_tok // T
    return pl.pallas_call(
        _combine_kernel,
        grid=(nt,),
        in_specs=[
            pl.BlockSpec((T, D), lambda i: (i, 0)),
            pl.BlockSpec((T * Y_TILES, LANES), lambda i: (i, 0)),
            pl.BlockSpec((T * Y_TILES, LANES), lambda i: (nt + i, 0)),
            pl.BlockSpec((T, LANES), lambda i: (i, 0)),
            pl.BlockSpec((1, D), lambda i: (0, 0)),
        ],
        out_specs=pl.BlockSpec((T, D), lambda i: (i, 0)),
        out_shape=jax.ShapeDtypeStruct((n_tok, D), F32),
        compiler_params=pltpu.CompilerParams(dimension_semantics=("arbitrary",)),
        name="combine",
    )(h2d, y2, y2, gates, fnorm)


def _plan_kernel(e_ref, ux_ref, ones_ref, lx_ref, dest_ref, bstart_ref, nblk_ref):
    rows = e_ref.shape[0]
    ev = e_ref[...]
    lane8 = lax.broadcasted_iota(I32, (SUBLANES, LANES), 1)
    dest = jnp.zeros((rows, LANES), F32)
    bstart = jnp.zeros((SUBLANES, LANES), I32)
    nblk = jnp.zeros((SUBLANES, LANES), I32)
    pstart = jnp.zeros((1, LANES), F32)
    for e in range(N_EXPERTS):
        ohb = ev == e
        oh = jnp.where(ohb, 1.0, 0.0).astype(BF16)
        within = _dot(oh, ux_ref[...])
        rtot = _dot(oh, ones_ref[...])
        rpre = _dot(lx_ref[...], rtot.astype(BF16))
        cnt = rpre[rows - 1:rows, :] + rtot[rows - 1:rows, :]
        dest = dest + jnp.where(ohb, within + rpre + pstart, 0.0)
        cnt_i = cnt.astype(I32)
        nb_e = lax.shift_right_logical(cnt_i + (R_BLK - 1), R_SHIFT)
        bstart = jnp.where(lane8 == e, lax.shift_right_logical(pstart.astype(I32), R_SHIFT), bstart)
        nblk = jnp.where(lane8 == e, nb_e, nblk)
        pstart = pstart + lax.shift_left(nb_e, R_SHIFT).astype(F32)
    dest_ref[...] = dest.astype(I32)
    bstart_ref[...] = bstart
    nblk_ref[...] = nblk


def _invert_kernel(dest_ref, init_ref, inv_ref):
    n_asg = dest_ref.shape[0]
    pltpu.sync_copy(init_ref, inv_ref)

    def put(a, c):
        inv_ref[dest_ref[a]] = a
        return c

    lax.fori_loop(0, n_asg, put, 0, unroll=16)


def _index_tiles_kernel(inv_ref, pk_ref, *, n_tok):
    nb = pk_ref.shape[0] // SUBLANES
    per_blk = R_BLK // LANES
    assert per_blk == 2
    pk_ref[...] = jnp.zeros(pk_ref.shape, I32)
    for half in range(per_blk):
        a = inv_ref[pl.ds(half, nb, stride=per_blk), :]
        t = jnp.bitwise_and(a, n_tok - 1)
        pk_ref[pl.ds(half, nb, stride=SUBLANES), :] = lax.shift_left(lax.shift_right_logical(t, 1), 3)
        pk_ref[pl.ds(2 + half, nb, stride=SUBLANES), :] = lax.shift_left(jnp.bitwise_and(a, 1), 4)
        pk_ref[pl.ds(4 + half, nb, stride=SUBLANES), :] = a * Y_TILES


def _plan(e_rows, n_tok, nb):
    R = R_BLK
    n_asg = 2 * n_tok
    rows = n_asg // LANES
    emat = e_rows[0:2].reshape(rows, LANES)
    li = jnp.arange(LANES)
    ux = (li[:, None] < li[None, :]).astype(BF16)
    ones = jnp.ones((LANES, LANES), BF16)
    ri = jnp.arange(rows)
    lx = (ri[:, None] > ri[None, :]).astype(BF16)
    dest, bstart, nblk = pl.pallas_call(
        _plan_kernel,
        out_shape=[jax.ShapeDtypeStruct((rows, LANES), I32),
                   jax.ShapeDtypeStruct((SUBLANES, LANES), I32),
                   jax.ShapeDtypeStruct((SUBLANES, LANES), I32)],
        name="plan",
    )(emat, ux, ones, lx)
    pos = jnp.arange(nb * R, dtype=I32)
    init = n_asg + (pos & (R - 1)) + jnp.where(pos >= (nb - 1) * R, R, 0)
    inv = pl.pallas_call(
        _invert_kernel,
        in_specs=[pl.BlockSpec(memory_space=pltpu.SMEM), pl.BlockSpec(memory_space=pl.ANY)],
        out_specs=pl.BlockSpec(memory_space=pltpu.SMEM),
        out_shape=jax.ShapeDtypeStruct((nb * R,), I32),
        name="invert",
    )(dest.reshape(n_asg), init)
    pk = pl.pallas_call(
        functools.partial(_index_tiles_kernel, n_tok=n_tok),
        out_shape=jax.ShapeDtypeStruct((nb * SUBLANES, LANES), I32),
        name="index_tiles",
    )(inv.reshape(nb * R // LANES, LANES))
    return bstart[0, 0:N_EXPERTS], nblk[0, 0:N_EXPERTS], pk.reshape(nb, SUBLANES, LANES)


def kernel(x, norm_mix, w_in, ssd_conv_w, ssd_conv_b, dt_bias, a_log, d_skip, ssd_norm, sc_conv_w,
           sc_norm, w_out, norm_ffn, w_router_group, w_router_expert, w_gate, w_up, w_down, final_norm):
    B, L, D = x.shape
    n_tok = B * L
    depth = norm_mix.shape[0]
    assert depth == 1 and D == D_MODEL and (n_tok & (n_tok - 1)) == 0
    nb = -(-((2 * n_tok) // R_BLK + N_EXPERTS + 1) // SUBLANES) * SUBLANES

    o1 = SSD_WIDTH
    o2 = o1 + XBC
    o3 = o2 + N_HEADS
    wi = w_in.reshape(D, -1)
    wit = wi.T
    w_all = wit.astype(BF16)
    wdt = jnp.pad(wit[o2:o3], ((0, LANES - N_HEADS), (0, 0))).astype(BF16)
    pad_h = (0, LANES - N_HEADS)
    dtb = jnp.pad(dt_bias[0], pad_h).reshape(1, LANES)
    alog = jnp.pad(a_log[0], pad_h).reshape(1, LANES)
    dskip = jnp.repeat(d_skip[0], HEAD_DIM).reshape(1, SSD_WIDTH)

    wre = jnp.transpose(w_router_expert[0], (1, 0, 2)).reshape(D, N_EXPERTS)
    wrt = jnp.pad(jnp.concatenate([w_router_group[0], wre], axis=1).T,
                  ((0, ROUTER_ROWS - N_GROUPS - N_EXPERTS), (0, 0)))
    wrt_hi = wrt.astype(BF16)
    wr = jnp.concatenate([wrt_hi, (wrt - wrt_hi.astype(F32)).astype(BF16)], axis=0)

    ri = jnp.arange(T_MIX)
    tri = (ri[:, None] >= ri[None, :]).astype(BF16)
    er = jnp.arange(LANES)
    ec = jnp.arange(SSD_WIDTH)
    e3 = ((er[:, None] < 48) & ((er[:, None] % 16) == (ec[None, :] // HEAD_DIM))).astype(BF16)
    gsum = ((ec[:, None] // HEAD_DIM) == er[None, :]).astype(BF16)

    h, hn2, e_rows, gates = _mixer_call(
        x, norm_mix[0].reshape(1, D), w_all, wdt, ssd_conv_w[0], ssd_conv_b[0].reshape(1, XBC), dtb, alog,
        dskip, ssd_norm[0].reshape(1, SSD_WIDTH), sc_conv_w[0], sc_norm[0].reshape(1, SC_WIDTH),
        w_out.reshape(-1, D).astype(BF16), norm_ffn[0].reshape(1, D), wr, tri, e3, gsum)

    bstart, nblk, pk = _plan(e_rows, n_tok, nb)
    y2 = _moe_call(bstart, nblk, pk, hn2, w_gate.reshape(N_EXPERTS, D, D_FF),
                   w_up.reshape(N_EXPERTS, D, D_FF), w_down.reshape(N_EXPERTS, D_FF, D), n_tok)
    out = _combine_call(h.reshape(n_tok, D), y2, gates, final_norm.reshape(1, D))
    return out.reshape(B, L, D)
```

```python
import functools

import jax
import jax.numpy as jnp
from jax import lax
from jax.experimental import pallas as pl
from jax.experimental.pallas import tpu as pltpu

F32 = jnp.float32
BF16 = jnp.bfloat16
I32 = jnp.int32

EPS = 1e-6
D_MODEL = 1024
N_HEADS = 16
HEAD_DIM = 64
N_BC_GROUPS = 2
STATE = 128
SSD_WIDTH = 1024
XBC = SSD_WIDTH + 2 * N_BC_GROUPS * STATE
SC_WIDTH = 1024
SC_GROUPS = 16
N_GROUPS = 4
EPG = 8
N_EXPERTS = 32
D_FF = 512

LANES = 128
SUBLANES = 8
ROW_TILES = D_MODEL // LANES
Y_TILES = ROW_TILES // 2

T_MIX = 512
Q_SSD = 128
COL_CHUNK = 512
INV_CHUNK = 4096
PAIR_ROWS = T_MIX // 2
R_BLK = 256
R_SHIFT = 8
ROUTER_ROWS = 48
N_SLOTS = 4
T_CMB = 1024

VMEM_LIMIT = 56 * 1024 * 1024


def _dot(a, b):
    return jnp.dot(a, b, preferred_element_type=F32)


def _dot_t(a, b):
    return lax.dot_general(a, b, (((1,), (1,)), ((), ())), preferred_element_type=F32)


def _split3(v):
    p1 = v.astype(BF16).astype(F32)
    r1 = v - p1
    p2 = r1.astype(BF16).astype(F32)
    p3 = (r1 - p2).astype(BF16).astype(F32)
    return p1, p2, p3


def _pack3(v):
    p1, p2, p3 = _split3(v)
    return (p1 + pltpu.roll(p2, 16, 1) + pltpu.roll(p3, 32, 1)).astype(BF16)


def _silu(v):
    return v * jax.nn.sigmoid(v)


def _mixer_kernel(x_ref, gmix_ref, wa_ref, wb_ref, wdt_ref, convw_ref, convb_ref, dtb_ref, alog_ref,
                  dskip_ref, ssdn_ref, scw_ref, scn_ref, wout_ref, gffn_ref, wr_ref,
                  tri_ref, e3_ref, gsum_ref,
                  h_ref, hn2_ref, e_ref, g_ref,
                  cbuf, sbuf, st_ref, pbuf):
    T = T_MIX
    t = pl.program_id(1)

    @pl.when(t == 0)
    def _():
        cbuf[0:8, :] = jnp.zeros((8, XBC), F32)
        sbuf[0:8, :] = jnp.zeros((8, SC_WIDTH), F32)
        st_ref[...] = jnp.zeros(st_ref.shape, F32)

    @pl.when(t > 0)
    def _():
        cbuf[0:8, :] = cbuf[T:T + 8, :]
        sbuf[0:8, :] = sbuf[T:T + 8, :]

    x = x_ref[0]
    ms = jnp.mean(x * x, axis=-1, keepdims=True)
    hn = (x * lax.rsqrt(ms + EPS) * gmix_ref[...]).astype(BF16)

    dt_raw = _dot_t(hn, wdt_ref[...])

    cw = convw_ref[...]
    xact_parts = []
    for c0 in range(0, XBC, COL_CHUNK):
        c1 = c0 + COL_CHUNK
        xbc = _dot_t(hn, wa_ref[1024 + c0:1024 + c1, :])
        cbuf[8:8 + T, c0:c1] = xbc
        acc = convb_ref[:, c0:c1] + cw[3:4, c0:c1] * xbc
        for k in range(3):
            acc = acc + cw[k:k + 1, c0:c1] * cbuf[5 + k:5 + k + T, c0:c1]
        xact_parts.append(_silu(acc))
    xact = jnp.concatenate(xact_parts, axis=1)
    xs = xact[:, 0:SSD_WIDTH]

    lane = lax.broadcasted_iota(I32, (1, LANES), 1)
    hmask = lane < N_HEADS
    a = jnp.where(hmask, -jnp.exp(alog_ref[...]), 0.0)
    dtv = dt_raw + dtb_ref[...]
    dt = jnp.where(hmask, jnp.maximum(dtv, 0.0) + jnp.log1p(jnp.exp(-jnp.abs(dtv))), 0.0)
    adt = dt * a

    sw = scw_ref[...]
    sc_parts = []
    for c0 in range(0, SC_WIDTH, COL_CHUNK):
        c1 = c0 + COL_CHUNK
        u = _dot_t(hn, wb_ref[1024 + c0:1024 + c1, :]) * _dot_t(hn, wb_ref[2048 + c0:2048 + c1, :])
        sbuf[8:8 + T, c0:c1] = u
        conv = (sw[2:3, c0:c1] * u + sw[1:2, c0:c1] * sbuf[7:7 + T, c0:c1]
                + sw[0:1, c0:c1] * sbuf[6:6 + T, c0:c1])
        sc_parts.append(_dot_t(hn, wb_ref[c0:c1, :]) * conv)
    sc = jnp.concatenate(sc_parts, axis=1)
    z_lo = _dot_t(hn, wa_ref[0:COL_CHUNK, :])
    gs = _dot((sc * sc).astype(BF16), gsum_ref[...])
    z_hi = _dot_t(hn, wa_ref[COL_CHUNK:2 * COL_CHUNK, :])
    rstd = jnp.where(lax.broadcasted_iota(I32, (1, LANES), 1) < SC_GROUPS,
                     lax.rsqrt(gs * (1.0 / HEAD_DIM) + EPS), 0.0)
    rstd_e = _dot(_pack3(rstd), e3_ref[...])
    sc_out = (sc * rstd_e * scn_ref[...]).astype(BF16)
    z = jnp.concatenate([z_lo, z_hi], axis=1)

    c3 = _dot(tri_ref[...], _pack3(adt))
    ac = jnp.where(hmask, c3 + pltpu.roll(c3, LANES - 16, 1) + pltpu.roll(c3, LANES - 32, 1), 0.0)
    Q = Q_SSD
    n_sub = T // Q
    ends = [ac[(c + 1) * Q - 1:(c + 1) * Q, :] for c in range(n_sub)]
    base = jnp.concatenate([jnp.broadcast_to(ends[c - 1] if c else jnp.zeros_like(ends[0]), (Q, LANES))
                            for c in range(n_sub)], axis=0)
    endv = jnp.concatenate([jnp.broadcast_to(ends[c], (Q, LANES)) for c in range(n_sub)], axis=0)
    eac = jnp.where(hmask, jnp.exp(ac - base), 0.0)
    wdt = dt * jnp.exp(endv - ac)

    stacked = jnp.concatenate([_pack3(dt), _pack3(wdt), _pack3(eac)], axis=0)
    ex = _dot(stacked, e3_ref[...])
    dt_e = ex[0:T]
    wdt_e = ex[T:2 * T]
    eac_e = ex[2 * T:3 * T]
    xdt = (xs * dt_e).astype(BF16)
    xdtw = (xs * wdt_e).astype(BF16)

    mix_sc = []
    mix_todo = list(range(0, D_MODEL, COL_CHUNK))

    def emit_mix():
        if mix_todo:
            c0 = mix_todo.pop(0)
            mix_sc.append(_dot(sc_out, wout_ref[SSD_WIDTH:SSD_WIDTH + SC_WIDTH, c0:c0 + COL_CHUNK]))

    ac_rows = ac.T
    rr = lax.broadcasted_iota(I32, (Q, Q), 0)
    cc = lax.broadcasted_iota(I32, (Q, Q), 1)
    causal = rr >= cc
    cblk = lax.shift_right_logical(lax.broadcasted_iota(I32, (Q, 4 * HEAD_DIM), 1), 6)

    y_cols = []
    for g in range(N_BC_GROUPS):
        bg = xact[:, SSD_WIDTH + STATE * g:SSD_WIDTH + STATE * (g + 1)]
        cg = xact[:, SSD_WIDTH + 2 * STATE + STATE * g:SSD_WIDTH + 2 * STATE + STATE * (g + 1)]
        st = st_ref[g]
        y_rows = [[], []]
        for c in range(n_sub):
            r0, r1 = c * Q, (c + 1) * Q
            bb = bg[r0:r1].astype(BF16)
            cbf = cg[r0:r1].astype(BF16)
            cb = lax.dot_general(cbf, bb, (((1,), (1,)), ((), ())), preferred_element_type=F32)
            yoff = _dot(cbf, st.astype(BF16))
            dec = eac_e[r1 - 1:r1, 512 * g:512 * (g + 1)]
            bgt = bg[r0:r1].T.astype(BF16)
            st = st * dec + _dot(bgt, xdtw[r0:r1, 512 * g:512 * (g + 1)])
            for q in range(2):
                ms_list = []
                for r in range(4):
                    hh = 8 * g + 4 * q + r
                    seg = ac[r0:r1, hh:hh + 1] - ac_rows[hh:hh + 1, r0:r1]
                    lh = jnp.exp(jnp.where(causal, seg, -jnp.inf))
                    ms_list.append((cb * lh).astype(BF16))
                lhs = jnp.concatenate(ms_list, axis=1)
                if c == 1 and q == 0:
                    emit_mix()
                lo = 512 * g + 256 * q
                x4 = xdt[r0:r1, lo:lo + 256]
                rhs = jnp.concatenate(
                    [jnp.where(cblk == r, x4, jnp.zeros_like(x4)) for r in range(4)], axis=0)
                yd = _dot(lhs, rhs)
                y_rows[q].append(yd + eac_e[r0:r1, lo:lo + 256] * yoff[:, 256 * q:256 * (q + 1)])
        st_ref[g] = st
        y_cols += [jnp.concatenate(y_rows[0], axis=0), jnp.concatenate(y_rows[1], axis=0)]
    y = jnp.concatenate(y_cols, axis=1) + dskip_ref[...] * xs

    while mix_todo:
        emit_mix()
    v = y * _silu(z)
    outs = []
    for g in range(N_BC_GROUPS):
        vg = v[:, 512 * g:512 * (g + 1)]
        msg = jnp.mean(vg * vg, axis=-1, keepdims=True)
        outs.append(vg * lax.rsqrt(msg + EPS))
    ssd_out = (jnp.concatenate(outs, axis=1) * ssdn_ref[...]).astype(BF16)

    mix = _dot(ssd_out, wout_ref[0:SSD_WIDTH, :]) + jnp.concatenate(mix_sc, axis=1)
    h = x + mix
    h_ref[0] = h

    ms2 = jnp.mean(h * h, axis=-1, keepdims=True)
    hn2 = h * lax.rsqrt(ms2 + EPS) * gffn_ref[...]
    hi = hn2.astype(BF16)
    bits = pltpu.bitcast(hi.astype(F32), jnp.uint32)
    for s in range(ROW_TILES):
        pbuf[pl.ds(s * T, T), :] = bits[:, LANES * s:LANES * (s + 1)]
    for s in range(ROW_TILES):
        even = pbuf[pl.ds(s * T, PAIR_ROWS, stride=2), :]
        odd = pbuf[pl.ds(s * T + 1, PAIR_ROWS, stride=2), :]
        hn2_ref[pl.ds(s, PAIR_ROWS, stride=ROW_TILES), :] = jnp.bitwise_or(
            jnp.right_shift(even, jnp.uint32(16)), odd)
    lo_ = (hn2 - hi.astype(F32)).astype(BF16)
    RT = ROUTER_ROWS
    nt_dims = (((1,), (1,)), ((), ()))
    both = lax.dot_general(wr_ref[...], hi, nt_dims, preferred_element_type=F32)
    low = lax.dot_general(wr_ref[0:RT, :], lo_, nt_dims, preferred_element_type=F32)
    logits = both[0:RT] + (low + both[RT:2 * RT])

    ri = lax.broadcasted_iota(I32, (RT, T), 0)
    ri_f = ri.astype(F32)
    neg = -jnp.inf
    big = 1e9
    gl = jnp.where(ri < N_GROUPS, logits, neg)
    gmax = jnp.max(gl, axis=0, keepdims=True)
    gidx = jnp.min(jnp.where(gl == gmax, ri_f, big), axis=0, keepdims=True)
    gsum = jnp.sum(jnp.where(ri < N_GROUPS, jnp.exp(logits - gmax), 0.0), axis=0, keepdims=True)
    gw = 1.0 / gsum
    egrp = lax.shift_right_logical(ri - N_GROUPS, 3).astype(F32)
    in_grp = (ri >= N_GROUPS) & (ri < N_GROUPS + N_EXPERTS) & (egrp == gidx)
    el = jnp.where(in_grp, logits, neg)
    v1 = jnp.max(el, axis=0, keepdims=True)
    i1 = jnp.min(jnp.where(el == v1, ri_f, big), axis=0, keepdims=True)
    el2 = jnp.where(ri_f == i1, neg, el)
    v2 = jnp.max(el2, axis=0, keepdims=True)
    i2 = jnp.min(jnp.where(el2 == v2, ri_f, big), axis=0, keepdims=True)
    p = jnp.exp(v2 - v1)
    s1 = 1.0 / (1.0 + p)
    gate1 = gw * s1
    gate2 = gw * (p * s1)
    r8 = lax.broadcasted_iota(I32, (SUBLANES, T), 0)
    e_ref[...] = jnp.where(r8 == 0, i1 - N_GROUPS, jnp.where(r8 == 1, i2 - N_GROUPS, 0.0)).astype(I32)
    rl = lax.broadcasted_iota(I32, (LANES, T), 0)
    g_ref[...] = jnp.where(rl == 0, gate1, jnp.where(rl == 1, gate2, 0.0)).T


def _mixer_call(x, gmix, wa, wb, wdt, convw, convb, dtb, alog, dskip, ssdn, scw, scn, wout, gffn,
                wr, tri, e3, gsum):
    B, L, D = x.shape
    T = T_MIX
    nt = L // T
    n_tok = B * L

    def const(shape):
        return pl.BlockSpec(shape, lambda b, t: (0,) * len(shape), pipeline_mode=pl.Buffered(1))

    in_specs = [
        pl.BlockSpec((1, T, D), lambda b, t: (b, t, 0)),
        const((1, D)),
        const(wa.shape), const(wb.shape), const(wdt.shape), const(convw.shape), const(convb.shape),
        const(dtb.shape), const(alog.shape), const(dskip.shape), const(ssdn.shape),
        const(scw.shape), const(scn.shape), const(wout.shape), const(gffn.shape),
        const(wr.shape), const(tri.shape), const(e3.shape), const(gsum.shape),
    ]
    out_shape = [
        jax.ShapeDtypeStruct((B, L, D), F32),
        jax.ShapeDtypeStruct((n_tok // 2 * ROW_TILES, LANES), jnp.uint32),
        jax.ShapeDtypeStruct((8, n_tok), I32),
        jax.ShapeDtypeStruct((n_tok, LANES), F32),
    ]
    out_specs = [
        pl.BlockSpec((1, T, D), lambda b, t: (b, t, 0)),
        pl.BlockSpec((PAIR_ROWS * ROW_TILES, LANES), lambda b, t: (b * nt + t, 0)),
        pl.BlockSpec((8, T), lambda b, t: (0, b * nt + t)),
        pl.BlockSpec((T, LANES), lambda b, t: (b * nt + t, 0)),
    ]
    return pl.pallas_call(
        _mixer_kernel,
        grid=(B, nt),
        in_specs=in_specs,
        out_specs=out_specs,
        out_shape=out_shape,
        scratch_shapes=[
            pltpu.VMEM((T + 8, XBC), F32),
            pltpu.VMEM((T + 8, SC_WIDTH), F32),
            pltpu.VMEM((N_BC_GROUPS, STATE, 512), F32),
            pltpu.VMEM((ROW_TILES * T, LANES), jnp.uint32),
        ],
        compiler_params=pltpu.CompilerParams(
            dimension_semantics=("arbitrary", "arbitrary"),
            vmem_limit_bytes=VMEM_LIMIT),
        name="mixer",
    )(x, gmix, wa, wb, wdt, convw, convb, dtb, alog, dskip, ssdn, scw, scn, wout, gffn, wr,
      tri, e3, gsum)


def _moe_kernel(bstart_ref, nblk_ref, pk_ref, hn2p_ref, wg_ref, wu_ref, wd_ref,
                y2_ref, hn2v, xbuf, ybuf, wgu, wdb, idx_ref, lsem, ssem, isem, *, n_tok):
    R = R_BLK
    slab = R * ROW_TILES
    e = pl.program_id(0)
    g0 = bstart_ref[e]
    nb_e = nblk_ref[e]

    def idx_word(sl, field, j):
        return idx_ref[sl, 2 * field + j // LANES, j % LANES]

    def idx_copy(g, sl):
        return pltpu.make_async_copy(pk_ref.at[g], idx_ref.at[sl], isem.at[sl])

    def gather_rows(sl, xs):
        for j in range(R):
            row = pl.multiple_of(idx_word(sl, 0, j), ROW_TILES)
            sh = idx_word(sl, 1, j).astype(jnp.uint32)
            w = hn2v[pl.ds(row, ROW_TILES), :]
            w = jnp.left_shift(jnp.right_shift(w, sh), jnp.uint32(16))
            xbuf[pl.ds(xs * slab + j * ROW_TILES, ROW_TILES), :] = pltpu.bitcast(w, F32)

    yslab = R * Y_TILES
    FH = D_FF // 2

    def y_copy(sl, j, dst_row):
        return pltpu.make_async_copy(
            ybuf.at[pl.ds(sl * yslab + j * Y_TILES, Y_TILES), :],
            y2_ref.at[pl.ds(pl.multiple_of(dst_row, Y_TILES), Y_TILES), :],
            ssem.at[sl])

    def scatter_start(sl, lo, hi):
        for j in range(lo, hi):
            y_copy(sl, j, idx_word(sl, 2, j)).start(priority=j % 2)

    def slab_wait(sl):
        view = ybuf.at[pl.ds(sl * yslab, yslab), :]
        pltpu.make_async_copy(view, view, ssem.at[sl]).wait()

    n_idx_blocks = pk_ref.shape[0]

    @pl.when(e == 0)
    def _():
        load = pltpu.make_async_copy(hn2p_ref, hn2v, lsem.at[0])
        load.start()
        first = idx_copy(0, 0)
        first.start()
        idx_copy(1, 1).start()
        prev = idx_copy(n_idx_blocks - 1, N_SLOTS - 1)
        prev.start()
        ybuf[...] = jnp.zeros(ybuf.shape, ybuf.dtype)
        for sl in range(N_SLOTS - 1):
            base = 2 * n_tok + (sl + 1 if sl else 0) * R
            for j in range(R):
                y_copy(sl, j, (base + j) * Y_TILES).start(priority=j % 2)
        prev.wait()
        first.wait()
        load.wait()
        gather_rows(0, 0)

    @pl.when(nb_e > 0)
    def _():
        for j in range(2):
            wgu[:, 2 * j * FH:(2 * j + 1) * FH] = wg_ref[0, :, j * FH:(j + 1) * FH].astype(BF16)
            wgu[:, (2 * j + 1) * FH:(2 * j + 2) * FH] = wu_ref[0, :, j * FH:(j + 1) * FH].astype(BF16)
        wdb[...] = wd_ref[0].astype(BF16)

    c1, c2 = R // 3, 2 * (R // 3)

    def run_block(g, sl):
        prv = (sl + N_SLOTS - 1) % N_SLOTS
        nx1 = (sl + 1) % N_SLOTS
        nx2 = (sl + 2) % N_SLOTS
        idx_copy(g + 1, nx1).wait()
        idx_copy(g + 2, nx2).start()
        slab_wait(sl)
        x = jnp.concatenate(
            [xbuf[pl.ds((sl % 2) * slab + s, R, stride=ROW_TILES), :] for s in range(ROW_TILES)],
            axis=1).astype(BF16)
        scatter_start(prv, 0, c1)
        gu = _dot(x, wgu[...])
        gather_rows(nx1, (sl + 1) % 2)
        scatter_start(prv, c1, c2)
        act = jnp.concatenate(
            [_silu(gu[:, 2 * j * FH:(2 * j + 1) * FH]) * gu[:, (2 * j + 1) * FH:(2 * j + 2) * FH]
             for j in range(2)], axis=1).astype(BF16)
        scatter_start(prv, c2, R)
        yy = _dot(act, wdb[...])
        for s in range(Y_TILES):
            lo_w = pltpu.bitcast(yy[:, 2 * LANES * s:2 * LANES * s + LANES].astype(BF16).astype(F32), jnp.uint32)
            hi_w = pltpu.bitcast(yy[:, 2 * LANES * s + LANES:2 * LANES * (s + 1)].astype(BF16).astype(F32),
                                 jnp.uint32)
            ybuf[pl.ds(sl * yslab + s, R, stride=Y_TILES), :] = jnp.bitwise_or(
                jnp.right_shift(lo_w, jnp.uint32(16)), hi_w)

    def block(i, carry):
        g = g0 + i
        sl = lax.rem(g, N_SLOTS)
        for k in range(N_SLOTS):
            @pl.when(sl == k)
            def _(k=k):
                run_block(g, k)
        return carry

    lax.fori_loop(0, nb_e, block, 0)

    @pl.when(e == pl.num_programs(0) - 1)
    def _():
        g_end = g0 + nb_e
        sl_end = lax.rem(g_end, N_SLOTS)
        for k in range(N_SLOTS):
            @pl.when(sl_end == k)
            def _(k=k):
                scatter_start((k + N_SLOTS - 1) % N_SLOTS, 0, R)
                idx_copy(g_end + 1, (k + 1) % N_SLOTS).wait()
        for k in range(N_SLOTS):
            slab_wait(k)


def _moe_call(bstart, nblk, pk, hn2p, wg, wu, wd, n_tok):
    R = R_BLK
    slab = R * ROW_TILES
    grid_spec = pltpu.PrefetchScalarGridSpec(
        num_scalar_prefetch=2,
        grid=(N_EXPERTS,),
        in_specs=[
            pl.BlockSpec(memory_space=pl.ANY),
            pl.BlockSpec(memory_space=pl.ANY),
            pl.BlockSpec((1, D_MODEL, D_FF), lambda e, bs, nb: (e, 0, 0)),
            pl.BlockSpec((1, D_MODEL, D_FF), lambda e, bs, nb: (e, 0, 0)),
            pl.BlockSpec((1, D_FF, D_MODEL), lambda e, bs, nb: (e, 0, 0)),
        ],
        out_specs=pl.BlockSpec(memory_space=pl.ANY),
        scratch_shapes=[
            pltpu.VMEM(hn2p.shape, jnp.uint32),
            pltpu.VMEM((2 * slab, LANES), F32),
            pltpu.VMEM((N_SLOTS * R * Y_TILES, LANES), jnp.uint32),
            pltpu.VMEM((D_MODEL, 2 * D_FF), BF16),
            pltpu.VMEM((D_FF, D_MODEL), BF16),
            pltpu.SMEM((N_SLOTS, SUBLANES, LANES), I32),
            pltpu.SemaphoreType.DMA((1,)),
            pltpu.SemaphoreType.DMA((N_SLOTS,)),
            pltpu.SemaphoreType.DMA((N_SLOTS,)),
        ],
    )
    return pl.pallas_call(
        functools.partial(_moe_kernel, n_tok=n_tok),
        grid_spec=grid_spec,
        out_shape=jax.ShapeDtypeStruct(((2 * n_tok + N_SLOTS * R) * Y_TILES, LANES), jnp.uint32),
        compiler_params=pltpu.CompilerParams(
            dimension_semantics=("arbitrary",),
            vmem_limit_bytes=VMEM_LIMIT),
        name="moe",
    )(bstart, nblk, pk, hn2p, wg, wu, wd)


def _unpack_rows(y_ref, rows):
    parts = []
    for s in range(Y_TILES):
        w = y_ref[pl.ds(s, rows, stride=Y_TILES), :]
        parts.append(pltpu.bitcast(jnp.left_shift(w, jnp.uint32(16)), F32))
        parts.append(pltpu.bitcast(jnp.bitwise_and(w, jnp.uint32(0xFFFF0000)), F32))
    return jnp.concatenate(parts, axis=1)


def _combine_kernel(h_ref, y0_ref, y1_ref, g_ref, fn_ref, o_ref):
    T = T_CMB
    h = h_ref[...]
    y0 = _unpack_rows(y0_ref, T)
    y1 = _unpack_rows(y1_ref, T)
    g = g_ref[...]
    v = h + (y0 * g[:, 0:1] + y1 * g[:, 1:2])
    ms = jnp.mean(v * v, axis=-1, keepdims=True)
    o_ref[...] = v * lax.rsqrt(ms + EPS) * fn_ref[...]


def _combine_call(h2d, y2, gates, fnorm):
    n_tok, D = h2d.shape
    T = T_CMB
    nt = n_tok // T
    return pl.pallas_call(
        _combine_kernel,
        grid=(nt,),
        in_specs=[
            pl.BlockSpec((T, D), lambda i: (i, 0)),
            pl.BlockSpec((T * Y_TILES, LANES), lambda i: (i, 0)),
            pl.BlockSpec((T * Y_TILES, LANES), lambda i: (nt + i, 0)),
            pl.BlockSpec((T, LANES), lambda i: (i, 0)),
            pl.BlockSpec((1, D), lambda i: (0, 0)),
        ],
        out_specs=pl.BlockSpec((T, D), lambda i: (i, 0)),
        out_shape=jax.ShapeDtypeStruct((n_tok, D), F32),
        compiler_params=pltpu.CompilerParams(dimension_semantics=("arbitrary",)),
        name="combine",
    )(h2d, y2, y2, gates, fnorm)


def _plan_kernel(e_ref, ux_ref, ones_ref, lx_ref, dest_ref, bstart_ref, nblk_ref):
    rows = e_ref.shape[0]
    ev = e_ref[...]
    lane8 = lax.broadcasted_iota(I32, (SUBLANES, LANES), 1)
    dest = jnp.zeros((rows, LANES), F32)
    bstart = jnp.zeros((SUBLANES, LANES), I32)
    nblk = jnp.zeros((SUBLANES, LANES), I32)
    pstart = jnp.zeros((1, LANES), F32)
    for e in range(N_EXPERTS):
        ohb = ev == e
        oh = jnp.where(ohb, 1.0, 0.0).astype(BF16)
        within = _dot(oh, ux_ref[...])
        rtot = _dot(oh, ones_ref[...])
        rpre = _dot(lx_ref[...], rtot.astype(BF16))
        cnt = rpre[rows - 1:rows, :] + rtot[rows - 1:rows, :]
        dest = dest + jnp.where(ohb, within + rpre + pstart, 0.0)
        cnt_i = cnt.astype(I32)
        nb_e = lax.shift_right_logical(cnt_i + (R_BLK - 1), R_SHIFT)
        bstart = jnp.where(lane8 == e, lax.shift_right_logical(pstart.astype(I32), R_SHIFT), bstart)
        nblk = jnp.where(lane8 == e, nb_e, nblk)
        pstart = pstart + lax.shift_left(nb_e, R_SHIFT).astype(F32)
    dest_ref[...] = dest.astype(I32)
    bstart_ref[...] = bstart
    nblk_ref[...] = nblk


def _invert_kernel(dest_ref, init_ref, inv_ref, buf0, buf1, sems):
    n_asg = dest_ref.shape[0]
    chunk = buf0.shape[0]
    n_chunks = n_asg // chunk
    bufs = (buf0, buf1)

    def chunk_copy(k):
        return pltpu.make_async_copy(dest_ref.at[pl.ds(k * chunk, chunk)], bufs[k % 2], sems.at[k % 2])

    init_copy = pltpu.make_async_copy(init_ref, inv_ref, sems.at[2])
    init_copy.start()
    chunk_copy(0).start()
    init_copy.wait()
    for k in range(n_chunks):
        if k + 1 < n_chunks:
            chunk_copy(k + 1).start()
        chunk_copy(k).wait()
        buf = bufs[k % 2]

        def put(i, c, buf=buf, base=k * chunk):
            inv_ref[buf[i]] = base + i
            return c

        lax.fori_loop(0, chunk, put, 0, unroll=16)


def _index_tiles_kernel(inv_ref, pk_ref, *, n_tok):
    nb = pk_ref.shape[0] // SUBLANES
    per_blk = R_BLK // LANES
    assert per_blk == 2
    pk_ref[...] = jnp.zeros(pk_ref.shape, I32)
    for half in range(per_blk):
        a = inv_ref[pl.ds(half, nb, stride=per_blk), :]
        t = jnp.bitwise_and(a, n_tok - 1)
        pk_ref[pl.ds(half, nb, stride=SUBLANES), :] = lax.shift_left(lax.shift_right_logical(t, 1), 3)
        pk_ref[pl.ds(2 + half, nb, stride=SUBLANES), :] = lax.shift_left(jnp.bitwise_and(a, 1), 4)
        pk_ref[pl.ds(4 + half, nb, stride=SUBLANES), :] = a * Y_TILES


def _plan(e_rows, n_tok, nb):
    R = R_BLK
    n_asg = 2 * n_tok
    rows = n_asg // LANES
    emat = e_rows[0:2].reshape(rows, LANES)
    li = jnp.arange(LANES)
    ux = (li[:, None] < li[None, :]).astype(BF16)
    ones = jnp.ones((LANES, LANES), BF16)
    ri = jnp.arange(rows)
    lx = (ri[:, None] > ri[None, :]).astype(BF16)
    dest, bstart, nblk = pl.pallas_call(
        _plan_kernel,
        out_shape=[jax.ShapeDtypeStruct((rows, LANES), I32),
                   jax.ShapeDtypeStruct((SUBLANES, LANES), I32),
                   jax.ShapeDtypeStruct((SUBLANES, LANES), I32)],
        name="plan",
    )(emat, ux, ones, lx)
    pos = jnp.arange(nb * R, dtype=I32)
    init = n_asg + (pos & (R - 1)) + jnp.where(pos >= (nb - 1) * R, R, 0)
    assert n_asg % INV_CHUNK == 0
    inv = pl.pallas_call(
        _invert_kernel,
        in_specs=[pl.BlockSpec(memory_space=pl.ANY), pl.BlockSpec(memory_space=pl.ANY)],
        out_specs=pl.BlockSpec(memory_space=pltpu.SMEM),
        out_shape=jax.ShapeDtypeStruct((nb * R,), I32),
        scratch_shapes=[pltpu.SMEM((INV_CHUNK,), I32), pltpu.SMEM((INV_CHUNK,), I32),
                        pltpu.SemaphoreType.DMA((3,))],
        name="invert",
    )(dest.reshape(n_asg), init)
    pk = pl.pallas_call(
        functools.partial(_index_tiles_kernel, n_tok=n_tok),
        out_shape=jax.ShapeDtypeStruct((nb * SUBLANES, LANES), I32),
        name="index_tiles",
    )(inv.reshape(nb * R // LANES, LANES))
    return bstart[0, 0:N_EXPERTS], nblk[0, 0:N_EXPERTS], pk.reshape(nb, SUBLANES, LANES)


def kernel(x, norm_mix, w_in, ssd_conv_w, ssd_conv_b, dt_bias, a_log, d_skip, ssd_norm, sc_conv_w,
           sc_norm, w_out, norm_ffn, w_router_group, w_router_expert, w_gate, w_up, w_down, final_norm):
    B, L, D = x.shape
    n_tok = B * L
    depth = norm_mix.shape[0]
    assert depth == 1 and D == D_MODEL and (n_tok & (n_tok - 1)) == 0
    nb = -(-((2 * n_tok) // R_BLK + N_EXPERTS + 1) // SUBLANES) * SUBLANES

    o1 = SSD_WIDTH
    o2 = o1 + XBC
    o3 = o2 + N_HEADS
    wi = w_in.reshape(D, -1)
    wit = wi.T
    wa = wit[0:o2].astype(BF16)
    wb = wit[o3:].astype(BF16)
    wdt = jnp.pad(wit[o2:o3], ((0, LANES - N_HEADS), (0, 0))).astype(BF16)
    pad_h = (0, LANES - N_HEADS)
    dtb = jnp.pad(dt_bias[0], pad_h).reshape(1, LANES)
    alog = jnp.pad(a_log[0], pad_h).reshape(1, LANES)
    dskip = jnp.repeat(d_skip[0], HEAD_DIM).reshape(1, SSD_WIDTH)

    wre = jnp.transpose(w_router_expert[0], (1, 0, 2)).reshape(D, N_EXPERTS)
    wrt = jnp.pad(jnp.concatenate([w_router_group[0], wre], axis=1).T,
                  ((0, ROUTER_ROWS - N_GROUPS - N_EXPERTS), (0, 0)))
    wrt_hi = wrt.astype(BF16)
    wr = jnp.concatenate([wrt_hi, (wrt - wrt_hi.astype(F32)).astype(BF16)], axis=0)

    ri = jnp.arange(T_MIX)
    tri = (ri[:, None] >= ri[None, :]).astype(BF16)
    er = jnp.arange(LANES)
    ec = jnp.arange(SSD_WIDTH)
    e3 = ((er[:, None] < 48) & ((er[:, None] % 16) == (ec[None, :] // HEAD_DIM))).astype(BF16)
    gsum = ((ec[:, None] // HEAD_DIM) == er[None, :]).astype(BF16)

    h, hn2, e_rows, gates = _mixer_call(
        x, norm_mix[0].reshape(1, D), wa, wb, wdt, ssd_conv_w[0], ssd_conv_b[0].reshape(1, XBC), dtb, alog,
        dskip, ssd_norm[0].reshape(1, SSD_WIDTH), sc_conv_w[0], sc_norm[0].reshape(1, SC_WIDTH),
        w_out.reshape(-1, D).astype(BF16), norm_ffn[0].reshape(1, D), wr, tri, e3, gsum)

    bstart, nblk, pk = _plan(e_rows, n_tok, nb)
    y2 = _moe_call(bstart, nblk, pk, hn2, w_gate.reshape(N_EXPERTS, D, D_FF),
                   w_up.reshape(N_EXPERTS, D, D_FF), w_down.reshape(N_EXPERTS, D_FF, D), n_tok)
    out = _combine_call(h.reshape(n_tok, D), y2, gates, final_norm.reshape(1, D))
    return out.reshape(B, L, D)
```

```python
import functools

import jax
import jax.numpy as jnp
from jax import lax
from jax.experimental import pallas as pl
from jax.experimental.pallas import tpu as pltpu

F32 = jnp.float32
BF16 = jnp.bfloat16
I32 = jnp.int32

EPS = 1e-6
D_MODEL = 1024
N_HEADS = 16
HEAD_DIM = 64
N_BC_GROUPS = 2
STATE = 128
SSD_WIDTH = 1024
XBC = SSD_WIDTH + 2 * N_BC_GROUPS * STATE
SC_WIDTH = 1024
SC_GROUPS = 16
N_GROUPS = 4
EPG = 8
N_EXPERTS = 32
D_FF = 512

LANES = 128
SUBLANES = 8
ROW_TILES = D_MODEL // LANES
Y_TILES = ROW_TILES // 2

T_MIX = 512
Q_SSD = 128
COL_CHUNK = 512
PAIR_ROWS = T_MIX // 2
R_BLK = 256
R_SHIFT = 8
ROUTER_ROWS = 48
N_SLOTS = 4
T_CMB = 1024

VMEM_LIMIT = 56 * 1024 * 1024


def _dot(a, b):
    return jnp.dot(a, b, preferred_element_type=F32)


def _dot_t(a, b):
    return lax.dot_general(a, b, (((1,), (1,)), ((), ())), preferred_element_type=F32)


def _split3(v):
    p1 = v.astype(BF16).astype(F32)
    r1 = v - p1
    p2 = r1.astype(BF16).astype(F32)
    p3 = (r1 - p2).astype(BF16).astype(F32)
    return p1, p2, p3


def _pack3(v):
    p1, p2, p3 = _split3(v)
    return (p1 + pltpu.roll(p2, 16, 1) + pltpu.roll(p3, 32, 1)).astype(BF16)


def _silu(v):
    return v * jax.nn.sigmoid(v)


def _mixer_kernel(x_ref, gmix_ref, wa_ref, wb_ref, wdt_ref, convw_ref, convb_ref, dtb_ref, alog_ref,
                  dskip_ref, ssdn_ref, scw_ref, scn_ref, wout_ref, gffn_ref, wr_ref,
                  tri_ref, e3_ref, gsum_ref,
                  h_ref, hn2_ref, e_ref, g_ref,
                  cbuf, sbuf, st_ref, pbuf):
    T = T_MIX
    t = pl.program_id(1)

    @pl.when(t == 0)
    def _():
        cbuf[0:8, :] = jnp.zeros((8, XBC), F32)
        sbuf[0:8, :] = jnp.zeros((8, SC_WIDTH), F32)
        st_ref[...] = jnp.zeros(st_ref.shape, F32)

    @pl.when(t > 0)
    def _():
        cbuf[0:8, :] = cbuf[T:T + 8, :]
        sbuf[0:8, :] = sbuf[T:T + 8, :]

    x = x_ref[0]
    ms = jnp.mean(x * x, axis=-1, keepdims=True)
    hn = (x * lax.rsqrt(ms + EPS) * gmix_ref[...]).astype(BF16)

    dt_raw = _dot_t(hn, wdt_ref[...])

    cw = convw_ref[...]
    xact_parts = []
    for c0 in range(0, XBC, COL_CHUNK):
        c1 = c0 + COL_CHUNK
        xbc = _dot_t(hn, wa_ref[1024 + c0:1024 + c1, :])
        cbuf[8:8 + T, c0:c1] = xbc
        acc = convb_ref[:, c0:c1] + cw[3:4, c0:c1] * xbc
        for k in range(3):
            acc = acc + cw[k:k + 1, c0:c1] * cbuf[5 + k:5 + k + T, c0:c1]
        xact_parts.append(_silu(acc))
    xact = jnp.concatenate(xact_parts, axis=1)
    xs = xact[:, 0:SSD_WIDTH]

    lane = lax.broadcasted_iota(I32, (1, LANES), 1)
    hmask = lane < N_HEADS
    a = jnp.where(hmask, -jnp.exp(alog_ref[...]), 0.0)
    dtv = dt_raw + dtb_ref[...]
    dt = jnp.where(hmask, jnp.maximum(dtv, 0.0) + jnp.log1p(jnp.exp(-jnp.abs(dtv))), 0.0)
    adt = dt * a

    sw = scw_ref[...]
    sc_parts = []
    for c0 in range(0, SC_WIDTH, COL_CHUNK):
        c1 = c0 + COL_CHUNK
        u = _dot_t(hn, wb_ref[1024 + c0:1024 + c1, :]) * _dot_t(hn, wb_ref[2048 + c0:2048 + c1, :])
        sbuf[8:8 + T, c0:c1] = u
        conv = (sw[2:3, c0:c1] * u + sw[1:2, c0:c1] * sbuf[7:7 + T, c0:c1]
                + sw[0:1, c0:c1] * sbuf[6:6 + T, c0:c1])
        sc_parts.append(_dot_t(hn, wb_ref[c0:c1, :]) * conv)
    sc = jnp.concatenate(sc_parts, axis=1)
    z_lo = _dot_t(hn, wa_ref[0:COL_CHUNK, :])
    gs = _dot((sc * sc).astype(BF16), gsum_ref[...])
    z_hi = _dot_t(hn, wa_ref[COL_CHUNK:2 * COL_CHUNK, :])
    rstd = jnp.where(lax.broadcasted_iota(I32, (1, LANES), 1) < SC_GROUPS,
                     lax.rsqrt(gs * (1.0 / HEAD_DIM) + EPS), 0.0)
    rstd_e = _dot(_pack3(rstd), e3_ref[...])
    sc_out = (sc * rstd_e * scn_ref[...]).astype(BF16)
    z = jnp.concatenate([z_lo, z_hi], axis=1)

    c3 = _dot(tri_ref[...], _pack3(adt))
    ac = jnp.where(hmask, c3 + pltpu.roll(c3, LANES - 16, 1) + pltpu.roll(c3, LANES - 32, 1), 0.0)
    Q = Q_SSD
    n_sub = T // Q
    ends = [ac[(c + 1) * Q - 1:(c + 1) * Q, :] for c in range(n_sub)]
    base = jnp.concatenate([jnp.broadcast_to(ends[c - 1] if c else jnp.zeros_like(ends[0]), (Q, LANES))
                            for c in range(n_sub)], axis=0)
    endv = jnp.concatenate([jnp.broadcast_to(ends[c], (Q, LANES)) for c in range(n_sub)], axis=0)
    eac = jnp.where(hmask, jnp.exp(ac - base), 0.0)
    wdt = dt * jnp.exp(endv - ac)

    stacked = jnp.concatenate([_pack3(dt), _pack3(wdt), _pack3(eac)], axis=0)
    ex = _dot(stacked, e3_ref[...])
    dt_e = ex[0:T]
    wdt_e = ex[T:2 * T]
    eac_e = ex[2 * T:3 * T]
    xdt = (xs * dt_e).astype(BF16)
    xdtw = (xs * wdt_e).astype(BF16)

    mix_sc = []
    mix_todo = list(range(0, D_MODEL, COL_CHUNK))

    def emit_mix():
        if mix_todo:
            c0 = mix_todo.pop(0)
            mix_sc.append(_dot(sc_out, wout_ref[SSD_WIDTH:SSD_WIDTH + SC_WIDTH, c0:c0 + COL_CHUNK]))

    ac_rows = ac.T
    rr = lax.broadcasted_iota(I32, (Q, Q), 0)
    cc = lax.broadcasted_iota(I32, (Q, Q), 1)
    causal = rr >= cc
    cblk = lax.shift_right_logical(lax.broadcasted_iota(I32, (Q, 4 * HEAD_DIM), 1), 6)

    y_cols = []
    for g in range(N_BC_GROUPS):
        bg = xact[:, SSD_WIDTH + STATE * g:SSD_WIDTH + STATE * (g + 1)]
        cg = xact[:, SSD_WIDTH + 2 * STATE + STATE * g:SSD_WIDTH + 2 * STATE + STATE * (g + 1)]
        st = st_ref[g]
        y_rows = [[], []]
        for c in range(n_sub):
            r0, r1 = c * Q, (c + 1) * Q
            bb = bg[r0:r1].astype(BF16)
            cbf = cg[r0:r1].astype(BF16)
            cb = lax.dot_general(cbf, bb, (((1,), (1,)), ((), ())), preferred_element_type=F32)
            yoff = _dot(cbf, st.astype(BF16))
            dec = eac_e[r1 - 1:r1, 512 * g:512 * (g + 1)]
            bgt = bg[r0:r1].T.astype(BF16)
            st = st * dec + _dot(bgt, xdtw[r0:r1, 512 * g:512 * (g + 1)])
            for q in range(2):
                ms_list = []
                for r in range(4):
                    hh = 8 * g + 4 * q + r
                    seg = ac[r0:r1, hh:hh + 1] - ac_rows[hh:hh + 1, r0:r1]
                    lh = jnp.exp(jnp.where(causal, seg, -jnp.inf))
                    ms_list.append((cb * lh).astype(BF16))
                lhs = jnp.concatenate(ms_list, axis=1)
                if c == 1 and q == 0:
                    emit_mix()
                lo = 512 * g + 256 * q
                x4 = xdt[r0:r1, lo:lo + 256]
                rhs = jnp.concatenate(
                    [jnp.where(cblk == r, x4, jnp.zeros_like(x4)) for r in range(4)], axis=0)
                yd = _dot(lhs, rhs)
                y_rows[q].append(yd + eac_e[r0:r1, lo:lo + 256] * yoff[:, 256 * q:256 * (q + 1)])
        st_ref[g] = st
        y_cols += [jnp.concatenate(y_rows[0], axis=0), jnp.concatenate(y_rows[1], axis=0)]
    y = jnp.concatenate(y_cols, axis=1) + dskip_ref[...] * xs

    while mix_todo:
        emit_mix()
    v = y * _silu(z)
    outs = []
    for g in range(N_BC_GROUPS):
        vg = v[:, 512 * g:512 * (g + 1)]
        msg = jnp.mean(vg * vg, axis=-1, keepdims=True)
        outs.append(vg * lax.rsqrt(msg + EPS))
    ssd_out = (jnp.concatenate(outs, axis=1) * ssdn_ref[...]).astype(BF16)

    mix = _dot(ssd_out, wout_ref[0:SSD_WIDTH, :]) + jnp.concatenate(mix_sc, axis=1)
    h = x + mix
    h_ref[0] = h

    ms2 = jnp.mean(h * h, axis=-1, keepdims=True)
    hn2 = h * lax.rsqrt(ms2 + EPS) * gffn_ref[...]
    hi = hn2.astype(BF16)
    bits = pltpu.bitcast(hi.astype(F32), jnp.uint32)
    for s in range(ROW_TILES):
        pbuf[pl.ds(s * T, T), :] = bits[:, LANES * s:LANES * (s + 1)]
    for s in range(ROW_TILES):
        even = pbuf[pl.ds(s * T, PAIR_ROWS, stride=2), :]
        odd = pbuf[pl.ds(s * T + 1, PAIR_ROWS, stride=2), :]
        hn2_ref[pl.ds(s, PAIR_ROWS, stride=ROW_TILES), :] = jnp.bitwise_or(
            jnp.right_shift(even, jnp.uint32(16)), odd)
    lo_ = (hn2 - hi.astype(F32)).astype(BF16)
    RT = ROUTER_ROWS
    nt_dims = (((1,), (1,)), ((), ()))
    both = lax.dot_general(wr_ref[...], hi, nt_dims, preferred_element_type=F32)
    low = lax.dot_general(wr_ref[0:RT, :], lo_, nt_dims, preferred_element_type=F32)
    logits = both[0:RT] + (low + both[RT:2 * RT])

    ri = lax.broadcasted_iota(I32, (RT, T), 0)
    ri_f = ri.astype(F32)
    neg = -jnp.inf
    big = 1e9
    gl = jnp.where(ri < N_GROUPS, logits, neg)
    gmax = jnp.max(gl, axis=0, keepdims=True)
    gidx = jnp.min(jnp.where(gl == gmax, ri_f, big), axis=0, keepdims=True)
    gsum = jnp.sum(jnp.where(ri < N_GROUPS, jnp.exp(logits - gmax), 0.0), axis=0, keepdims=True)
    gw = 1.0 / gsum
    egrp = lax.shift_right_logical(ri - N_GROUPS, 3).astype(F32)
    in_grp = (ri >= N_GROUPS) & (ri < N_GROUPS + N_EXPERTS) & (egrp == gidx)
    el = jnp.where(in_grp, logits, neg)
    v1 = jnp.max(el, axis=0, keepdims=True)
    i1 = jnp.min(jnp.where(el == v1, ri_f, big), axis=0, keepdims=True)
    el2 = jnp.where(ri_f == i1, neg, el)
    v2 = jnp.max(el2, axis=0, keepdims=True)
    i2 = jnp.min(jnp.where(el2 == v2, ri_f, big), axis=0, keepdims=True)
    p = jnp.exp(v2 - v1)
    s1 = 1.0 / (1.0 + p)
    gate1 = gw * s1
    gate2 = gw * (p * s1)
    r8 = lax.broadcasted_iota(I32, (SUBLANES, T), 0)
    e_ref[...] = jnp.where(r8 == 0, i1 - N_GROUPS, jnp.where(r8 == 1, i2 - N_GROUPS, 0.0)).astype(I32)
    rl = lax.broadcasted_iota(I32, (LANES, T), 0)
    g_ref[...] = jnp.where(rl == 0, gate1, jnp.where(rl == 1, gate2, 0.0)).T


def _mixer_call(x, gmix, wa, wb, wdt, convw, convb, dtb, alog, dskip, ssdn, scw, scn, wout, gffn,
                wr, tri, e3, gsum):
    B, L, D = x.shape
    T = T_MIX
    nt = L // T
    n_tok = B * L

    def const(shape):
        return pl.BlockSpec(shape, lambda b, t: (0,) * len(shape), pipeline_mode=pl.Buffered(1))

    in_specs = [
        pl.BlockSpec((1, T, D), lambda b, t: (b, t, 0)),
        const((1, D)),
        const(wa.shape), const(wb.shape), const(wdt.shape), const(convw.shape), const(convb.shape),
        const(dtb.shape), const(alog.shape), const(dskip.shape), const(ssdn.shape),
        const(scw.shape), const(scn.shape), const(wout.shape), const(gffn.shape),
        const(wr.shape), const(tri.shape), const(e3.shape), const(gsum.shape),
    ]
    out_shape = [
        jax.ShapeDtypeStruct((B, L, D), F32),
        jax.ShapeDtypeStruct((n_tok // 2 * ROW_TILES, LANES), jnp.uint32),
        jax.ShapeDtypeStruct((8, n_tok), I32),
        jax.ShapeDtypeStruct((n_tok, LANES), F32),
    ]
    out_specs = [
        pl.BlockSpec((1, T, D), lambda b, t: (b, t, 0)),
        pl.BlockSpec((PAIR_ROWS * ROW_TILES, LANES), lambda b, t: (b * nt + t, 0)),
        pl.BlockSpec((8, T), lambda b, t: (0, b * nt + t)),
        pl.BlockSpec((T, LANES), lambda b, t: (b * nt + t, 0)),
    ]
    return pl.pallas_call(
        _mixer_kernel,
        grid=(B, nt),
        in_specs=in_specs,
        out_specs=out_specs,
        out_shape=out_shape,
        scratch_shapes=[
            pltpu.VMEM((T + 8, XBC), F32),
            pltpu.VMEM((T + 8, SC_WIDTH), F32),
            pltpu.VMEM((N_BC_GROUPS, STATE, 512), F32),
            pltpu.VMEM((ROW_TILES * T, LANES), jnp.uint32),
        ],
        compiler_params=pltpu.CompilerParams(
            dimension_semantics=("arbitrary", "arbitrary"),
            vmem_limit_bytes=VMEM_LIMIT),
        name="mixer",
    )(x, gmix, wa, wb, wdt, convw, convb, dtb, alog, dskip, ssdn, scw, scn, wout, gffn, wr,
      tri, e3, gsum)


def _moe_kernel(bstart_ref, nblk_ref, pk_ref, hn2p_ref, wg_ref, wu_ref, wd_ref,
                y2_ref, hn2v, xbuf, ybuf, wgu, wdb, idx_ref, lsem, ssem, isem, *, n_tok):
    R = R_BLK
    slab = R * ROW_TILES
    e = pl.program_id(0)
    g0 = bstart_ref[e]
    nb_e = nblk_ref[e]

    def idx_word(sl, field, j):
        return idx_ref[sl, 2 * field + j // LANES, j % LANES]

    def idx_copy(g, sl):
        return pltpu.make_async_copy(pk_ref.at[g], idx_ref.at[sl], isem.at[sl])

    def gather_rows(sl, xs):
        for j in range(R):
            row = pl.multiple_of(idx_word(sl, 0, j), ROW_TILES)
            sh = idx_word(sl, 1, j).astype(jnp.uint32)
            w = hn2v[pl.ds(row, ROW_TILES), :]
            w = jnp.left_shift(jnp.right_shift(w, sh), jnp.uint32(16))
            xbuf[pl.ds(xs * slab + j * ROW_TILES, ROW_TILES), :] = pltpu.bitcast(w, F32)

    yslab = R * Y_TILES
    FH = D_FF // 2

    def y_copy(sl, j, dst_row):
        return pltpu.make_async_copy(
            ybuf.at[pl.ds(sl * yslab + j * Y_TILES, Y_TILES), :],
            y2_ref.at[pl.ds(pl.multiple_of(dst_row, Y_TILES), Y_TILES), :],
            ssem.at[sl])

    def scatter_start(sl, lo, hi):
        for j in range(lo, hi):
            y_copy(sl, j, idx_word(sl, 2, j)).start(priority=j % 2)

    def slab_wait(sl):
        view = ybuf.at[pl.ds(sl * yslab, yslab), :]
        pltpu.make_async_copy(view, view, ssem.at[sl]).wait()

    n_idx_blocks = pk_ref.shape[0]

    @pl.when(e == 0)
    def _():
        load = pltpu.make_async_copy(hn2p_ref, hn2v, lsem.at[0])
        load.start()
        first = idx_copy(0, 0)
        first.start()
        idx_copy(1, 1).start()
        prev = idx_copy(n_idx_blocks - 1, N_SLOTS - 1)
        prev.start()
        ybuf[...] = jnp.zeros(ybuf.shape, ybuf.dtype)
        for sl in range(N_SLOTS - 1):
            base = 2 * n_tok + (sl + 1 if sl else 0) * R
            for j in range(R):
                y_copy(sl, j, (base + j) * Y_TILES).start(priority=j % 2)
        prev.wait()
        first.wait()
        load.wait()
        gather_rows(0, 0)

    @pl.when(nb_e > 0)
    def _():
        for j in range(2):
            wgu[:, 2 * j * FH:(2 * j + 1) * FH] = wg_ref[0, :, j * FH:(j + 1) * FH].astype(BF16)
            wgu[:, (2 * j + 1) * FH:(2 * j + 2) * FH] = wu_ref[0, :, j * FH:(j + 1) * FH].astype(BF16)
        wdb[...] = wd_ref[0].astype(BF16)

    c1, c2 = R // 3, 2 * (R // 3)

    def run_block(g, sl):
        prv = (sl + N_SLOTS - 1) % N_SLOTS
        nx1 = (sl + 1) % N_SLOTS
        nx2 = (sl + 2) % N_SLOTS
        idx_copy(g + 1, nx1).wait()
        idx_copy(g + 2, nx2).start()
        slab_wait(sl)
        x = jnp.concatenate(
            [xbuf[pl.ds((sl % 2) * slab + s, R, stride=ROW_TILES), :] for s in range(ROW_TILES)],
            axis=1).astype(BF16)
        scatter_start(prv, 0, c1)
        gu = _dot(x, wgu[...])
        gather_rows(nx1, (sl + 1) % 2)
        scatter_start(prv, c1, c2)
        act = jnp.concatenate(
            [_silu(gu[:, 2 * j * FH:(2 * j + 1) * FH]) * gu[:, (2 * j + 1) * FH:(2 * j + 2) * FH]
             for j in range(2)], axis=1).astype(BF16)
        scatter_start(prv, c2, R)
        yy = _dot(act, wdb[...])
        for s in range(Y_TILES):
            lo_w = pltpu.bitcast(yy[:, 2 * LANES * s:2 * LANES * s + LANES].astype(BF16).astype(F32), jnp.uint32)
            hi_w = pltpu.bitcast(yy[:, 2 * LANES * s + LANES:2 * LANES * (s + 1)].astype(BF16).astype(F32),
                                 jnp.uint32)
            ybuf[pl.ds(sl * yslab + s, R, stride=Y_TILES), :] = jnp.bitwise_or(
                jnp.right_shift(lo_w, jnp.uint32(16)), hi_w)

    def block(i, carry):
        g = g0 + i
        sl = lax.rem(g, N_SLOTS)
        for k in range(N_SLOTS):
            @pl.when(sl == k)
            def _(k=k):
                run_block(g, k)
        return carry

    lax.fori_loop(0, nb_e, block, 0)

    @pl.when(e == pl.num_programs(0) - 1)
    def _():
        g_end = g0 + nb_e
        sl_end = lax.rem(g_end, N_SLOTS)
        for k in range(N_SLOTS):
            @pl.when(sl_end == k)
            def _(k=k):
                scatter_start((k + N_SLOTS - 1) % N_SLOTS, 0, R)
                idx_copy(g_end + 1, (k + 1) % N_SLOTS).wait()
        for k in range(N_SLOTS):
            slab_wait(k)


def _moe_call(bstart, nblk, pk, hn2p, wg, wu, wd, n_tok):
    R = R_BLK
    slab = R * ROW_TILES
    grid_spec = pltpu.PrefetchScalarGridSpec(
        num_scalar_prefetch=2,
        grid=(N_EXPERTS,),
        in_specs=[
            pl.BlockSpec(memory_space=pl.ANY),
            pl.BlockSpec(memory_space=pl.ANY),
            pl.BlockSpec((1, D_MODEL, D_FF), lambda e, bs, nb: (e, 0, 0)),
            pl.BlockSpec((1, D_MODEL, D_FF), lambda e, bs, nb: (e, 0, 0)),
            pl.BlockSpec((1, D_FF, D_MODEL), lambda e, bs, nb: (e, 0, 0)),
        ],
        out_specs=pl.BlockSpec(memory_space=pl.ANY),
        scratch_shapes=[
            pltpu.VMEM(hn2p.shape, jnp.uint32),
            pltpu.VMEM((2 * slab, LANES), F32),
            pltpu.VMEM((N_SLOTS * R * Y_TILES, LANES), jnp.uint32),
            pltpu.VMEM((D_MODEL, 2 * D_FF), BF16),
            pltpu.VMEM((D_FF, D_MODEL), BF16),
            pltpu.SMEM((N_SLOTS, SUBLANES, LANES), I32),
            pltpu.SemaphoreType.DMA((1,)),
            pltpu.SemaphoreType.DMA((N_SLOTS,)),
            pltpu.SemaphoreType.DMA((N_SLOTS,)),
        ],
    )
    return pl.pallas_call(
        functools.partial(_moe_kernel, n_tok=n_tok),
        grid_spec=grid_spec,
        out_shape=jax.ShapeDtypeStruct(((2 * n_tok + N_SLOTS * R) * Y_TILES, LANES), jnp.uint32),
        compiler_params=pltpu.CompilerParams(
            dimension_semantics=("arbitrary",),
            vmem_limit_bytes=VMEM_LIMIT),
        name="moe",
    )(bstart, nblk, pk, hn2p, wg, wu, wd)


def _unpack_rows(y_ref, rows):
    parts = []
    for s in range(Y_TILES):
        w = y_ref[pl.ds(s, rows, stride=Y_TILES), :]
        parts.append(pltpu.bitcast(jnp.left_shift(w, jnp.uint32(16)), F32))
        parts.append(pltpu.bitcast(jnp.bitwise_and(w, jnp.uint32(0xFFFF0000)), F32))
    return jnp.concatenate(parts, axis=1)


def _combine_kernel(h_ref, y0_ref, y1_ref, g_ref, fn_ref, o_ref):
    T = T_CMB
    h = h_ref[...]
    y0 = _unpack_rows(y0_ref, T)
    y1 = _unpack_rows(y1_ref, T)
    g = g_ref[...]
    v = h + (y0 * g[:, 0:1] + y1 * g[:, 1:2])
    ms = jnp.mean(v * v, axis=-1, keepdims=True)
    o_ref[...] = v * lax.rsqrt(ms + EPS) * fn_ref[...]


def _combine_call(h2d, y2, gates, fnorm):
    n_tok, D = h2d.shape
    T = T_CMB
    nt = n_tok // T
    return pl.pallas_call(
        _combine_kernel,
        grid=(nt,),
        in_specs=[
            pl.BlockSpec((T, D), lambda i: (i, 0)),
            pl.BlockSpec((T * Y_TILES, LANES), lambda i: (i, 0)),
            pl.BlockSpec((T * Y_TILES, LANES), lambda i: (nt + i, 0)),
            pl.BlockSpec((T, LANES), lambda i: (i, 0)),
            pl.BlockSpec((1, D), lambda i: (0, 0)),
        ],
        out_specs=pl.BlockSpec((T, D), lambda i: (i, 0)),
        out_shape=jax.ShapeDtypeStruct((n_tok, D), F32),
        compiler_params=pltpu.CompilerParams(dimension_semantics=("arbitrary",)),
        name="combine",
    )(h2d, y2, y2, gates, fnorm)


def _plan_kernel(e_ref, ux_ref, ones_ref, lx_ref, dest_ref, bstart_ref, nblk_ref):
    rows = e_ref.shape[0]
    ev = e_ref[...]
    lane8 = lax.broadcasted_iota(I32, (SUBLANES, LANES), 1)
    dest = jnp.zeros((rows, LANES), F32)
    bstart = jnp.zeros((SUBLANES, LANES), I32)
    nblk = jnp.zeros((SUBLANES, LANES), I32)
    pstart = jnp.zeros((1, LANES), F32)
    for e in range(N_EXPERTS):
        ohb = ev == e
        oh = jnp.where(ohb, 1.0, 0.0).astype(BF16)
        within = _dot(oh, ux_ref[...])
        rtot = _dot(oh, ones_ref[...])
        rpre = _dot(lx_ref[...], rtot.astype(BF16))
        cnt = rpre[rows - 1:rows, :] + rtot[rows - 1:rows, :]
        dest = dest + jnp.where(ohb, within + rpre + pstart, 0.0)
        cnt_i = cnt.astype(I32)
        nb_e = lax.shift_right_logical(cnt_i + (R_BLK - 1), R_SHIFT)
        bstart = jnp.where(lane8 == e, lax.shift_right_logical(pstart.astype(I32), R_SHIFT), bstart)
        nblk = jnp.where(lane8 == e, nb_e, nblk)
        pstart = pstart + lax.shift_left(nb_e, R_SHIFT).astype(F32)
    dest_ref[...] = dest.astype(I32)
    bstart_ref[...] = bstart
    nblk_ref[...] = nblk


def _invert_kernel(dest_hbm, init_ref, inv_ref, dest_ref, sems):
    n_asg = dest_ref.shape[0]
    copies = (pltpu.make_async_copy(init_ref, inv_ref, sems.at[0]),
              pltpu.make_async_copy(dest_hbm, dest_ref, sems.at[1]))
    for cp in copies:
        cp.start()
    for cp in copies:
        cp.wait()

    def put(a, c):
        inv_ref[dest_ref[a]] = a
        return c

    lax.fori_loop(0, n_asg, put, 0, unroll=16)


def _index_tiles_kernel(inv_ref, pk_ref, *, n_tok):
    nb = pk_ref.shape[0] // SUBLANES
    per_blk = R_BLK // LANES
    assert per_blk == 2
    pk_ref[...] = jnp.zeros(pk_ref.shape, I32)
    for half in range(per_blk):
        a = inv_ref[pl.ds(half, nb, stride=per_blk), :]
        t = jnp.bitwise_and(a, n_tok - 1)
        pk_ref[pl.ds(half, nb, stride=SUBLANES), :] = lax.shift_left(lax.shift_right_logical(t, 1), 3)
        pk_ref[pl.ds(2 + half, nb, stride=SUBLANES), :] = lax.shift_left(jnp.bitwise_and(a, 1), 4)
        pk_ref[pl.ds(4 + half, nb, stride=SUBLANES), :] = a * Y_TILES


def _plan(e_rows, n_tok, nb):
    R = R_BLK
    n_asg = 2 * n_tok
    rows = n_asg // LANES
    emat = e_rows[0:2].reshape(rows, LANES)
    li = jnp.arange(LANES)
    ux = (li[:, None] < li[None, :]).astype(BF16)
    ones = jnp.ones((LANES, LANES), BF16)
    ri = jnp.arange(rows)
    lx = (ri[:, None] > ri[None, :]).astype(BF16)
    dest, bstart, nblk = pl.pallas_call(
        _plan_kernel,
        out_shape=[jax.ShapeDtypeStruct((rows, LANES), I32),
                   jax.ShapeDtypeStruct((SUBLANES, LANES), I32),
                   jax.ShapeDtypeStruct((SUBLANES, LANES), I32)],
        name="plan",
    )(emat, ux, ones, lx)
    pos = jnp.arange(nb * R, dtype=I32)
    init = n_asg + (pos & (R - 1)) + jnp.where(pos >= (nb - 1) * R, R, 0)
    inv = pl.pallas_call(
        _invert_kernel,
        in_specs=[pl.BlockSpec(memory_space=pl.ANY), pl.BlockSpec(memory_space=pl.ANY)],
        out_specs=pl.BlockSpec(memory_space=pltpu.SMEM),
        out_shape=jax.ShapeDtypeStruct((nb * R,), I32),
        scratch_shapes=[pltpu.SMEM((n_asg,), I32), pltpu.SemaphoreType.DMA((2,))],
        name="invert",
    )(dest.reshape(n_asg), init)
    pk = pl.pallas_call(
        functools.partial(_index_tiles_kernel, n_tok=n_tok),
        out_shape=jax.ShapeDtypeStruct((nb * SUBLANES, LANES), I32),
        name="index_tiles",
    )(inv.reshape(nb * R // LANES, LANES))
    return bstart[0, 0:N_EXPERTS], nblk[0, 0:N_EXPERTS], pk.reshape(nb, SUBLANES, LANES)


def kernel(x, norm_mix, w_in, ssd_conv_w, ssd_conv_b, dt_bias, a_log, d_skip, ssd_norm, sc_conv_w,
           sc_norm, w_out, norm_ffn, w_router_group, w_router_expert, w_gate, w_up, w_down, final_norm):
    B, L, D = x.shape
    n_tok = B * L
    depth = norm_mix.shape[0]
    assert depth == 1 and D == D_MODEL and (n_tok & (n_tok - 1)) == 0
    nb = -(-((2 * n_tok) // R_BLK + N_EXPERTS + 1) // SUBLANES) * SUBLANES

    o1 = SSD_WIDTH
    o2 = o1 + XBC
    o3 = o2 + N_HEADS
    wi = w_in.reshape(D, -1)
    wit = wi.T
    wa = wit[0:o2].astype(BF16)
    wb = wit[o3:].astype(BF16)
    wdt = jnp.pad(wit[o2:o3], ((0, LANES - N_HEADS), (0, 0))).astype(BF16)
    pad_h = (0, LANES - N_HEADS)
    dtb = jnp.pad(dt_bias[0], pad_h).reshape(1, LANES)
    alog = jnp.pad(a_log[0], pad_h).reshape(1, LANES)
    dskip = jnp.repeat(d_skip[0], HEAD_DIM).reshape(1, SSD_WIDTH)

    wre = jnp.transpose(w_router_expert[0], (1, 0, 2)).reshape(D, N_EXPERTS)
    wrt = jnp.pad(jnp.concatenate([w_router_group[0], wre], axis=1).T,
                  ((0, ROUTER_ROWS - N_GROUPS - N_EXPERTS), (0, 0)))
    wrt_hi = wrt.astype(BF16)
    wr = jnp.concatenate([wrt_hi, (wrt - wrt_hi.astype(F32)).astype(BF16)], axis=0)

    ri = jnp.arange(T_MIX)
    tri = (ri[:, None] >= ri[None, :]).astype(BF16)
    er = jnp.arange(LANES)
    ec = jnp.arange(SSD_WIDTH)
    e3 = ((er[:, None] < 48) & ((er[:, None] % 16) == (ec[None, :] // HEAD_DIM))).astype(BF16)
    gsum = ((ec[:, None] // HEAD_DIM) == er[None, :]).astype(BF16)

    h, hn2, e_rows, gates = _mixer_call(
        x, norm_mix[0].reshape(1, D), wa, wb, wdt, ssd_conv_w[0], ssd_conv_b[0].reshape(1, XBC), dtb, alog,
        dskip, ssd_norm[0].reshape(1, SSD_WIDTH), sc_conv_w[0], sc_norm[0].reshape(1, SC_WIDTH),
        w_out.reshape(-1, D).astype(BF16), norm_ffn[0].reshape(1, D), wr, tri, e3, gsum)

    bstart, nblk, pk = _plan(e_rows, n_tok, nb)
    y2 = _moe_call(bstart, nblk, pk, hn2, w_gate.reshape(N_EXPERTS, D, D_FF),
                   w_up.reshape(N_EXPERTS, D, D_FF), w_down.reshape(N_EXPERTS, D_FF, D), n_tok)
    out = _combine_call(h.reshape(n_tok, D), y2, gates, final_norm.reshape(1, D))
    return out.reshape(B, L, D)
```
